```python
import jax, jax.numpy as jnp
from jax import lax
import numpy as np

D_MODEL = 1024
BATCH = 8
SEQ = 2048
DEPTH = 1

N_META = 16
EPS = 1e-6
MLA_HEADS = 8
MLA_NOPE = 64
MLA_ROPE = 32
MLA_V = 64
Q_LORA = 384
KV_LORA = 256
ROPE_BASE = 10000.0
Q_BLOCK = 128
MLA_WIDTH = MLA_HEADS * MLA_V
HG_HEADS = 4
HG_DK = 128
HG_DV = 128
HG_CHUNK = 64
HG_FWIDTH = HG_HEADS * HG_DK
HG_WIDTH = HG_HEADS * HG_DV
N_GROUPS = 8
EXPERTS_PER_GROUP = 8
N_EXPERTS = N_GROUPS * EXPERTS_PER_GROUP
TOP_K = 2
D_EXPERT = 256
MOE_BLOCK = 128
IN_SIZES = (Q_LORA, KV_LORA, MLA_ROPE, HG_FWIDTH, HG_FWIDTH, HG_WIDTH, HG_WIDTH, D_MODEL, D_MODEL)
D_IN = sum(IN_SIZES)

kernel_name = 'hybrid_mla_hgrn2_hiermoe_block'


def rmsnorm(x, g):
    x32 = x.astype(jnp.float32)
    y = x32 * lax.rsqrt(jnp.mean(x32 * x32, axis=-1, keepdims=True) + EPS)
    return (y * g.astype(jnp.float32)).astype(x.dtype)


def rope_tables(L):
    inv = ROPE_BASE ** (-jnp.arange(0, MLA_ROPE, 2, dtype=jnp.float32) / MLA_ROPE)
    ang = jnp.arange(L, dtype=jnp.float32)[:, None] * inv[None, :]
    return jnp.cos(ang), jnp.sin(ang)


def apply_rope(x, cos, sin):
    x1, x2 = jnp.split(x.astype(jnp.float32), 2, axis=-1)
    return jnp.concatenate([x1 * cos - x2 * sin, x2 * cos + x1 * sin], axis=-1).astype(x.dtype)


def mla_branch(c_q, c_kv, k_pe, q_norm, w_uq, kv_norm, w_ukv, cos, sin):
    B, L, _ = c_q.shape
    q = (rmsnorm(c_q, q_norm) @ w_uq).reshape(B, L, MLA_HEADS, MLA_NOPE + MLA_ROPE)
    q_nope, q_pe = q[..., :MLA_NOPE], q[..., MLA_NOPE:]
    q_pe = apply_rope(q_pe, cos[:, None, :], sin[:, None, :])
    kv = (rmsnorm(c_kv, kv_norm) @ w_ukv).reshape(B, L, MLA_HEADS, MLA_NOPE + MLA_V)
    k_nope, v = kv[..., :MLA_NOPE], kv[..., MLA_NOPE:]
    k_pe = apply_rope(k_pe, cos, sin)
    k_idx = jnp.arange(L)
    scale = (MLA_NOPE + MLA_ROPE) ** -0.5

    def attend(qn, qp, q_idx):
        s = jnp.einsum('bqhd,bkhd->bhqk', qn, k_nope) + jnp.einsum('bqhd,bkd->bhqk', qp, k_pe)
        s = s.astype(jnp.float32) * scale
        s = jnp.where(k_idx[None, :] <= q_idx[:, None], s, -jnp.inf)
        p = jax.nn.softmax(s, axis=-1).astype(v.dtype)
        return jnp.einsum('bhqk,bkhd->bqhd', p, v)

    o_meta = attend(q_nope[:, :N_META], q_pe[:, :N_META], k_idx[:N_META])
    n_blk = (L - N_META) // Q_BLOCK

    def to_blocks(t):
        return jnp.moveaxis(t[:, N_META:].reshape(B, n_blk, Q_BLOCK, *t.shape[2:]), 1, 0)

    idx_blocks = jnp.arange(N_META, L).reshape(n_blk, Q_BLOCK)
    o_blocks = lax.map(lambda a: attend(*a), (to_blocks(q_nope), to_blocks(q_pe), idx_blocks))
    o_real = jnp.moveaxis(o_blocks, 0, 1).reshape(B, L - N_META, MLA_HEADS, MLA_V)
    return jnp.concatenate([o_meta, o_real], axis=1).reshape(B, L, MLA_WIDTH)


def hgrn2_branch(q_raw, f_raw, i_raw, g_raw, lb, hg_norm):
    B, L, _ = q_raw.shape
    f32 = jnp.float32
    f = lb.astype(f32) + (1.0 - lb.astype(f32)) * jax.nn.sigmoid(f_raw.astype(f32))
    log_f = jnp.log(f)
    k = 1.0 - f
    q = jax.nn.silu(q_raw.astype(f32))
    v = i_raw.astype(f32)
    pad = HG_CHUNK - N_META
    n_c = (L + pad) // HG_CHUNK

    def heads(t, d):
        t = t.reshape(B, L, HG_HEADS, d).transpose(0, 2, 1, 3)
        t = jnp.pad(t, ((0, 0), (0, 0), (pad, 0), (0, 0)))
        return jnp.moveaxis(t.reshape(B, HG_HEADS, n_c, HG_CHUNK, d), 2, 0)

    qc, kc, vc, lfc = heads(q, HG_DK), heads(k, HG_DK), heads(v, HG_DV), heads(log_f, HG_DK)
    causal = jnp.tril(jnp.ones((HG_CHUNK, HG_CHUNK), dtype=bool))

    def chunk_step(S, inp):
        q_c, k_c, v_c, lf_c = inp
        b = jnp.cumsum(lf_c, axis=2)
        o_inter = jnp.einsum('bhtk,bhkv->bhtv', q_c * jnp.exp(b), S)
        rel = b[:, :, :, None, :] - b[:, :, None, :, :]
        decay = jnp.exp(jnp.where(causal[:, :, None], rel, -jnp.inf))
        A = jnp.einsum('bhtk,bhsk,bhtsk->bhts', q_c, k_c, decay)
        o_c = o_inter + jnp.einsum('bhts,bhsv->bhtv', A, v_c)
        b_end = b[:, :, -1:, :]
        S_new = jnp.exp(b_end[:, :, 0, :, None]) * S + jnp.einsum(
            'bhsk,bhsv->bhkv', k_c * jnp.exp(b_end - b), v_c)
        return S_new, o_c

    S0 = jnp.zeros((B, HG_HEADS, HG_DK, HG_DV), f32)
    _, o = lax.scan(chunk_step, S0, (qc, kc, vc, lfc))
    o = jnp.moveaxis(o, 0, 2).reshape(B, HG_HEADS, n_c * HG_CHUNK, HG_DV)[:, :, pad:]
    o = o.transpose(0, 2, 1, 3)
    o = o * lax.rsqrt(jnp.mean(o * o, axis=-1, keepdims=True) + EPS)
    o = o.reshape(B, L, HG_WIDTH) * hg_norm.astype(f32) * jax.nn.silu(g_raw.astype(f32))
    return o.astype(q_raw.dtype)


def hier_moe(xf, w_group, b_group, w_route, b_route, w1, w3, w2):
    N, D = xf.shape
    g_logits = (xf @ w_group + b_group).astype(jnp.float32)
    p_group = jax.nn.softmax(g_logits, axis=-1)
    g_sel = jnp.argmax(g_logits, axis=-1)
    p_sel = jnp.take_along_axis(p_group, g_sel[:, None], axis=-1)
    e_logits = (xf @ w_route + b_route).astype(jnp.float32).reshape(N, N_GROUPS, EXPERTS_PER_GROUP)
    e_in_group = jnp.take_along_axis(e_logits, g_sel[:, None, None], axis=1)[:, 0]
    top_val, top_idx = lax.top_k(e_in_group, TOP_K)
    gate = (p_sel * jax.nn.softmax(top_val, axis=-1)).astype(xf.dtype)
    expert = g_sel[:, None] * EXPERTS_PER_GROUP + top_idx

    A = N * TOP_K
    flat_e = expert.reshape(A)
    flat_tok = jnp.broadcast_to(jnp.arange(N, dtype=jnp.int32)[:, None], (N, TOP_K)).reshape(A)
    flat_w = gate.reshape(A)
    order = jnp.argsort(flat_e)
    e_sorted = flat_e[order]
    counts = jnp.bincount(flat_e, length=N_EXPERTS)
    start = jnp.cumsum(counts) - counts
    padded = (counts + MOE_BLOCK - 1) // MOE_BLOCK * MOE_BLOCK
    p_end = jnp.cumsum(padded)
    p_start = p_end - padded
    dest = p_start[e_sorted] + (jnp.arange(A) - start[e_sorted])
    n_blocks = -(-A // MOE_BLOCK) + N_EXPERTS
    R = n_blocks * MOE_BLOCK
    buf_tok = jnp.full((R,), N, dtype=jnp.int32).at[dest].set(flat_tok[order])
    buf_w = jnp.zeros((R,), xf.dtype).at[dest].set(flat_w[order])
    blk_expert = jnp.minimum(
        jnp.searchsorted(p_end, jnp.arange(n_blocks) * MOE_BLOCK, side='right'), N_EXPERTS - 1)
    x_pad = jnp.concatenate([xf, jnp.zeros((1, D), xf.dtype)], axis=0)

    def expert_block(args):
        tok, e = args
        xb = x_pad[tok]
        hdn = jax.nn.silu(xb @ w1[e]) * (xb @ w3[e])
        return hdn @ w2[e]

    y = lax.map(expert_block, (buf_tok.reshape(n_blocks, MOE_BLOCK), blk_expert))
    y = y.reshape(R, D) * buf_w[:, None]
    return jnp.zeros((N + 1, D), xf.dtype).at[buf_tok].add(y)[:N]


def setup_inputs(seed: int = 0) -> dict:
    key = jax.random.key(seed)
    ks = jax.random.split(key, 24)
    f32 = jnp.float32

    def nrm(k, shape, scale):
        return jax.random.normal(k, shape, f32) * scale

    def gain(k, shape):
        return 1.0 + 0.02 * jax.random.normal(k, shape, f32)

    return {
        'x': nrm(ks[0], (BATCH, SEQ, D_MODEL), 1.0),
        'meta_tokens': nrm(ks[1], (N_META, D_MODEL), 1.0),
        'attn_norm': gain(ks[2], (DEPTH, D_MODEL)),
        'w_in': nrm(ks[3], (DEPTH, D_MODEL, D_IN), D_MODEL ** -0.5),
        'q_norm': gain(ks[4], (DEPTH, Q_LORA)),
        'w_uq': nrm(ks[5], (DEPTH, Q_LORA, MLA_HEADS * (MLA_NOPE + MLA_ROPE)), Q_LORA ** -0.5),
        'kv_norm': gain(ks[6], (DEPTH, KV_LORA)),
        'w_ukv': nrm(ks[7], (DEPTH, KV_LORA, MLA_HEADS * (MLA_NOPE + MLA_V)), KV_LORA ** -0.5),
        'lb_table': nrm(ks[8], (DEPTH + 1, HG_FWIDTH), 0.5),
        'hg_norm': gain(ks[9], (DEPTH, HG_WIDTH)),
        'w_br_mla': nrm(ks[10], (DEPTH, MLA_WIDTH, D_MODEL), MLA_WIDTH ** -0.5),
        'w_br_hgrn': nrm(ks[11], (DEPTH, HG_WIDTH, D_MODEL), HG_WIDTH ** -0.5),
        'b_gate': nrm(ks[12], (DEPTH, 2, D_MODEL), 0.02),
        'w_out': nrm(ks[13], (DEPTH, D_MODEL, D_MODEL), D_MODEL ** -0.5),
        'ffn_norm': gain(ks[14], (DEPTH, D_MODEL)),
        'w_group': nrm(ks[15], (DEPTH, D_MODEL, N_GROUPS), D_MODEL ** -0.5),
        'b_group': nrm(ks[16], (DEPTH, N_GROUPS), 0.01),
        'w_route': nrm(ks[17], (DEPTH, D_MODEL, N_EXPERTS), D_MODEL ** -0.5),
        'b_route': nrm(ks[18], (DEPTH, N_EXPERTS), 0.01),
        'w1': nrm(ks[19], (DEPTH, N_EXPERTS, D_MODEL, D_EXPERT), D_MODEL ** -0.5),
        'w3': nrm(ks[20], (DEPTH, N_EXPERTS, D_MODEL, D_EXPERT), D_MODEL ** -0.5),
        'w2': nrm(ks[21], (DEPTH, N_EXPERTS, D_EXPERT, D_MODEL), D_EXPERT ** -0.5),
        'final_norm': gain(ks[22], (D_MODEL,)),
    }


def reference(x, meta_tokens, attn_norm, w_in, q_norm, w_uq, kv_norm, w_ukv, lb_table, hg_norm,
              w_br_mla, w_br_hgrn, b_gate, w_out, ffn_norm, w_group, b_group, w_route, b_route,
              w1, w3, w2, final_norm):
    B, S, D = x.shape
    L = S + N_META
    h = jnp.concatenate([jnp.broadcast_to(meta_tokens[None].astype(x.dtype), (B, N_META, D)), x], axis=1)
    cos, sin = rope_tables(L)
    lbs = jnp.cumsum(jax.nn.softmax(lb_table.astype(jnp.float32), axis=0), axis=0)
    split_idx = [int(c) for c in np.cumsum(IN_SIZES)[:-1]]
    for l in range(DEPTH):
        u = rmsnorm(h, attn_norm[l])
        proj = u @ w_in[l]
        c_q, c_kv, k_pe, hq, hf, hi, hg, gm, gh = jnp.split(proj, split_idx, axis=-1)
        o_mla = mla_branch(c_q, c_kv, k_pe, q_norm[l], w_uq[l], kv_norm[l], w_ukv[l], cos, sin)
        o_hg = hgrn2_branch(hq, hf, hi, hg, lbs[l], hg_norm[l])
        gate_m = jax.nn.sigmoid(gm + b_gate[l, 0])
        gate_h = jax.nn.sigmoid(gh + b_gate[l, 1])
        merged = gate_m * (o_mla @ w_br_mla[l]) + gate_h * (o_hg @ w_br_hgrn[l])
        h = h + merged @ w_out[l]
        u = rmsnorm(h, ffn_norm[l])
        h = h + hier_moe(u.reshape(B * L, D), w_group[l], b_group[l], w_route[l], b_route[l],
                         w1[l], w3[l], w2[l]).reshape(B, L, D)
    return rmsnorm(h, final_norm)[:, N_META:]
```

```python
import functools

import jax
import jax.numpy as jnp
from jax import lax
from jax.experimental import pallas as pl
from jax.experimental.pallas import tpu as pltpu

F32 = jnp.float32
BF16 = jnp.bfloat16
I32 = jnp.int32

N_META = 16
EPS = 1e-6
MLA_HEADS = 8
MLA_NOPE = 64
MLA_ROPE = 32
MLA_V = 64
Q_LORA = 384
KV_LORA = 256
ROPE_BASE = 10000.0
HG_HEADS = 4
HG_DK = 128
HG_DV = 128
N_GROUPS = 8
EXPERTS_PER_GROUP = 8
N_EXPERTS = N_GROUPS * EXPERTS_PER_GROUP
D_EXPERT = 256
MOE_BLOCK = 128

LANES = 128
VMEM_LIMIT_BYTES = 56 * 1024 * 1024

SLOT = LANES
ONE_LANE = MLA_V
HG_CHUNK = 64
HG_SUB = 16
NEG_BIG = -1e30

SEG_CQ = (0, 384)
SEG_CKV = (384, 640)
SEG_KPE = (640, 768)
SEG_KPE_SW = (768, 896)
SEG_HQ = (896, 1408)
SEG_HF = (1408, 1920)
SEG_HI = (1920, 2432)
SEG_HG = (2432, 2944)
SEG_GM = (2944, 3968)
SEG_GH = (3968, 4992)


def _params(*sem):
    return pltpu.CompilerParams(dimension_semantics=sem, vmem_limit_bytes=VMEM_LIMIT_BYTES)


def _rms(x, g):
    return x * lax.rsqrt(jnp.mean(x * x, axis=-1, keepdims=True) + EPS) * g


def _dot(a, b):
    return jnp.dot(a, b, preferred_element_type=F32)


def _dot_nt(a, b):
    return lax.dot_general(a, b, (((1,), (1,)), ((), ())), preferred_element_type=F32)


def _dot_tn(a, b):
    return lax.dot_general(a, b, (((0,), (0,)), ((), ())), preferred_element_type=F32)


def _split2(x):
    hi = x.astype(BF16)
    lo = (x - hi.astype(F32)).astype(BF16)
    return hi, lo


def _split3(x):
    hi = x.astype(BF16)
    r = x - hi.astype(F32)
    mid = r.astype(BF16)
    lo = (r - mid.astype(F32)).astype(BF16)
    return hi, mid, lo


def _const_spec(shape):
    nd = len(shape)
    return pl.BlockSpec(shape, lambda *_: (0,) * nd)


def _in_proj_kernel(x_ref, g_ref, w_ref, qn_ref, kvn_ref, wq_ref, wqs_ref, wk_ref, wv_ref,
                    vone_ref, cos_ref, sin_ref, bg_ref,
                    q_out, k_out, v_out, hq_out, hf_out, hi_out, hg_out, gm_out, gh_out):
    u = _rms(x_ref[...], g_ref[...]).astype(BF16)

    def seg(s):
        return _dot(u, w_ref[:, s[0]:s[1]])

    cos = cos_ref[...]
    sin = sin_ref[...]
    cos_t = jnp.tile(cos, (1, MLA_HEADS))
    sin_t = jnp.tile(sin, (1, MLA_HEADS))
    scale = (MLA_NOPE + MLA_ROPE) ** -0.5

    cqn = _rms(seg(SEG_CQ), qn_ref[...]).astype(BF16)
    q = _dot(cqn, wq_ref[...]) * cos_t + _dot(cqn, wqs_ref[...]) * sin_t
    q_out[...] = (q * scale).astype(BF16)

    ckvn = _rms(seg(SEG_CKV), kvn_ref[...]).astype(BF16)
    k_slot = seg(SEG_KPE) * cos + seg(SEG_KPE_SW) * sin
    k_out[...] = (_dot(ckvn, wk_ref[...]) + jnp.tile(k_slot, (1, MLA_HEADS))).astype(BF16)
    v_out[...] = (_dot(ckvn, wv_ref[...]) + vone_ref[...]).astype(BF16)

    hq_out[...] = seg(SEG_HQ).astype(BF16)
    hf_out[...] = seg(SEG_HF)
    hi_out[...] = seg(SEG_HI).astype(BF16)
    hg_out[...] = seg(SEG_HG).astype(BF16)
    gm_out[...] = jax.nn.sigmoid(seg(SEG_GM) + bg_ref[0:1, :]).astype(BF16)
    gh_out[...] = jax.nn.sigmoid(seg(SEG_GH) + bg_ref[1:2, :]).astype(BF16)


def _in_proj(x2, tm, tiles_per_seq, cos, sin, wts):
    n, d = x2.shape
    wide = MLA_HEADS * SLOT
    hw = HG_HEADS * HG_DK
    row = lambda width: pl.BlockSpec((tm, width), lambda i: (i, 0))
    tab = pl.BlockSpec((tm, SLOT), lambda i: (i % tiles_per_seq, 0))
    consts = (wts["attn_norm"], wts["w_in"], wts["q_norm"], wts["kv_norm"], wts["wq"], wts["wq_sw"],
              wts["wk"], wts["wv"], wts["v_one"])
    in_specs = [row(d)] + [_const_spec(c.shape) for c in consts] + [tab, tab, _const_spec(wts["b_gate"].shape)]
    widths = (wide, wide, wide, hw, hw, hw, hw, d, d)
    dtypes = (BF16, BF16, BF16, BF16, F32, BF16, BF16, BF16, BF16)
    return pl.pallas_call(
        _in_proj_kernel,
        grid=(n // tm,),
        in_specs=in_specs,
        out_specs=[row(w) for w in widths],
        out_shape=[jax.ShapeDtypeStruct((n, w), t) for w, t in zip(widths, dtypes)],
        compiler_params=_params("parallel"),
        name="in_proj",
    )(x2, *consts, cos, sin, wts["b_gate"])


def _attn_kernel(q_ref, k_ref, v_ref, km_ref, vm_ref, o_ref):
    i = pl.program_id(1)
    tq = q_ref.shape[0]
    row = lax.broadcasted_iota(I32, (tq, tq), 0)
    col = lax.broadcasted_iota(I32, (tq, tq), 1)

    def step(qh, kt, vt, m, acc, mask):
        s = _dot_nt(qh, kt)
        if mask is not None:
            s = jnp.where(mask, s, -jnp.inf)
        m_new = jnp.maximum(m, jnp.max(s, axis=-1, keepdims=True))
        p = jnp.exp(s - m_new)
        return m_new, jnp.exp(m - m_new) * acc + _dot(p.astype(BF16), vt)

    outs = []
    for h in range(MLA_HEADS):
        sl = slice(h * SLOT, (h + 1) * SLOT)
        qh = q_ref[:, sl]
        m0 = jnp.full((tq, 1), -jnp.inf, F32)
        acc0 = jnp.zeros((tq, SLOT), F32)
        m, acc = step(qh, km_ref[:, sl], vm_ref[:, sl], m0, acc0, col < N_META)

        def body(j, carry, qh=qh, sl=sl):
            r0 = pl.multiple_of(j * tq, tq)
            return step(qh, k_ref[pl.ds(r0, tq), sl], v_ref[pl.ds(r0, tq), sl], carry[0], carry[1], None)

        m, acc = lax.fori_loop(0, i, body, (m, acc))
        r0 = pl.multiple_of(i * tq, tq)
        m, acc = step(qh, k_ref[pl.ds(r0, tq), sl], v_ref[pl.ds(r0, tq), sl], m, acc, col <= row)
        outs.append(acc[:, :MLA_V] / acc[:, ONE_LANE:ONE_LANE + 1])
    o_ref[...] = jnp.concatenate(outs, axis=-1).astype(BF16)


def _attention(q, k, v, k_meta, v_meta, batch, seq, tq):
    wide = MLA_HEADS * SLOT
    nq = seq // tq
    return pl.pallas_call(
        _attn_kernel,
        grid=(batch, nq),
        in_specs=[
            pl.BlockSpec((tq, wide), lambda b, i: (b * nq + i, 0)),
            pl.BlockSpec((seq, wide), lambda b, i: (b, 0)),
            pl.BlockSpec((seq, wide), lambda b, i: (b, 0)),
            _const_spec(k_meta.shape),
            _const_spec(v_meta.shape),
        ],
        out_specs=pl.BlockSpec((tq, MLA_HEADS * MLA_V), lambda b, i: (b * nq + i, 0)),
        out_shape=jax.ShapeDtypeStruct((batch * seq, MLA_HEADS * MLA_V), BF16),
        compiler_params=_params("parallel", "arbitrary"),
        name="mla_attention",
    )(q, k, v, k_meta, v_meta)


def _lower_bound(lbt_ref):
    t0 = lbt_ref[0:1, :]
    t1 = lbt_ref[1:2, :]
    mx = jnp.maximum(t0, t1)
    e0 = jnp.exp(t0 - mx)
    return e0 / (e0 + jnp.exp(t1 - mx))


def _forget(hf, lb):
    f = lb + (1.0 - lb) * jax.nn.sigmoid(hf)
    return jnp.log(f), 1.0 - f


def _cumsum_rows(tril, lf):
    parts = _split3(lf)
    return _dot(tril, parts[0]) + _dot(tril, parts[1]) + _dot(tril, parts[2])


def _tril(n):
    return (lax.broadcasted_iota(I32, (n, n), 0) >= lax.broadcasted_iota(I32, (n, n), 1)).astype(BF16)


def _hgrn_meta_kernel(hf_ref, hi_ref, lbt_ref, s_out):
    lb_all = _lower_bound(lbt_ref)
    tril = _tril(hf_ref.shape[0])
    for h in range(HG_HEADS):
        sl = slice(h * HG_DK, (h + 1) * HG_DK)
        lf, kk = _forget(hf_ref[:, sl], lb_all[:, sl])
        b = _cumsum_rows(tril, lf)
        kdec = kk * jnp.exp(b[-1:, :] - b)
        s_out[h] = _dot_tn(hi_ref[:, sl], kdec.astype(BF16))


def _hgrn_meta_state(hf, hi, lb_table):
    return pl.pallas_call(
        _hgrn_meta_kernel,
        out_shape=jax.ShapeDtypeStruct((HG_HEADS, HG_DV, HG_DK), F32),
        compiler_params=pltpu.CompilerParams(vmem_limit_bytes=VMEM_LIMIT_BYTES),
        name="hgrn_meta_state",
    )(hf, hi, lb_table)


def _hgrn_kernel(hq_ref, hf_ref, hi_ref, hg_ref, lbt_ref, hgn_ref, s0_ref, o_ref,
                 st_ref, pb_ref, pk_ref, pv_ref):
    c_rows = HG_CHUNK
    n_chunks = hq_ref.shape[0] // c_rows
    st_ref[...] = s0_ref[...]
    pad = jnp.zeros((HG_HEADS, HG_SUB, HG_DK), F32)
    pb_ref[:, 0:HG_SUB, :] = pad
    pk_ref[:, 0:HG_SUB, :] = pad
    pv_ref[:, 0:HG_SUB, :] = pad
    lb_all = _lower_bound(lbt_ref)
    hgn = hgn_ref[...]
    tril = _tril(c_rows)
    ones = jnp.ones((HG_DK, LANES), BF16)
    rowi = lax.broadcasted_iota(I32, (c_rows, HG_DK), 0)
    row_in_sub = rowi & (HG_SUB - 1)
    n_sub = c_rows // HG_SUB

    def chunk(c, carry):
        r0 = pl.multiple_of(c * c_rows, c_rows)
        rows = pl.ds(r0, c_rows)
        for h in range(HG_HEADS):
            sl = slice(h * HG_DK, (h + 1) * HG_DK)
            lf, kk = _forget(hf_ref[rows, sl], lb_all[:, sl])
            q = jax.nn.silu(hq_ref[rows, sl].astype(F32))
            v_bf = hi_ref[rows, sl]
            v = v_bf.astype(F32)
            b = _cumsum_rows(tril, lf)
            st = st_ref[h]

            o = _dot_nt((q * jnp.exp(b)).astype(BF16), st.astype(BF16))

            a_off = jnp.zeros((c_rows, c_rows), F32)
            for j in range(n_sub - 1):
                rj = b[HG_SUB * j + HG_SUB - 1:HG_SUB * j + HG_SUB, :]
                q_ok = rowi >= HG_SUB * (j + 1)
                k_ok = (rowi >= HG_SUB * j) & (rowi < HG_SUB * (j + 1))
                qp = q * jnp.exp(jnp.where(q_ok, b - rj, NEG_BIG))
                kp = kk * jnp.exp(jnp.where(k_ok, rj - b, NEG_BIG))
                a_off = a_off + _dot_nt(qp.astype(BF16), kp.astype(BF16))
            o = o + _dot(a_off.astype(BF16), v_bf)

            pb_ref[h, HG_SUB:HG_SUB + c_rows, :] = b
            pk_ref[h, HG_SUB:HG_SUB + c_rows, :] = kk
            pv_ref[h, HG_SUB:HG_SUB + c_rows, :] = v
            for d in range(HG_SUB):
                shifted = slice(HG_SUB - d, HG_SUB - d + c_rows)
                bs = pb_ref[h, shifted, :]
                ks = pk_ref[h, shifted, :]
                vs = pv_ref[h, shifted, :]
                x = jnp.where(row_in_sub >= d, q * ks * jnp.exp(b - bs), 0.0)
                x_hi, x_lo = _split2(x)
                o = o + (_dot(x_hi, ones) + _dot(x_lo, ones)) * vs

            o = o * lax.rsqrt(jnp.mean(o * o, axis=-1, keepdims=True) + EPS)
            o = o * hgn[:, sl] * jax.nn.silu(hg_ref[rows, sl].astype(F32))
            o_ref[rows, sl] = o.astype(BF16)

            b_end = b[c_rows - 1:c_rows, :]
            kdec = kk * jnp.exp(b_end - b)
            st_ref[h] = st * jnp.exp(b_end) + _dot_tn(v_bf, kdec.astype(BF16))
        return carry

    lax.fori_loop(0, n_chunks, chunk, 0)


def _hgrn(hq, hf, hi, hg, lb_table, hg_norm, s0, batch, seq):
    hw = HG_HEADS * HG_DK
    seq_spec = pl.BlockSpec((seq, hw), lambda b: (b, 0))
    pad_rows = HG_SUB + HG_CHUNK
    return pl.pallas_call(
        _hgrn_kernel,
        grid=(batch,),
        in_specs=[seq_spec, seq_spec, seq_spec, seq_spec,
                  _const_spec(lb_table.shape), _const_spec(hg_norm.shape), _const_spec(s0.shape)],
        out_specs=seq_spec,
        out_shape=jax.ShapeDtypeStruct((batch * seq, hw), BF16),
        scratch_shapes=[pltpu.VMEM((HG_HEADS, HG_DV, HG_DK), F32)]
        + [pltpu.VMEM((HG_HEADS, pad_rows, HG_DK), F32)] * 3,
        compiler_params=_params("parallel"),
        name="hgrn2",
    )(hq, hf, hi, hg, lb_table, hg_norm, s0)


ROUTE_E1, ROUTE_E2, ROUTE_G1, ROUTE_G2, ROUTE_R1, ROUTE_R2 = range(6)


def _merge_kernel(om_ref, oh_ref, gm_ref, gh_ref, x_ref, wbm_ref, wbh_ref, wo_ref, fn_ref,
                  wr_hi_ref, wr_lo_ref, br_ref, h_out, u_out, route_out, cnt_out, carry_ref):
    i = pl.program_id(0)
    tm = x_ref.shape[0]

    @pl.when(i == 0)
    def _():
        carry_ref[...] = jnp.zeros_like(carry_ref)

    a = _dot(om_ref[...], wbm_ref[...])
    g = _dot(oh_ref[...], wbh_ref[...])
    merged = gm_ref[...].astype(F32) * a + gh_ref[...].astype(F32) * g
    h1 = x_ref[...] + _dot(merged.astype(BF16), wo_ref[...])
    h_out[...] = h1
    u = _rms(h1, fn_ref[...])
    u_out[...] = u

    u_hi, u_lo = _split2(u)
    logits = (_dot(u_hi, wr_hi_ref[...]) + _dot(u_hi, wr_lo_ref[...]) + _dot(u_lo, wr_hi_ref[...])
              + br_ref[...])
    lane = lax.broadcasted_iota(I32, (tm, LANES), 1)
    lane_f = lane.astype(F32)
    big = float(2 * LANES)

    def first_max(vals):
        mx = jnp.max(vals, axis=-1, keepdims=True)
        idx = jnp.min(jnp.where(vals == mx, lane_f, big), axis=-1, keepdims=True)
        return mx, idx

    gl = jnp.where(lane < N_GROUPS, logits, -jnp.inf)
    g_max, g_sel = first_max(gl)
    p_sel = 1.0 / jnp.sum(jnp.exp(gl - g_max), axis=-1, keepdims=True)
    lo = N_GROUPS + g_sel * EXPERTS_PER_GROUP
    el = jnp.where((lane_f >= lo) & (lane_f < lo + EXPERTS_PER_GROUP), logits, -jnp.inf)
    v1, i1 = first_max(el)
    el2 = jnp.where(lane_f == i1, -jnp.inf, el)
    v2, i2 = first_max(el2)
    t = jnp.exp(v2 - v1)
    g1 = p_sel * (1.0 / (1.0 + t))
    g2 = p_sel * (t / (1.0 + t))

    hit1 = lane_f == i1
    hit2 = lane_f == i2
    onehot = jnp.where(hit1 | hit2, 1.0, 0.0)
    strict = (lax.broadcasted_iota(I32, (tm, tm), 0) > lax.broadcasted_iota(I32, (tm, tm), 1)).astype(BF16)
    before = _dot(strict, onehot.astype(BF16)) + carry_ref[0:1, :]
    r1 = jnp.sum(jnp.where(hit1, before, 0.0), axis=-1, keepdims=True)
    r2 = jnp.sum(jnp.where(hit2, before, 0.0), axis=-1, keepdims=True)
    total = carry_ref[0:1, :] + jnp.sum(onehot, axis=0, keepdims=True)
    carry_ref[...] = jnp.broadcast_to(total, carry_ref.shape)
    cnt_out[...] = jnp.broadcast_to(total, cnt_out.shape)

    route = jnp.zeros((tm, LANES), F32)
    for pos, val in ((ROUTE_E1, i1 - N_GROUPS), (ROUTE_E2, i2 - N_GROUPS), (ROUTE_G1, g1),
                     (ROUTE_G2, g2), (ROUTE_R1, r1), (ROUTE_R2, r2)):
        route = jnp.where(lane == pos, val, route)
    route_out[...] = route


def _merge(o_mla, o_hg, gm, gh, x2, wts, tm):
    n, d = x2.shape
    row = lambda width: pl.BlockSpec((tm, width), lambda i: (i, 0))
    consts = (wts["w_br_mla"], wts["w_br_hgrn"], wts["w_out"], wts["ffn_norm"],
              wts["w_router_hi"], wts["w_router_lo"], wts["b_router"])
    return pl.pallas_call(
        _merge_kernel,
        grid=(n // tm,),
        in_specs=[row(o_mla.shape[1]), row(o_hg.shape[1]), row(d), row(d), row(d)]
        + [_const_spec(c.shape) for c in consts],
        out_specs=[row(d), row(d), row(LANES), _const_spec((8, LANES))],
        out_shape=[jax.ShapeDtypeStruct((n, d), F32), jax.ShapeDtypeStruct((n, d), F32),
                   jax.ShapeDtypeStruct((n, LANES), F32), jax.ShapeDtypeStruct((8, LANES), F32)],
        scratch_shapes=[pltpu.VMEM((8, LANES), F32)],
        compiler_params=_params("arbitrary"),
        name="merge_route",
    )(o_mla, o_hg, gm, gh, x2, *consts)


def _row_copy(src_ref, src_row, dst_ref, dst_row, sem):
    return pltpu.make_async_copy(src_ref.at[pl.ds(src_row, 1)], dst_ref.at[pl.ds(dst_row, 1)], sem)


def _dispatch_kernel(d1_ref, d2_ref, u_ref, xs_in, xs_out, sem):
    del xs_in
    tm = u_ref.shape[0]
    base = pl.program_id(0) * tm

    def start(r, carry):
        _row_copy(u_ref, r, xs_out, d1_ref[base + r], sem).start()
        _row_copy(u_ref, r, xs_out, d2_ref[base + r], sem).start()
        return carry

    def wait(r, carry):
        _row_copy(u_ref, 0, xs_out, 0, sem).wait()
        _row_copy(u_ref, 0, xs_out, 0, sem).wait()
        return carry

    lax.fori_loop(0, tm, start, 0)
    lax.fori_loop(0, tm, wait, 0)


def _dispatch(dest1, dest2, u, xs_zero, tm):
    n, d = u.shape
    return pl.pallas_call(
        _dispatch_kernel,
        grid_spec=pltpu.PrefetchScalarGridSpec(
            num_scalar_prefetch=2,
            grid=(n // tm,),
            in_specs=[pl.BlockSpec((tm, d), lambda i, *_: (i, 0)), pl.BlockSpec(memory_space=pl.ANY)],
            out_specs=pl.BlockSpec(memory_space=pl.ANY),
            scratch_shapes=[pltpu.SemaphoreType.DMA],
        ),
        out_shape=jax.ShapeDtypeStruct(xs_zero.shape, xs_zero.dtype),
        input_output_aliases={3: 0},
        compiler_params=_params("arbitrary"),
        name="moe_dispatch",
    )(dest1, dest2, u, xs_zero)


def _expert_kernel(be_ref, nu_ref, xs_ref, w1_ref, w3_ref, w2_ref, y_ref, w13_s, w2_s):
    i = pl.program_id(0)
    used = i < nu_ref[0]

    @pl.when(used)
    def _():
        prev = be_ref[jnp.maximum(i - 1, 0)]

        @pl.when((i == 0) | (be_ref[i] != prev))
        def _():
            w13_s[:, 0:D_EXPERT] = w1_ref[0].astype(BF16)
            w13_s[:, D_EXPERT:2 * D_EXPERT] = w3_ref[0].astype(BF16)
            w2_s[...] = w2_ref[0].astype(BF16)

        hcat = _dot(xs_ref[...].astype(BF16), w13_s[...])
        hdn = jax.nn.silu(hcat[:, 0:D_EXPERT]) * hcat[:, D_EXPERT:2 * D_EXPERT]
        y_ref[...] = _dot(hdn.astype(BF16), w2_s[...])

    @pl.when(jnp.logical_not(used))
    def _():
        y_ref[...] = jnp.zeros_like(y_ref)


def _experts(blk_expert, n_used, xs, w1, w3, w2):
    r, d = xs.shape
    n_blocks = r // MOE_BLOCK
    return pl.pallas_call(
        _expert_kernel,
        grid_spec=pltpu.PrefetchScalarGridSpec(
            num_scalar_prefetch=2,
            grid=(n_blocks,),
            in_specs=[
                pl.BlockSpec((MOE_BLOCK, d), lambda i, be, nu: (i, 0)),
                pl.BlockSpec((1, d, D_EXPERT), lambda i, be, nu: (be[i], 0, 0)),
                pl.BlockSpec((1, d, D_EXPERT), lambda i, be, nu: (be[i], 0, 0)),
                pl.BlockSpec((1, D_EXPERT, d), lambda i, be, nu: (be[i], 0, 0)),
            ],
            out_specs=pl.BlockSpec((MOE_BLOCK, d), lambda i, be, nu: (i, 0)),
            scratch_shapes=[pltpu.VMEM((d, 2 * D_EXPERT), BF16), pltpu.VMEM((D_EXPERT, d), BF16)],
        ),
        out_shape=jax.ShapeDtypeStruct((r, d), F32),
        compiler_params=_params("arbitrary"),
        name="moe_experts",
    )(blk_expert, n_used, xs, w1, w3, w2)


def _combine_kernel(d1_ref, d2_ref, y_ref, h_ref, route_ref, fn_ref, o_ref, buf_ref, sem):
    tm = h_ref.shape[0]
    base = pl.program_id(0) * tm

    def start(r, carry):
        _row_copy(y_ref, d1_ref[base + r], buf_ref.at[0], r, sem).start()
        _row_copy(y_ref, d2_ref[base + r], buf_ref.at[1], r, sem).start()
        return carry

    def wait(r, carry):
        _row_copy(y_ref, 0, buf_ref.at[0], 0, sem).wait()
        _row_copy(y_ref, 0, buf_ref.at[1], 0, sem).wait()
        return carry

    lax.fori_loop(0, tm, start, 0)
    lax.fori_loop(0, tm, wait, 0)
    route = route_ref[...]
    g1 = route[:, ROUTE_G1:ROUTE_G1 + 1]
    g2 = route[:, ROUTE_G2:ROUTE_G2 + 1]
    h2 = h_ref[...] + (g1 * buf_ref[0] + g2 * buf_ref[1])
    o_ref[...] = _rms(h2, fn_ref[...])


def _combine(dest1, dest2, y, h1, route, final_norm, tm):
    n, d = h1.shape
    return pl.pallas_call(
        _combine_kernel,
        grid_spec=pltpu.PrefetchScalarGridSpec(
            num_scalar_prefetch=2,
            grid=(n // tm,),
            in_specs=[
                pl.BlockSpec(memory_space=pl.ANY),
                pl.BlockSpec((tm, d), lambda i, *_: (i, 0)),
                pl.BlockSpec((tm, LANES), lambda i, *_: (i, 0)),
                pl.BlockSpec((1, d), lambda i, *_: (0, 0)),
            ],
            out_specs=pl.BlockSpec((tm, d), lambda i, *_: (i, 0)),
            scratch_shapes=[pltpu.VMEM((2, tm, d), F32), pltpu.SemaphoreType.DMA],
        ),
        out_shape=jax.ShapeDtypeStruct((n, d), F32),
        compiler_params=_params("arbitrary"),
        name="moe_combine",
    )(dest1, dest2, y, h1, route, final_norm)


def _prepare_weights(attn_norm, w_in, q_norm, w_uq, kv_norm, w_ukv, b_gate, hg_norm, w_br_mla, w_br_hgrn,
                     w_out, ffn_norm, w_group, b_group, w_route, b_route, final_norm):
    d = w_in.shape[0]
    half = MLA_ROPE // 2
    pe0, pe1 = Q_LORA + KV_LORA, Q_LORA + KV_LORA + MLA_ROPE
    w_pe = w_in[:, pe0:pe1]
    zeros = lambda r, c: jnp.zeros((r, c), F32)
    kpe_slot = jnp.concatenate([zeros(d, MLA_NOPE), w_pe, zeros(d, SLOT - MLA_NOPE - MLA_ROPE)], axis=1)
    kpe_swap = jnp.concatenate([zeros(d, MLA_NOPE), -w_pe[:, half:], w_pe[:, :half],
                                zeros(d, SLOT - MLA_NOPE - MLA_ROPE)], axis=1)
    w_wide = jnp.concatenate([w_in[:, :pe0], kpe_slot, kpe_swap, w_in[:, pe1:]], axis=1).astype(BF16)

    wq3 = w_uq.reshape(Q_LORA, MLA_HEADS, MLA_NOPE + MLA_ROPE)
    q_nope, q_pe = wq3[..., :MLA_NOPE], wq3[..., MLA_NOPE:]
    zq = jnp.zeros((Q_LORA, MLA_HEADS, SLOT - MLA_NOPE - MLA_ROPE), F32)
    wq = jnp.concatenate([q_nope, q_pe, zq], axis=-1).reshape(Q_LORA, MLA_HEADS * SLOT).astype(BF16)
    wq_sw = jnp.concatenate([jnp.zeros_like(q_nope), -q_pe[..., half:], q_pe[..., :half], zq],
                            axis=-1).reshape(Q_LORA, MLA_HEADS * SLOT).astype(BF16)

    wkv3 = w_ukv.reshape(KV_LORA, MLA_HEADS, MLA_NOPE + MLA_V)
    zk = jnp.zeros((KV_LORA, MLA_HEADS, SLOT - MLA_NOPE), F32)
    wk = jnp.concatenate([wkv3[..., :MLA_NOPE], zk], axis=-1).reshape(KV_LORA, MLA_HEADS * SLOT).astype(BF16)
    zv = jnp.zeros((KV_LORA, MLA_HEADS, SLOT - MLA_V), F32)
    wv = jnp.concatenate([wkv3[..., MLA_NOPE:], zv], axis=-1).reshape(KV_LORA, MLA_HEADS * SLOT).astype(BF16)
    v_one = jnp.tile((jnp.arange(SLOT) == ONE_LANE).astype(F32), MLA_HEADS)[None, :]

    w_router = jnp.concatenate([w_group, w_route, zeros(d, LANES - N_GROUPS - N_EXPERTS)], axis=1)
    wr_hi = w_router.astype(BF16)
    wr_lo = (w_router - wr_hi.astype(F32)).astype(BF16)
    b_router = jnp.concatenate([b_group, b_route, jnp.zeros((LANES - N_GROUPS - N_EXPERTS,), F32)])[None, :]
    return {
        "attn_norm": attn_norm[None, :], "w_in": w_wide, "q_norm": q_norm[None, :], "kv_norm": kv_norm[None, :],
        "wq": wq, "wq_sw": wq_sw, "wk": wk, "wv": wv, "v_one": v_one, "b_gate": b_gate,
        "hg_norm": hg_norm[None, :], "w_br_mla": w_br_mla.astype(BF16), "w_br_hgrn": w_br_hgrn.astype(BF16),
        "w_out": w_out.astype(BF16), "ffn_norm": ffn_norm[None, :], "w_router_hi": wr_hi, "w_router_lo": wr_lo,
        "b_router": b_router, "final_norm": final_norm[None, :],
    }


def _rope_tables(length):
    inv = ROPE_BASE ** (-jnp.arange(0, MLA_ROPE, 2, dtype=F32) / MLA_ROPE)
    ang = jnp.arange(length, dtype=F32)[:, None] * inv[None, :]
    cos, sin = jnp.cos(ang), jnp.sin(ang)
    ones = jnp.ones((length, MLA_NOPE), F32)
    tail = SLOT - MLA_NOPE - MLA_ROPE
    cos_t = jnp.concatenate([ones, cos, cos, jnp.ones((length, tail), F32)], axis=1)
    sin_t = jnp.concatenate([0.0 * ones, sin, sin, jnp.zeros((length, tail), F32)], axis=1)
    return cos_t, sin_t


def kernel(x, meta_tokens, attn_norm, w_in, q_norm, w_uq, kv_norm, w_ukv, lb_table, hg_norm, w_br_mla, w_br_hgrn, b_gate, w_out, ffn_norm, w_group, b_group, w_route, b_route, w1, w3, w2, final_norm):
    batch, seq, d = x.shape
    assert attn_norm.shape[0] == 1, "one layer"
    n = batch * seq
    x2 = x.reshape(n, d)
    wts = _prepare_weights(attn_norm[0], w_in[0], q_norm[0], w_uq[0], kv_norm[0], w_ukv[0], b_gate[0],
                           hg_norm[0], w_br_mla[0], w_br_hgrn[0], w_out[0], ffn_norm[0], w_group[0],
                           b_group[0], w_route[0], b_route[0], final_norm)
    cos_t, sin_t = _rope_tables(N_META + seq)

    meta = _in_proj(meta_tokens.astype(x.dtype), N_META, 1, cos_t[:N_META], sin_t[:N_META], wts)
    tq = 128
    pad_meta = lambda t: jnp.pad(t, ((0, tq - N_META), (0, 0)))
    s0 = _hgrn_meta_state(meta[4], meta[5], lb_table)

    tm = min(256, seq)
    q, k, v, hq, hf, hi, hg, gm, gh = _in_proj(x2, tm, seq // tm, cos_t[N_META:], sin_t[N_META:], wts)
    o_mla = _attention(q, k, v, pad_meta(meta[1]), pad_meta(meta[2]), batch, seq, tq)
    o_hg = _hgrn(hq, hf, hi, hg, lb_table, wts["hg_norm"], s0, batch, seq)
    h1, u, route, counts = _merge(o_mla, o_hg, gm, gh, x2, wts, tm)

    e1 = route[:, ROUTE_E1].astype(I32)
    e2 = route[:, ROUTE_E2].astype(I32)
    cnt = counts[0, N_GROUPS:N_GROUPS + N_EXPERTS].astype(I32)
    padded = (cnt + MOE_BLOCK - 1) // MOE_BLOCK * MOE_BLOCK
    p_end = jnp.cumsum(padded)
    p_start = p_end - padded
    dest1 = p_start[e1] + route[:, ROUTE_R1].astype(I32)
    dest2 = p_start[e2] + route[:, ROUTE_R2].astype(I32)
    n_blocks = -(-2 * n // MOE_BLOCK) + N_EXPERTS
    blk_expert = jnp.minimum(
        jnp.searchsorted(p_end, jnp.arange(n_blocks, dtype=I32) * MOE_BLOCK, side="right"), N_EXPERTS - 1
    ).astype(I32)
    n_used = (p_end[-1:] // MOE_BLOCK).astype(I32)

    xs = _dispatch(dest1, dest2, u, jnp.zeros((n_blocks * MOE_BLOCK, d), F32), tm)
    y = _experts(blk_expert, n_used, xs, w1[0], w3[0], w2[0])
    out = _combine(dest1, dest2, y, h1, route, wts["final_norm"], tm)
    return out.reshape(batch, seq, d)
```

```python
import functools

import jax
import jax.numpy as jnp
from jax import lax
from jax.experimental import pallas as pl
from jax.experimental.pallas import tpu as pltpu

F32 = jnp.float32
BF16 = jnp.bfloat16
I32 = jnp.int32

N_META = 16
EPS = 1e-6
MLA_HEADS = 8
MLA_NOPE = 64
MLA_ROPE = 32
MLA_V = 64
Q_LORA = 384
KV_LORA = 256
ROPE_BASE = 10000.0
HG_HEADS = 4
HG_DK = 128
HG_DV = 128
N_GROUPS = 8
EXPERTS_PER_GROUP = 8
N_EXPERTS = N_GROUPS * EXPERTS_PER_GROUP
D_EXPERT = 256
MOE_BLOCK = 128

LANES = 128
VMEM_LIMIT_BYTES = 56 * 1024 * 1024

SLOT = LANES
ONE_LANE = MLA_V
HG_CHUNK = 64
HG_SUB = 16
NEG_BIG = -1e30
LOG2_E = 1.4426950408889634

SEG_CQ = (0, 384)
SEG_CKV = (384, 640)
SEG_KPE = (640, 768)
SEG_KPE_SW = (768, 896)
SEG_HQ = (896, 1408)
SEG_HF = (1408, 1920)
SEG_HI = (1920, 2432)
SEG_HG = (2432, 2944)
SEG_GM = (2944, 3968)
SEG_GH = (3968, 4992)


def _params(*sem):
    return pltpu.CompilerParams(dimension_semantics=sem, vmem_limit_bytes=VMEM_LIMIT_BYTES)


def _rms(x, g):
    return x * lax.rsqrt(jnp.mean(x * x, axis=-1, keepdims=True) + EPS) * g


def _dot(a, b):
    return jnp.dot(a, b, preferred_element_type=F32)


def _dot_nt(a, b):
    return lax.dot_general(a, b, (((1,), (1,)), ((), ())), preferred_element_type=F32)


def _dot_tn(a, b):
    return lax.dot_general(a, b, (((0,), (0,)), ((), ())), preferred_element_type=F32)


def _split2(x):
    hi = x.astype(BF16)
    lo = (x - hi.astype(F32)).astype(BF16)
    return hi, lo


def _split3(x):
    hi = x.astype(BF16)
    r = x - hi.astype(F32)
    mid = r.astype(BF16)
    lo = (r - mid.astype(F32)).astype(BF16)
    return hi, mid, lo


def _const_spec(shape):
    nd = len(shape)
    return pl.BlockSpec(shape, lambda *_: (0,) * nd)


def _in_proj_kernel(x_ref, g_ref, w_ref, qn_ref, kvn_ref, wq_ref, wqs_ref, wk_ref, wvt_ref,
                    cos_ref, sin_ref, bg_ref,
                    q_out, k_out, vt_out, hq_out, hf_out, hi_out, hg_out, gm_out, gh_out):
    u = _rms(x_ref[...], g_ref[...]).astype(BF16)

    def seg(s):
        return _dot(u, w_ref[:, s[0]:s[1]])

    cos = cos_ref[...]
    sin = sin_ref[...]
    cos_t = jnp.tile(cos, (1, MLA_HEADS))
    sin_t = jnp.tile(sin, (1, MLA_HEADS))
    scale = (MLA_NOPE + MLA_ROPE) ** -0.5 * LOG2_E

    cqn = _rms(seg(SEG_CQ), qn_ref[...]).astype(BF16)
    q = _dot(cqn, wq_ref[...]) * cos_t + _dot(cqn, wqs_ref[...]) * sin_t
    q_out[...] = (q * scale).astype(BF16)

    ckvn = _rms(seg(SEG_CKV), kvn_ref[...]).astype(BF16)
    k_slot = seg(SEG_KPE) * cos + seg(SEG_KPE_SW) * sin
    k_out[...] = (_dot(ckvn, wk_ref[...]) + jnp.tile(k_slot, (1, MLA_HEADS))).astype(BF16)
    vt = _dot_nt(wvt_ref[...], ckvn)
    slot_row = lax.broadcasted_iota(I32, vt.shape, 0) & (SLOT - 1)
    vt_out[0] = jnp.where(slot_row == ONE_LANE, 1.0, vt).astype(BF16)

    hq_out[...] = seg(SEG_HQ).astype(BF16)
    hf_out[...] = seg(SEG_HF)
    hi_out[...] = seg(SEG_HI).astype(BF16)
    hg_out[...] = seg(SEG_HG).astype(BF16)
    gm_out[...] = jax.nn.sigmoid(seg(SEG_GM) + bg_ref[0:1, :]).astype(BF16)
    gh_out[...] = jax.nn.sigmoid(seg(SEG_GH) + bg_ref[1:2, :]).astype(BF16)


def _in_proj(x2, tm, tiles_per_seq, cos, sin, wts):
    n, d = x2.shape
    wide = MLA_HEADS * SLOT
    hw = HG_HEADS * HG_DK
    row = lambda width: pl.BlockSpec((tm, width), lambda i: (i, 0))
    tab = pl.BlockSpec((tm, SLOT), lambda i: (i % tiles_per_seq, 0))
    consts = (wts["attn_norm"], wts["w_in"], wts["q_norm"], wts["kv_norm"], wts["wq"], wts["wq_sw"],
              wts["wk"], wts["wv_t"])
    in_specs = [row(d)] + [_const_spec(c.shape) for c in consts] + [tab, tab, _const_spec(wts["b_gate"].shape)]
    widths = (wide, wide, None, hw, hw, hw, hw, d, d)
    dtypes = (BF16, BF16, BF16, BF16, F32, BF16, BF16, BF16, BF16)
    vt_spec = pl.BlockSpec((1, wide, tm), lambda i: (i, 0, 0))
    return pl.pallas_call(
        _in_proj_kernel,
        grid=(n // tm,),
        in_specs=in_specs,
        out_specs=[vt_spec if w is None else row(w) for w in widths],
        out_shape=[jax.ShapeDtypeStruct((n // tm, wide, tm) if w is None else (n, w), t)
                   for w, t in zip(widths, dtypes)],
        compiler_params=_params("parallel"),
        name="in_proj",
    )(x2, *consts, cos, sin, wts["b_gate"])


def _attn_kernel(q_ref, k_ref, vt_ref, km_ref, vtm_ref, o_ref, m_ref, acc_ref):
    i = pl.program_id(1)
    tq = q_ref.shape[0]

    heads = [slice(h * SLOT, (h + 1) * SLOT) for h in range(MLA_HEADS)]

    def tile(k_of, vt_of, mask, first=False):
        s = [_dot_nt(k_of(sl), q_ref[:, sl]) for sl in heads]
        if mask is not None:
            s = [jnp.where(mask, x, -jnp.inf) for x in s]
        mx = [jnp.max(x, axis=0, keepdims=True) for x in s]
        if first:
            m_new = mx
        else:
            m_old = [m_ref[h] for h in range(MLA_HEADS)]
            m_new = [jnp.maximum(a, b) for a, b in zip(m_old, mx)]
        p = [jnp.exp2(x - m).astype(BF16) for x, m in zip(s, m_new)]
        for h, sl in enumerate(heads):
            m_ref[h] = m_new[h]
            pv = _dot(vt_of(sl), p[h])
            acc_ref[h] = pv if first else jnp.exp2(m_old[h] - m_new[h]) * acc_ref[h] + pv

    tile(lambda sl: km_ref[:, sl], lambda sl: vtm_ref[0, sl, :], None, first=True)

    def x_tile(j, mask):
        rows = pl.ds(pl.multiple_of(j * tq, tq), tq)
        tile(lambda sl: k_ref[rows, sl], lambda sl: vt_ref[j, sl, :], mask)

    def body(j, carry):
        x_tile(j, None)
        return carry

    lax.fori_loop(0, i, body, 0)
    x_tile(i, lax.broadcasted_iota(I32, (tq, tq), 0) <= lax.broadcasted_iota(I32, (tq, tq), 1))

    outs = []
    for h in range(MLA_HEADS):
        acc = acc_ref[h]
        outs.append(acc[:MLA_V, :] / acc[ONE_LANE:ONE_LANE + 1, :])
    o_ref[...] = jnp.concatenate(outs, axis=0).T.astype(BF16)


def _attention(q, k, vt, k_meta, vt_meta, batch, seq, tq):
    wide = MLA_HEADS * SLOT
    nq = seq // tq
    assert vt.shape == (batch * nq, wide, tq)
    return pl.pallas_call(
        _attn_kernel,
        grid=(batch, nq),
        in_specs=[
            pl.BlockSpec((tq, wide), lambda b, i: (b * nq + i, 0)),
            pl.BlockSpec((seq, wide), lambda b, i: (b, 0)),
            pl.BlockSpec((nq, wide, tq), lambda b, i: (b, 0, 0)),
            _const_spec(k_meta.shape),
            _const_spec(vt_meta.shape),
        ],
        out_specs=pl.BlockSpec((tq, MLA_HEADS * MLA_V), lambda b, i: (b * nq + i, 0)),
        out_shape=jax.ShapeDtypeStruct((batch * seq, MLA_HEADS * MLA_V), BF16),
        scratch_shapes=[pltpu.VMEM((MLA_HEADS, 1, tq), F32), pltpu.VMEM((MLA_HEADS, SLOT, tq), F32)],
        compiler_params=_params("parallel", "arbitrary"),
        name="mla_attention",
    )(q, k, vt, k_meta, vt_meta)


def _lower_bound(lbt_ref):
    t0 = lbt_ref[0:1, :]
    t1 = lbt_ref[1:2, :]
    mx = jnp.maximum(t0, t1)
    e0 = jnp.exp(t0 - mx)
    return e0 / (e0 + jnp.exp(t1 - mx))


def _forget(hf, lb):
    f = lb + (1.0 - lb) * jax.nn.sigmoid(hf)
    return jnp.log(f), 1.0 - f


def _cumsum_rows(tril, lf):
    parts = _split3(lf)
    return _dot(tril, parts[0]) + _dot(tril, parts[1]) + _dot(tril, parts[2])


def _tril(n):
    return (lax.broadcasted_iota(I32, (n, n), 0) >= lax.broadcasted_iota(I32, (n, n), 1)).astype(BF16)


def _hgrn_meta_kernel(hf_ref, hi_ref, lbt_ref, s_out):
    lb_all = _lower_bound(lbt_ref)
    tril = _tril(hf_ref.shape[0])
    for h in range(HG_HEADS):
        sl = slice(h * HG_DK, (h + 1) * HG_DK)
        lf, kk = _forget(hf_ref[:, sl], lb_all[:, sl])
        b = _cumsum_rows(tril, lf)
        kdec = kk * jnp.exp(b[-1:, :] - b)
        s_out[h] = _dot_tn(hi_ref[:, sl], kdec.astype(BF16))


def _hgrn_meta_state(hf, hi, lb_table):
    return pl.pallas_call(
        _hgrn_meta_kernel,
        out_shape=jax.ShapeDtypeStruct((HG_HEADS, HG_DV, HG_DK), F32),
        compiler_params=pltpu.CompilerParams(vmem_limit_bytes=VMEM_LIMIT_BYTES),
        name="hgrn_meta_state",
    )(hf, hi, lb_table)


def _hgrn_kernel(hq_ref, hf_ref, hi_ref, hg_ref, lbt_ref, hgn_ref, s0_ref, o_ref,
                 st_ref, pb_ref, pk_ref, pv_ref):
    c_rows = HG_CHUNK
    n_chunks = hq_ref.shape[0] // c_rows
    st_ref[...] = s0_ref[...]
    pad = jnp.zeros((HG_HEADS, HG_SUB, HG_DK), F32)
    pb_ref[:, 0:HG_SUB, :] = pad
    pk_ref[:, 0:HG_SUB, :] = pad
    pv_ref[:, 0:HG_SUB, :] = pad
    lb_all = _lower_bound(lbt_ref)
    hgn = hgn_ref[...]
    tril = _tril(c_rows)
    ones = jnp.ones((HG_DK, LANES), BF16)
    rowi = lax.broadcasted_iota(I32, (c_rows, HG_DK), 0)
    row_in_sub = rowi & (HG_SUB - 1)
    n_sub = c_rows // HG_SUB

    def chunk(c, carry):
        r0 = pl.multiple_of(c * c_rows, c_rows)
        rows = pl.ds(r0, c_rows)
        for h in range(HG_HEADS):
            sl = slice(h * HG_DK, (h + 1) * HG_DK)
            lf, kk = _forget(hf_ref[rows, sl], lb_all[:, sl])
            q = jax.nn.silu(hq_ref[rows, sl].astype(F32))
            v_bf = hi_ref[rows, sl]
            v = v_bf.astype(F32)
            b = _cumsum_rows(tril, lf)
            st = st_ref[h]

            o = _dot_nt((q * jnp.exp(b)).astype(BF16), st.astype(BF16))

            a_off = jnp.zeros((c_rows, c_rows), F32)
            for j in range(n_sub - 1):
                rj = b[HG_SUB * j + HG_SUB - 1:HG_SUB * j + HG_SUB, :]
                q_ok = rowi >= HG_SUB * (j + 1)
                k_ok = (rowi >= HG_SUB * j) & (rowi < HG_SUB * (j + 1))
                qp = q * jnp.exp(jnp.where(q_ok, b - rj, NEG_BIG))
                kp = kk * jnp.exp(jnp.where(k_ok, rj - b, NEG_BIG))
                a_off = a_off + _dot_nt(qp.astype(BF16), kp.astype(BF16))
            o = o + _dot(a_off.astype(BF16), v_bf)

            pb_ref[h, HG_SUB:HG_SUB + c_rows, :] = b
            pk_ref[h, HG_SUB:HG_SUB + c_rows, :] = kk
            pv_ref[h, HG_SUB:HG_SUB + c_rows, :] = v
            for d in range(HG_SUB):
                shifted = slice(HG_SUB - d, HG_SUB - d + c_rows)
                bs = pb_ref[h, shifted, :]
                ks = pk_ref[h, shifted, :]
                vs = pv_ref[h, shifted, :]
                x = jnp.where(row_in_sub >= d, q * ks * jnp.exp(b - bs), 0.0)
                x_hi, x_lo = _split2(x)
                o = o + (_dot(x_hi, ones) + _dot(x_lo, ones)) * vs

            o = o * lax.rsqrt(jnp.mean(o * o, axis=-1, keepdims=True) + EPS)
            o = o * hgn[:, sl] * jax.nn.silu(hg_ref[rows, sl].astype(F32))
            o_ref[rows, sl] = o.astype(BF16)

            b_end = b[c_rows - 1:c_rows, :]
            kdec = kk * jnp.exp(b_end - b)
            st_ref[h] = st * jnp.exp(b_end) + _dot_tn(v_bf, kdec.astype(BF16))
        return carry

    lax.fori_loop(0, n_chunks, chunk, 0)


def _hgrn(hq, hf, hi, hg, lb_table, hg_norm, s0, batch, seq):
    hw = HG_HEADS * HG_DK
    seq_spec = pl.BlockSpec((seq, hw), lambda b: (b, 0))
    pad_rows = HG_SUB + HG_CHUNK
    return pl.pallas_call(
        _hgrn_kernel,
        grid=(batch,),
        in_specs=[seq_spec, seq_spec, seq_spec, seq_spec,
                  _const_spec(lb_table.shape), _const_spec(hg_norm.shape), _const_spec(s0.shape)],
        out_specs=seq_spec,
        out_shape=jax.ShapeDtypeStruct((batch * seq, hw), BF16),
        scratch_shapes=[pltpu.VMEM((HG_HEADS, HG_DV, HG_DK), F32)]
        + [pltpu.VMEM((HG_HEADS, pad_rows, HG_DK), F32)] * 3,
        compiler_params=_params("parallel"),
        name="hgrn2",
    )(hq, hf, hi, hg, lb_table, hg_norm, s0)


ROUTE_C1, ROUTE_C2, ROUTE_G1, ROUTE_G2 = range(4)
RANK_BITS = 16


def _slot(code, pstart_ref):
    return pstart_ref[code >> RANK_BITS] + (code & ((1 << RANK_BITS) - 1))


def _merge_kernel(om_ref, oh_ref, gm_ref, gh_ref, x_ref, wbm_ref, wbh_ref, wo_ref, fn_ref,
                  wr_hi_ref, wr_lo_ref, br_ref, h_out, u_out, route_out, cnt_out, carry_ref):
    i = pl.program_id(0)
    tm = x_ref.shape[0]

    @pl.when(i == 0)
    def _():
        carry_ref[...] = jnp.zeros_like(carry_ref)

    a = _dot(om_ref[...], wbm_ref[...])
    g = _dot(oh_ref[...], wbh_ref[...])
    merged = gm_ref[...].astype(F32) * a + gh_ref[...].astype(F32) * g
    h1 = x_ref[...] + _dot(merged.astype(BF16), wo_ref[...])
    h_out[...] = h1
    u = _rms(h1, fn_ref[...])
    u_out[...] = u

    u_hi, u_lo = _split2(u)
    logits = (_dot(u_hi, wr_hi_ref[...]) + _dot(u_hi, wr_lo_ref[...]) + _dot(u_lo, wr_hi_ref[...])
              + br_ref[...])
    lane = lax.broadcasted_iota(I32, (tm, LANES), 1)
    lane_f = lane.astype(F32)
    big = float(2 * LANES)

    def first_max(vals):
        mx = jnp.max(vals, axis=-1, keepdims=True)
        idx = jnp.min(jnp.where(vals == mx, lane_f, big), axis=-1, keepdims=True)
        return mx, idx

    gl = jnp.where(lane < N_GROUPS, logits, -jnp.inf)
    g_max, g_sel = first_max(gl)
    p_sel = 1.0 / jnp.sum(jnp.exp(gl - g_max), axis=-1, keepdims=True)
    lo = N_GROUPS + g_sel * EXPERTS_PER_GROUP
    el = jnp.where((lane_f >= lo) & (lane_f < lo + EXPERTS_PER_GROUP), logits, -jnp.inf)
    v1, i1 = first_max(el)
    el2 = jnp.where(lane_f == i1, -jnp.inf, el)
    v2, i2 = first_max(el2)
    t = jnp.exp(v2 - v1)
    g1 = p_sel * (1.0 / (1.0 + t))
    g2 = p_sel * (t / (1.0 + t))

    hit1 = lane_f == i1
    hit2 = lane_f == i2
    onehot = jnp.where(hit1 | hit2, 1.0, 0.0)
    strict = (lax.broadcasted_iota(I32, (tm, tm), 0) > lax.broadcasted_iota(I32, (tm, tm), 1)).astype(BF16)
    before = _dot(strict, onehot.astype(BF16)) + carry_ref[0:1, :]
    r1 = jnp.sum(jnp.where(hit1, before, 0.0), axis=-1, keepdims=True)
    r2 = jnp.sum(jnp.where(hit2, before, 0.0), axis=-1, keepdims=True)
    total = carry_ref[0:1, :] + jnp.sum(onehot, axis=0, keepdims=True)
    carry_ref[...] = jnp.broadcast_to(total, carry_ref.shape)
    cnt_out[...] = jnp.broadcast_to(total, cnt_out.shape)

    route = jnp.zeros((tm, LANES), F32)
    code1 = (i1 - N_GROUPS) * float(1 << RANK_BITS) + r1
    code2 = (i2 - N_GROUPS) * float(1 << RANK_BITS) + r2
    for pos, val in ((ROUTE_C1, code1), (ROUTE_C2, code2), (ROUTE_G1, g1), (ROUTE_G2, g2)):
        route = jnp.where(lane == pos, val, route)
    route_out[...] = route


def _merge(o_mla, o_hg, gm, gh, x2, wts, tm):
    n, d = x2.shape
    row = lambda width: pl.BlockSpec((tm, width), lambda i: (i, 0))
    consts = (wts["w_br_mla"], wts["w_br_hgrn"], wts["w_out"], wts["ffn_norm"],
              wts["w_router_hi"], wts["w_router_lo"], wts["b_router"])
    return pl.pallas_call(
        _merge_kernel,
        grid=(n // tm,),
        in_specs=[row(o_mla.shape[1]), row(o_hg.shape[1]), row(d), row(d), row(d)]
        + [_const_spec(c.shape) for c in consts],
        out_specs=[row(d), row(d), row(LANES), _const_spec((8, LANES))],
        out_shape=[jax.ShapeDtypeStruct((n, d), F32), jax.ShapeDtypeStruct((n, d), F32),
                   jax.ShapeDtypeStruct((n, LANES), F32), jax.ShapeDtypeStruct((8, LANES), F32)],
        scratch_shapes=[pltpu.VMEM((8, LANES), F32)],
        compiler_params=_params("arbitrary"),
        name="merge_route",
    )(o_mla, o_hg, gm, gh, x2, *consts)


def _row_copy(src_ref, src_row, dst_ref, dst_row, sem):
    return pltpu.make_async_copy(src_ref.at[pl.ds(src_row, 1)], dst_ref.at[pl.ds(dst_row, 1)], sem)


def _dispatch_kernel(c1_ref, c2_ref, ps_ref, u_ref, xs_in, xs_out, sem):
    del xs_in
    tm = u_ref.shape[0]
    base = pl.program_id(0) * tm

    def start(r, carry):
        _row_copy(u_ref, r, xs_out, _slot(c1_ref[base + r], ps_ref), sem).start()
        _row_copy(u_ref, r, xs_out, _slot(c2_ref[base + r], ps_ref), sem).start()
        return carry

    def wait(r, carry):
        _row_copy(u_ref, 0, xs_out, 0, sem).wait()
        _row_copy(u_ref, 0, xs_out, 0, sem).wait()
        return carry

    lax.fori_loop(0, tm, start, 0)
    lax.fori_loop(0, tm, wait, 0)


def _dispatch(code1, code2, p_start, u, xs_zero, tm):
    n, d = u.shape
    return pl.pallas_call(
        _dispatch_kernel,
        grid_spec=pltpu.PrefetchScalarGridSpec(
            num_scalar_prefetch=3,
            grid=(n // tm,),
            in_specs=[pl.BlockSpec((tm, d), lambda i, *_: (i, 0)), pl.BlockSpec(memory_space=pl.ANY)],
            out_specs=pl.BlockSpec(memory_space=pl.ANY),
            scratch_shapes=[pltpu.SemaphoreType.DMA],
        ),
        out_shape=jax.ShapeDtypeStruct(xs_zero.shape, xs_zero.dtype),
        input_output_aliases={4: 0},
        compiler_params=_params("arbitrary"),
        name="moe_dispatch",
    )(code1, code2, p_start, u, xs_zero)


def _expert_kernel(be_ref, nu_ref, xs_ref, w1_ref, w3_ref, w2_ref, y_ref, w13_s, w2_s):
    i = pl.program_id(0)
    used = i < nu_ref[0]

    @pl.when(used)
    def _():
        prev = be_ref[jnp.maximum(i - 1, 0)]

        @pl.when((i == 0) | (be_ref[i] != prev))
        def _():
            w13_s[:, 0:D_EXPERT] = w1_ref[0].astype(BF16)
            w13_s[:, D_EXPERT:2 * D_EXPERT] = w3_ref[0].astype(BF16)
            w2_s[...] = w2_ref[0].astype(BF16)

        hcat = _dot(xs_ref[...].astype(BF16), w13_s[...])
        hdn = jax.nn.silu(hcat[:, 0:D_EXPERT]) * hcat[:, D_EXPERT:2 * D_EXPERT]
        y_ref[...] = _dot(hdn.astype(BF16), w2_s[...])

    @pl.when(jnp.logical_not(used))
    def _():
        y_ref[...] = jnp.zeros_like(y_ref)


def _experts(blk_expert, n_used, xs, w1, w3, w2):
    r, d = xs.shape
    n_blocks = r // MOE_BLOCK
    return pl.pallas_call(
        _expert_kernel,
        grid_spec=pltpu.PrefetchScalarGridSpec(
            num_scalar_prefetch=2,
            grid=(n_blocks,),
            in_specs=[
                pl.BlockSpec((MOE_BLOCK, d), lambda i, be, nu: (i, 0)),
                pl.BlockSpec((1, d, D_EXPERT), lambda i, be, nu: (be[i], 0, 0)),
                pl.BlockSpec((1, d, D_EXPERT), lambda i, be, nu: (be[i], 0, 0)),
                pl.BlockSpec((1, D_EXPERT, d), lambda i, be, nu: (be[i], 0, 0)),
            ],
            out_specs=pl.BlockSpec((MOE_BLOCK, d), lambda i, be, nu: (i, 0)),
            scratch_shapes=[pltpu.VMEM((d, 2 * D_EXPERT), BF16), pltpu.VMEM((D_EXPERT, d), BF16)],
        ),
        out_shape=jax.ShapeDtypeStruct((r, d), F32),
        compiler_params=_params("arbitrary"),
        name="moe_experts",
    )(blk_expert, n_used, xs, w1, w3, w2)


def _combine_kernel(c1_ref, c2_ref, ps_ref, y_ref, h_ref, route_ref, fn_ref, o_ref, buf_ref, sem):
    tm = h_ref.shape[0]
    base = pl.program_id(0) * tm

    def start(r, carry):
        _row_copy(y_ref, _slot(c1_ref[base + r], ps_ref), buf_ref.at[0], r, sem).start()
        _row_copy(y_ref, _slot(c2_ref[base + r], ps_ref), buf_ref.at[1], r, sem).start()
        return carry

    def wait(r, carry):
        _row_copy(y_ref, 0, buf_ref.at[0], 0, sem).wait()
        _row_copy(y_ref, 0, buf_ref.at[1], 0, sem).wait()
        return carry

    lax.fori_loop(0, tm, start, 0)
    lax.fori_loop(0, tm, wait, 0)
    route = route_ref[...]
    g1 = route[:, ROUTE_G1:ROUTE_G1 + 1]
    g2 = route[:, ROUTE_G2:ROUTE_G2 + 1]
    h2 = h_ref[...] + (g1 * buf_ref[0] + g2 * buf_ref[1])
    o_ref[...] = _rms(h2, fn_ref[...])


def _combine(code1, code2, p_start, y, h1, route, final_norm, tm):
    n, d = h1.shape
    return pl.pallas_call(
        _combine_kernel,
        grid_spec=pltpu.PrefetchScalarGridSpec(
            num_scalar_prefetch=3,
            grid=(n // tm,),
            in_specs=[
                pl.BlockSpec(memory_space=pl.ANY),
                pl.BlockSpec((tm, d), lambda i, *_: (i, 0)),
                pl.BlockSpec((tm, LANES), lambda i, *_: (i, 0)),
                pl.BlockSpec((1, d), lambda i, *_: (0, 0)),
            ],
            out_specs=pl.BlockSpec((tm, d), lambda i, *_: (i, 0)),
            scratch_shapes=[pltpu.VMEM((2, tm, d), F32), pltpu.SemaphoreType.DMA],
        ),
        out_shape=jax.ShapeDtypeStruct((n, d), F32),
        compiler_params=_params("arbitrary"),
        name="moe_combine",
    )(code1, code2, p_start, y, h1, route, final_norm)


def _prepare_weights(attn_norm, w_in, q_norm, w_uq, kv_norm, w_ukv, b_gate, hg_norm, w_br_mla, w_br_hgrn,
                     w_out, ffn_norm, w_group, b_group, w_route, b_route, final_norm):
    d = w_in.shape[0]
    half = MLA_ROPE // 2
    pe0, pe1 = Q_LORA + KV_LORA, Q_LORA + KV_LORA + MLA_ROPE
    w_pe = w_in[:, pe0:pe1]
    zeros = lambda r, c: jnp.zeros((r, c), F32)
    kpe_slot = jnp.concatenate([zeros(d, MLA_NOPE), w_pe, zeros(d, SLOT - MLA_NOPE - MLA_ROPE)], axis=1)
    kpe_swap = jnp.concatenate([zeros(d, MLA_NOPE), -w_pe[:, half:], w_pe[:, :half],
                                zeros(d, SLOT - MLA_NOPE - MLA_ROPE)], axis=1)
    w_wide = jnp.concatenate([w_in[:, :pe0], kpe_slot, kpe_swap, w_in[:, pe1:]], axis=1).astype(BF16)

    wq3 = w_uq.reshape(Q_LORA, MLA_HEADS, MLA_NOPE + MLA_ROPE)
    q_nope, q_pe = wq3[..., :MLA_NOPE], wq3[..., MLA_NOPE:]
    zq = jnp.zeros((Q_LORA, MLA_HEADS, SLOT - MLA_NOPE - MLA_ROPE), F32)
    wq = jnp.concatenate([q_nope, q_pe, zq], axis=-1).reshape(Q_LORA, MLA_HEADS * SLOT).astype(BF16)
    wq_sw = jnp.concatenate([jnp.zeros_like(q_nope), -q_pe[..., half:], q_pe[..., :half], zq],
                            axis=-1).reshape(Q_LORA, MLA_HEADS * SLOT).astype(BF16)

    wkv3 = w_ukv.reshape(KV_LORA, MLA_HEADS, MLA_NOPE + MLA_V)
    zk = jnp.zeros((KV_LORA, MLA_HEADS, SLOT - MLA_NOPE), F32)
    wk = jnp.concatenate([wkv3[..., :MLA_NOPE], zk], axis=-1).reshape(KV_LORA, MLA_HEADS * SLOT).astype(BF16)
    zv = jnp.zeros((KV_LORA, MLA_HEADS, SLOT - MLA_V), F32)
    wv_t = jnp.concatenate([wkv3[..., MLA_NOPE:], zv], axis=-1).reshape(KV_LORA, MLA_HEADS * SLOT).T.astype(BF16)

    w_router = jnp.concatenate([w_group, w_route, zeros(d, LANES - N_GROUPS - N_EXPERTS)], axis=1)
    wr_hi = w_router.astype(BF16)
    wr_lo = (w_router - wr_hi.astype(F32)).astype(BF16)
    b_router = jnp.concatenate([b_group, b_route, jnp.zeros((LANES - N_GROUPS - N_EXPERTS,), F32)])[None, :]
    return {
        "attn_norm": attn_norm[None, :], "w_in": w_wide, "q_norm": q_norm[None, :], "kv_norm": kv_norm[None, :],
        "wq": wq, "wq_sw": wq_sw, "wk": wk, "wv_t": wv_t, "b_gate": b_gate,
        "hg_norm": hg_norm[None, :], "w_br_mla": w_br_mla.astype(BF16), "w_br_hgrn": w_br_hgrn.astype(BF16),
        "w_out": w_out.astype(BF16), "ffn_norm": ffn_norm[None, :], "w_router_hi": wr_hi, "w_router_lo": wr_lo,
        "b_router": b_router, "final_norm": final_norm[None, :],
    }


def _rope_tables(length):
    inv = ROPE_BASE ** (-jnp.arange(0, MLA_ROPE, 2, dtype=F32) / MLA_ROPE)
    ang = jnp.arange(length, dtype=F32)[:, None] * inv[None, :]
    cos, sin = jnp.cos(ang), jnp.sin(ang)
    ones = jnp.ones((length, MLA_NOPE), F32)
    tail = SLOT - MLA_NOPE - MLA_ROPE
    cos_t = jnp.concatenate([ones, cos, cos, jnp.ones((length, tail), F32)], axis=1)
    sin_t = jnp.concatenate([0.0 * ones, sin, sin, jnp.zeros((length, tail), F32)], axis=1)
    return cos_t, sin_t


def kernel(x, meta_tokens, attn_norm, w_in, q_norm, w_uq, kv_norm, w_ukv, lb_table, hg_norm, w_br_mla, w_br_hgrn, b_gate, w_out, ffn_norm, w_group, b_group, w_route, b_route, w1, w3, w2, final_norm):
    batch, seq, d = x.shape
    assert attn_norm.shape[0] == 1, "one layer"
    n = batch * seq
    x2 = x.reshape(n, d)
    wts = _prepare_weights(attn_norm[0], w_in[0], q_norm[0], w_uq[0], kv_norm[0], w_ukv[0], b_gate[0],
                           hg_norm[0], w_br_mla[0], w_br_hgrn[0], w_out[0], ffn_norm[0], w_group[0],
                           b_group[0], w_route[0], b_route[0], final_norm)
    cos_t, sin_t = _rope_tables(N_META + seq)

    meta = _in_proj(meta_tokens.astype(x.dtype), N_META, 1, cos_t[:N_META], sin_t[:N_META], wts)
    s0 = _hgrn_meta_state(meta[4], meta[5], lb_table)

    tm = min(256, seq)
    q, k, vt, hq, hf, hi, hg, gm, gh = _in_proj(x2, tm, seq // tm, cos_t[N_META:], sin_t[N_META:], wts)
    o_mla = _attention(q, k, vt, meta[1], meta[2], batch, seq, tm)
    o_hg = _hgrn(hq, hf, hi, hg, lb_table, wts["hg_norm"], s0, batch, seq)
    h1, u, route, counts = _merge(o_mla, o_hg, gm, gh, x2, wts, tm)

    code1 = route[:, ROUTE_C1].astype(I32)
    code2 = route[:, ROUTE_C2].astype(I32)
    cnt = counts[0, N_GROUPS:N_GROUPS + N_EXPERTS].astype(I32)
    padded = (cnt + MOE_BLOCK - 1) // MOE_BLOCK * MOE_BLOCK
    p_end = jnp.cumsum(padded)
    p_start = p_end - padded
    n_blocks = -(-2 * n // MOE_BLOCK) + N_EXPERTS
    blk_row = jnp.arange(n_blocks, dtype=I32) * MOE_BLOCK
    blk_expert = jnp.minimum(jnp.sum((p_end[None, :] <= blk_row[:, None]).astype(I32), axis=1), N_EXPERTS - 1)
    n_used = p_end[-1:] // MOE_BLOCK

    xs = _dispatch(code1, code2, p_start, u, jnp.zeros((n_blocks * MOE_BLOCK, d), F32), tm)
    y = _experts(blk_expert, n_used, xs, w1[0], w3[0], w2[0])
    out = _combine(code1, code2, p_start, y, h1, route, wts["final_norm"], tm)
    return out.reshape(batch, seq, d)
```

```python
import functools

import jax
import jax.numpy as jnp
from jax import lax
from jax.experimental import pallas as pl
from jax.experimental.pallas import tpu as pltpu

F32 = jnp.float32
BF16 = jnp.bfloat16
I32 = jnp.int32

N_META = 16
EPS = 1e-6
MLA_HEADS = 8
MLA_NOPE = 64
MLA_ROPE = 32
MLA_V = 64
Q_LORA = 384
KV_LORA = 256
ROPE_BASE = 10000.0
HG_HEADS = 4
HG_DK = 128
HG_DV = 128
N_GROUPS = 8
EXPERTS_PER_GROUP = 8
N_EXPERTS = N_GROUPS * EXPERTS_PER_GROUP
D_EXPERT = 256
MOE_BLOCK = 256

LANES = 128
VMEM_LIMIT_BYTES = 56 * 1024 * 1024

SLOT = LANES
ONE_LANE = MLA_V
ROW_TILE = 512
ATTN_TILE = 256
MOVE_TILE = 256
HG_CHUNK = 64
HG_SUB = 16
LOG2_E = 1.4426950408889634

SEG_CQ = (0, 384)
SEG_CKV = (384, 640)
SEG_KPE = (640, 768)
SEG_KPE_SW = (768, 896)
SEG_HQ = (896, 1408)
SEG_HF = (1408, 1920)
SEG_HI = (1920, 2432)
SEG_HG = (2432, 2944)
SEG_GM = (2944, 3968)
SEG_GH = (3968, 4992)


def _params(*sem):
    return pltpu.CompilerParams(dimension_semantics=sem, vmem_limit_bytes=VMEM_LIMIT_BYTES)


def _rms(x, g):
    return x * lax.rsqrt(jnp.mean(x * x, axis=-1, keepdims=True) + EPS) * g


def _dot(a, b):
    return jnp.dot(a, b, preferred_element_type=F32)


def _dot_nt(a, b):
    return lax.dot_general(a, b, (((1,), (1,)), ((), ())), preferred_element_type=F32)


def _dot_tn(a, b):
    return lax.dot_general(a, b, (((0,), (0,)), ((), ())), preferred_element_type=F32)


def _split2(x):
    hi = x.astype(BF16)
    lo = (x - hi.astype(F32)).astype(BF16)
    return hi, lo


def _split3(x):
    hi = x.astype(BF16)
    r = x - hi.astype(F32)
    mid = r.astype(BF16)
    lo = (r - mid.astype(F32)).astype(BF16)
    return hi, mid, lo


def _const_spec(shape):
    nd = len(shape)
    return pl.BlockSpec(shape, lambda *_: (0,) * nd, pipeline_mode=pl.Buffered(1))


def _in_proj_kernel(x_ref, g_ref, w_ref, qn_ref, kvn_ref, wq_ref, wqs_ref, wk_ref, wvt_ref,
                    cos_ref, sin_ref, bg_ref,
                    q_out, k_out, vt_out, hq_out, hf_out, hi_out, hg_out, gm_out, gh_out):
    u = _rms(x_ref[...], g_ref[...]).astype(BF16)

    def seg(s):
        return _dot(u, w_ref[:, s[0]:s[1]])

    cos = cos_ref[...]
    sin = sin_ref[...]
    cos_t = jnp.tile(cos, (1, MLA_HEADS))
    sin_t = jnp.tile(sin, (1, MLA_HEADS))
    scale = (MLA_NOPE + MLA_ROPE) ** -0.5 * LOG2_E

    cqn = _rms(seg(SEG_CQ), qn_ref[...]).astype(BF16)
    q = _dot(cqn, wq_ref[...]) * cos_t + _dot(cqn, wqs_ref[...]) * sin_t
    q_out[...] = (q * scale).astype(BF16)

    ckvn = _rms(seg(SEG_CKV), kvn_ref[...]).astype(BF16)
    k_slot = seg(SEG_KPE) * cos + seg(SEG_KPE_SW) * sin
    k_out[...] = (_dot(ckvn, wk_ref[...]) + jnp.tile(k_slot, (1, MLA_HEADS))).astype(BF16)
    vt = _dot_nt(wvt_ref[...], ckvn)
    slot_row = lax.broadcasted_iota(I32, vt.shape, 0) & (SLOT - 1)
    vt = jnp.where(slot_row == ONE_LANE, 1.0, vt).astype(BF16)
    tv = vt_out.shape[2]
    for piece in range(vt_out.shape[0]):
        vt_out[piece] = vt[:, piece * tv:(piece + 1) * tv]

    hq_out[...] = seg(SEG_HQ).astype(BF16)
    hf_out[...] = seg(SEG_HF)
    hi_out[...] = seg(SEG_HI).astype(BF16)
    hg_out[...] = seg(SEG_HG).astype(BF16)
    gm_out[...] = jax.nn.sigmoid(seg(SEG_GM) + bg_ref[0:1, :]).astype(BF16)
    gh_out[...] = jax.nn.sigmoid(seg(SEG_GH) + bg_ref[1:2, :]).astype(BF16)


def _in_proj(x2, tm, tiles_per_seq, cos, sin, wts):
    n, d = x2.shape
    wide = MLA_HEADS * SLOT
    hw = HG_HEADS * HG_DK
    row = lambda width: pl.BlockSpec((tm, width), lambda i: (i, 0))
    tab = pl.BlockSpec((tm, SLOT), lambda i: (i % tiles_per_seq, 0))
    consts = (wts["attn_norm"], wts["w_in"], wts["q_norm"], wts["kv_norm"], wts["wq"], wts["wq_sw"],
              wts["wk"], wts["wv_t"])
    in_specs = [row(d)] + [_const_spec(c.shape) for c in consts] + [tab, tab, _const_spec(wts["b_gate"].shape)]
    widths = (wide, wide, None, hw, hw, hw, hw, d, d)
    dtypes = (BF16, BF16, BF16, BF16, F32, BF16, BF16, BF16, BF16)
    tv = min(tm, ATTN_TILE)
    vt_spec = pl.BlockSpec((tm // tv, wide, tv), lambda i: (i, 0, 0))
    return pl.pallas_call(
        _in_proj_kernel,
        grid=(n // tm,),
        in_specs=in_specs,
        out_specs=[vt_spec if w is None else row(w) for w in widths],
        out_shape=[jax.ShapeDtypeStruct((n // tv, wide, tv) if w is None else (n, w), t)
                   for w, t in zip(widths, dtypes)],
        compiler_params=_params("parallel"),
        name="in_proj",
    )(x2, *consts, cos, sin, wts["b_gate"])


def _attn_kernel(q_ref, k_ref, vt_ref, km_ref, vtm_ref, o_ref, m_ref, acc_ref):
    i = pl.program_id(1)
    tq = q_ref.shape[0]

    heads = [slice(h * SLOT, (h + 1) * SLOT) for h in range(MLA_HEADS)]

    def tile(k_of, vt_of, mask, first=False):
        s = [_dot_nt(k_of(sl), q_ref[:, sl]) for sl in heads]
        if mask is not None:
            s = [jnp.where(mask, x, -jnp.inf) for x in s]
        mx = [jnp.max(x, axis=0, keepdims=True) for x in s]
        if first:
            m_new = mx
        else:
            m_old = [m_ref[h] for h in range(MLA_HEADS)]
            m_new = [jnp.maximum(a, b) for a, b in zip(m_old, mx)]
        p = [jnp.exp2(x - m).astype(BF16) for x, m in zip(s, m_new)]
        for h, sl in enumerate(heads):
            m_ref[h] = m_new[h]
            pv = _dot(vt_of(sl), p[h])
            acc_ref[h] = pv if first else jnp.exp2(m_old[h] - m_new[h]) * acc_ref[h] + pv

    tile(lambda sl: km_ref[:, sl], lambda sl: vtm_ref[0, sl, :], None, first=True)

    def x_tile(j, mask):
        rows = pl.ds(pl.multiple_of(j * tq, tq), tq)
        tile(lambda sl: k_ref[rows, sl], lambda sl: vt_ref[j, sl, :], mask)

    def body(j, carry):
        x_tile(j, None)
        return carry

    lax.fori_loop(0, i, body, 0)
    x_tile(i, lax.broadcasted_iota(I32, (tq, tq), 0) <= lax.broadcasted_iota(I32, (tq, tq), 1))

    outs = []
    for h in range(MLA_HEADS):
        acc = acc_ref[h]
        outs.append(acc[:MLA_V, :] / acc[ONE_LANE:ONE_LANE + 1, :])
    o_ref[...] = jnp.concatenate(outs, axis=0).T.astype(BF16)


def _attention(q, k, vt, k_meta, vt_meta, batch, seq, tq):
    wide = MLA_HEADS * SLOT
    nq = seq // tq
    assert vt.shape == (batch * nq, wide, tq)
    return pl.pallas_call(
        _attn_kernel,
        grid=(batch, nq),
        in_specs=[
            pl.BlockSpec((tq, wide), lambda b, i: (b * nq + i, 0)),
            pl.BlockSpec((seq, wide), lambda b, i: (b, 0)),
            pl.BlockSpec((nq, wide, tq), lambda b, i: (b, 0, 0)),
            _const_spec(k_meta.shape),
            _const_spec(vt_meta.shape),
        ],
        out_specs=pl.BlockSpec((tq, MLA_HEADS * MLA_V), lambda b, i: (b * nq + i, 0)),
        out_shape=jax.ShapeDtypeStruct((batch * seq, MLA_HEADS * MLA_V), BF16),
        scratch_shapes=[pltpu.VMEM((MLA_HEADS, 1, tq), F32), pltpu.VMEM((MLA_HEADS, SLOT, tq), F32)],
        compiler_params=_params("parallel", "arbitrary"),
        name="mla_attention",
    )(q, k, vt, k_meta, vt_meta)


def _lower_bound(lbt_ref):
    t0 = lbt_ref[0:1, :]
    t1 = lbt_ref[1:2, :]
    mx = jnp.maximum(t0, t1)
    e0 = jnp.exp(t0 - mx)
    return e0 / (e0 + jnp.exp(t1 - mx))


def _forget(hf, lb):
    f = lb + (1.0 - lb) * jax.nn.sigmoid(hf)
    return jnp.log(f) * LOG2_E, 1.0 - f


def _cumsum_rows(tril, lf):
    parts = _split3(lf)
    return _dot(tril, parts[0]) + _dot(tril, parts[1]) + _dot(tril, parts[2])


def _tril(n):
    return (lax.broadcasted_iota(I32, (n, n), 0) >= lax.broadcasted_iota(I32, (n, n), 1)).astype(BF16)


def _hgrn_meta_kernel(hf_ref, hi_ref, lbt_ref, s_out):
    lb_all = _lower_bound(lbt_ref)
    tril = _tril(hf_ref.shape[0])
    for h in range(HG_HEADS):
        sl = slice(h * HG_DK, (h + 1) * HG_DK)
        lf, kk = _forget(hf_ref[:, sl], lb_all[:, sl])
        b = _cumsum_rows(tril, lf)
        kdec = kk * jnp.exp2(b[-1:, :] - b)
        s_out[h] = _dot_tn(hi_ref[:, sl], kdec.astype(BF16))


def _hgrn_meta_state(hf, hi, lb_table):
    return pl.pallas_call(
        _hgrn_meta_kernel,
        out_shape=jax.ShapeDtypeStruct((HG_HEADS, HG_DV, HG_DK), F32),
        compiler_params=pltpu.CompilerParams(vmem_limit_bytes=VMEM_LIMIT_BYTES),
        name="hgrn_meta_state",
    )(hf, hi, lb_table)


def _hgrn_kernel(hq_ref, hf_ref, hi_ref, hg_ref, lbt_ref, hgn_ref, s0_ref, o_ref,
                 st_ref, pb_ref, pk_ref, pv_ref):
    c_rows = HG_CHUNK
    n_chunks = hq_ref.shape[0] // c_rows
    st_ref[...] = s0_ref[...]
    pad = jnp.zeros((HG_HEADS, HG_SUB, HG_DK), F32)
    pb_ref[:, 0:HG_SUB, :] = pad
    pk_ref[:, 0:HG_SUB, :] = pad
    pv_ref[:, 0:HG_SUB, :] = pad
    lb_all = _lower_bound(lbt_ref)
    hgn = hgn_ref[...]
    tril = _tril(c_rows)
    ones = jnp.ones((HG_DK, LANES), BF16)
    n_sub = c_rows // HG_SUB
    far = (lax.broadcasted_iota(I32, (c_rows, c_rows), 0) - lax.broadcasted_iota(I32, (c_rows, c_rows), 1)) >= HG_SUB
    heads = range(HG_HEADS)
    cols = [slice(h * HG_DK, (h + 1) * HG_DK) for h in heads]
    stash = slice(HG_SUB, HG_SUB + c_rows)

    def chunk(c, carry):
        rows = pl.ds(pl.multiple_of(c * c_rows, c_rows), c_rows)
        gates = [_forget(hf_ref[rows, sl], lb_all[:, sl]) for sl in cols]
        kk = [g[1] for g in gates]
        b = [_cumsum_rows(tril, g[0]) for g in gates]
        q = [jax.nn.silu(hq_ref[rows, sl].astype(F32)) for sl in cols]
        v_bf = [hi_ref[rows, sl] for sl in cols]
        st = [st_ref[h] for h in heads]
        for h in heads:
            pb_ref[h, stash, :] = b[h]
            pk_ref[h, stash, :] = kk[h]
            pv_ref[h, stash, :] = v_bf[h].astype(F32)

        y = []
        for h in heads:
            xs = [(q[h] * kk[h]).astype(BF16)]
            for d in range(1, HG_SUB):
                shifted = slice(HG_SUB - d, HG_SUB - d + c_rows)
                xs.append((q[h] * pk_ref[h, shifted, :] * jnp.exp2(b[h] - pb_ref[h, shifted, :])).astype(BF16))
            y.append(_dot(jnp.concatenate(xs, axis=0), ones))

        o = [_dot_nt((q[h] * jnp.exp2(b[h])).astype(BF16), st[h].astype(BF16)) for h in heads]

        a_off = []
        for h in heads:
            a = jnp.zeros((c_rows, c_rows), F32)
            for j in range(n_sub - 1):
                k0, k1 = HG_SUB * j, HG_SUB * (j + 1)
                rj = b[h][k1 - 1:k1, :]
                qp = (q[h][k1:, :] * jnp.exp2(b[h][k1:, :] - rj)).astype(BF16)
                kp = (kk[h][k0:k1, :] * jnp.exp2(rj - b[h][k0:k1, :])).astype(BF16)
                zero_rows = lambda r: [jnp.zeros((r, HG_DK), BF16)] if r else []
                qp = jnp.concatenate(zero_rows(k1) + [qp], axis=0)
                kp = jnp.concatenate(zero_rows(k0) + [kp] + zero_rows(c_rows - k1), axis=0)
                a = a + _dot_nt(qp, kp)
            a_off.append(jnp.where(far, a, 0.0).astype(BF16))

        for h in heads:
            o[h] = o[h] + _dot(a_off[h], v_bf[h])
            for d in range(HG_SUB):
                shifted = slice(HG_SUB - d, HG_SUB - d + c_rows)
                o[h] = o[h] + y[h][d * c_rows:(d + 1) * c_rows, :] * pv_ref[h, shifted, :]

        for h, sl in enumerate(cols):
            on = o[h] * lax.rsqrt(jnp.mean(o[h] * o[h], axis=-1, keepdims=True) + EPS)
            on = on * hgn[:, sl] * jax.nn.silu(hg_ref[rows, sl].astype(F32))
            o_ref[rows, sl] = on.astype(BF16)

        for h in heads:
            b_end = b[h][c_rows - 1:c_rows, :]
            kdec = kk[h] * jnp.exp2(b_end - b[h])
            st_ref[h] = st[h] * jnp.exp2(b_end) + _dot_tn(v_bf[h], kdec.astype(BF16))
        return carry

    lax.fori_loop(0, n_chunks, chunk, 0)


def _hgrn(hq, hf, hi, hg, lb_table, hg_norm, s0, batch, seq):
    hw = HG_HEADS * HG_DK
    seq_spec = pl.BlockSpec((seq, hw), lambda b: (b, 0))
    pad_rows = HG_SUB + HG_CHUNK
    return pl.pallas_call(
        _hgrn_kernel,
        grid=(batch,),
        in_specs=[seq_spec, seq_spec, seq_spec, seq_spec,
                  _const_spec(lb_table.shape), _const_spec(hg_norm.shape), _const_spec(s0.shape)],
        out_specs=seq_spec,
        out_shape=jax.ShapeDtypeStruct((batch * seq, hw), BF16),
        scratch_shapes=[pltpu.VMEM((HG_HEADS, HG_DV, HG_DK), F32)]
        + [pltpu.VMEM((HG_HEADS, pad_rows, HG_DK), F32)] * 3,
        compiler_params=_params("parallel"),
        name="hgrn2",
    )(hq, hf, hi, hg, lb_table, hg_norm, s0)


ROUTE_C1, ROUTE_C2, ROUTE_G1, ROUTE_G2 = range(4)
RANK_BITS = 16


def _slot(code, pstart_ref):
    return pstart_ref[code >> RANK_BITS] + (code & ((1 << RANK_BITS) - 1))


def _merge_kernel(om_ref, oh_ref, gm_ref, gh_ref, x_ref, wbm_ref, wbh_ref, wo_ref, fn_ref,
                  wr_hi_ref, wr_lo_ref, br_ref, h_out, u_out, route_out, cnt_out, carry_ref):
    i = pl.program_id(0)
    tm = x_ref.shape[0]

    @pl.when(i == 0)
    def _():
        carry_ref[...] = jnp.zeros_like(carry_ref)

    a = _dot(om_ref[...], wbm_ref[...])
    g = _dot(oh_ref[...], wbh_ref[...])
    merged = gm_ref[...].astype(F32) * a + gh_ref[...].astype(F32) * g
    h1 = x_ref[...] + _dot(merged.astype(BF16), wo_ref[...])
    h_out[...] = h1
    u = _rms(h1, fn_ref[...])
    u_out[...] = u

    u_hi, u_lo = _split2(u)
    logits = (_dot(u_hi, wr_hi_ref[...]) + _dot(u_hi, wr_lo_ref[...]) + _dot(u_lo, wr_hi_ref[...])
              + br_ref[...])
    lane = lax.broadcasted_iota(I32, (tm, LANES), 1)
    lane_f = lane.astype(F32)
    big = float(2 * LANES)

    def first_max(vals):
        mx = jnp.max(vals, axis=-1, keepdims=True)
        idx = jnp.min(jnp.where(vals == mx, lane_f, big), axis=-1, keepdims=True)
        return mx, idx

    gl = jnp.where(lane < N_GROUPS, logits, -jnp.inf)
    g_max, g_sel = first_max(gl)
    p_sel = 1.0 / jnp.sum(jnp.exp(gl - g_max), axis=-1, keepdims=True)
    lo = N_GROUPS + g_sel * EXPERTS_PER_GROUP
    el = jnp.where((lane_f >= lo) & (lane_f < lo + EXPERTS_PER_GROUP), logits, -jnp.inf)
    v1, i1 = first_max(el)
    el2 = jnp.where(lane_f == i1, -jnp.inf, el)
    v2, i2 = first_max(el2)
    t = jnp.exp(v2 - v1)
    g1 = p_sel * (1.0 / (1.0 + t))
    g2 = p_sel * (t / (1.0 + t))

    hit1 = lane_f == i1
    hit2 = lane_f == i2
    onehot = jnp.where(hit1 | hit2, 1.0, 0.0)
    strict = (lax.broadcasted_iota(I32, (tm, tm), 0) > lax.broadcasted_iota(I32, (tm, tm), 1)).astype(BF16)
    before = _dot(strict, onehot.astype(BF16)) + carry_ref[0:1, :]
    r1 = jnp.sum(jnp.where(hit1, before, 0.0), axis=-1, keepdims=True)
    r2 = jnp.sum(jnp.where(hit2, before, 0.0), axis=-1, keepdims=True)
    total = carry_ref[0:1, :] + jnp.sum(onehot, axis=0, keepdims=True)
    carry_ref[...] = jnp.broadcast_to(total, carry_ref.shape)
    cnt_out[...] = jnp.broadcast_to(total, cnt_out.shape)

    route = jnp.zeros((tm, LANES), F32)
    code1 = (i1 - N_GROUPS) * float(1 << RANK_BITS) + r1
    code2 = (i2 - N_GROUPS) * float(1 << RANK_BITS) + r2
    for pos, val in ((ROUTE_C1, code1), (ROUTE_C2, code2), (ROUTE_G1, g1), (ROUTE_G2, g2)):
        route = jnp.where(lane == pos, val, route)
    route_out[...] = route


def _merge(o_mla, o_hg, gm, gh, x2, wts, tm):
    n, d = x2.shape
    row = lambda width: pl.BlockSpec((tm, width), lambda i: (i, 0))
    consts = (wts["w_br_mla"], wts["w_br_hgrn"], wts["w_out"], wts["ffn_norm"],
              wts["w_router_hi"], wts["w_router_lo"], wts["b_router"])
    return pl.pallas_call(
        _merge_kernel,
        grid=(n // tm,),
        in_specs=[row(o_mla.shape[1]), row(o_hg.shape[1]), row(d), row(d), row(d)]
        + [_const_spec(c.shape) for c in consts],
        out_specs=[row(d), row(d), row(LANES), pl.BlockSpec((8, LANES), lambda i: (0, 0))],
        out_shape=[jax.ShapeDtypeStruct((n, d), F32), jax.ShapeDtypeStruct((n, d), F32),
                   jax.ShapeDtypeStruct((n, LANES), F32), jax.ShapeDtypeStruct((8, LANES), F32)],
        scratch_shapes=[pltpu.VMEM((8, LANES), F32)],
        compiler_params=_params("arbitrary"),
        name="merge_route",
    )(o_mla, o_hg, gm, gh, x2, *consts)


ROW_UNROLL = 8


def _row_copy(src_ref, src_row, dst_ref, dst_row, sem):
    return pltpu.make_async_copy(src_ref.at[pl.ds(src_row, 1)], dst_ref.at[pl.ds(dst_row, 1)], sem)


def _dispatch_kernel(c1_ref, c2_ref, ps_ref, u_ref, xs_in, xs_out, sem):
    del xs_in
    tm = u_ref.shape[0]
    base = pl.program_id(0) * tm

    def start(g, carry):
        r0 = pl.multiple_of(g * ROW_UNROLL, ROW_UNROLL)
        for k in range(ROW_UNROLL):
            _row_copy(u_ref, r0 + k, xs_out, _slot(c1_ref[base + r0 + k], ps_ref), sem).start()
            _row_copy(u_ref, r0 + k, xs_out, _slot(c2_ref[base + r0 + k], ps_ref), sem).start()
        return carry

    def wait(g, carry):
        for _ in range(2 * ROW_UNROLL):
            _row_copy(u_ref, 0, xs_out, 0, sem).wait()
        return carry

    lax.fori_loop(0, tm // ROW_UNROLL, start, 0)
    lax.fori_loop(0, tm // ROW_UNROLL, wait, 0)


def _dispatch(code1, code2, p_start, u, xs_zero, tm):
    n, d = u.shape
    return pl.pallas_call(
        _dispatch_kernel,
        grid_spec=pltpu.PrefetchScalarGridSpec(
            num_scalar_prefetch=3,
            grid=(n // tm,),
            in_specs=[pl.BlockSpec((tm, d), lambda i, *_: (i, 0)), pl.BlockSpec(memory_space=pl.ANY)],
            out_specs=pl.BlockSpec(memory_space=pl.ANY),
            scratch_shapes=[pltpu.SemaphoreType.DMA],
        ),
        out_shape=jax.ShapeDtypeStruct(xs_zero.shape, xs_zero.dtype),
        input_output_aliases={4: 0},
        compiler_params=_params("arbitrary"),
        name="moe_dispatch",
    )(code1, code2, p_start, u, xs_zero)


def _expert_kernel(be_ref, nu_ref, xs_ref, w1_ref, w3_ref, w2_ref, y_ref, w13_s, w2_s):
    i = pl.program_id(0)
    used = i < nu_ref[0]

    @pl.when(used)
    def _():
        prev = be_ref[jnp.maximum(i - 1, 0)]

        @pl.when((i == 0) | (be_ref[i] != prev))
        def _():
            w13_s[:, 0:D_EXPERT] = w1_ref[0].astype(BF16)
            w13_s[:, D_EXPERT:2 * D_EXPERT] = w3_ref[0].astype(BF16)
            w2_s[...] = w2_ref[0].astype(BF16)

        hcat = _dot(xs_ref[...].astype(BF16), w13_s[...])
        hdn = jax.nn.silu(hcat[:, 0:D_EXPERT]) * hcat[:, D_EXPERT:2 * D_EXPERT]
        y_ref[...] = _dot(hdn.astype(BF16), w2_s[...])

    @pl.when(jnp.logical_not(used))
    def _():
        y_ref[...] = jnp.zeros_like(y_ref)


def _experts(blk_expert, n_used, xs, w1, w3, w2):
    r, d = xs.shape
    n_blocks = r // MOE_BLOCK
    return pl.pallas_call(
        _expert_kernel,
        grid_spec=pltpu.PrefetchScalarGridSpec(
            num_scalar_prefetch=2,
            grid=(n_blocks,),
            in_specs=[
                pl.BlockSpec((MOE_BLOCK, d), lambda i, be, nu: (i, 0)),
                pl.BlockSpec((1, d, D_EXPERT), lambda i, be, nu: (be[i], 0, 0)),
                pl.BlockSpec((1, d, D_EXPERT), lambda i, be, nu: (be[i], 0, 0)),
                pl.BlockSpec((1, D_EXPERT, d), lambda i, be, nu: (be[i], 0, 0)),
            ],
            out_specs=pl.BlockSpec((MOE_BLOCK, d), lambda i, be, nu: (i, 0)),
            scratch_shapes=[pltpu.VMEM((d, 2 * D_EXPERT), BF16), pltpu.VMEM((D_EXPERT, d), BF16)],
        ),
        out_shape=jax.ShapeDtypeStruct((r, d), F32),
        compiler_params=_params("arbitrary"),
        name="moe_experts",
    )(blk_expert, n_used, xs, w1, w3, w2)


def _combine_kernel(c1_ref, c2_ref, ps_ref, y_ref, h_ref, route_ref, fn_ref, o_ref, buf_ref, sem):
    tm = h_ref.shape[0]
    base = pl.program_id(0) * tm

    def start(g, carry):
        r0 = pl.multiple_of(g * ROW_UNROLL, ROW_UNROLL)
        for k in range(ROW_UNROLL):
            _row_copy(y_ref, _slot(c1_ref[base + r0 + k], ps_ref), buf_ref.at[0], r0 + k, sem).start()
            _row_copy(y_ref, _slot(c2_ref[base + r0 + k], ps_ref), buf_ref.at[1], r0 + k, sem).start()
        return carry

    def wait(g, carry):
        for _ in range(2 * ROW_UNROLL):
            _row_copy(y_ref, 0, buf_ref.at[0], 0, sem).wait()
        return carry

    lax.fori_loop(0, tm // ROW_UNROLL, start, 0)
    lax.fori_loop(0, tm // ROW_UNROLL, wait, 0)
    route = route_ref[...]
    g1 = route[:, ROUTE_G1:ROUTE_G1 + 1]
    g2 = route[:, ROUTE_G2:ROUTE_G2 + 1]
    h2 = h_ref[...] + (g1 * buf_ref[0] + g2 * buf_ref[1])
    o_ref[...] = _rms(h2, fn_ref[...])


def _combine(code1, code2, p_start, y, h1, route, final_norm, tm):
    n, d = h1.shape
    return pl.pallas_call(
        _combine_kernel,
        grid_spec=pltpu.PrefetchScalarGridSpec(
            num_scalar_prefetch=3,
            grid=(n // tm,),
            in_specs=[
                pl.BlockSpec(memory_space=pl.ANY),
                pl.BlockSpec((tm, d), lambda i, *_: (i, 0)),
                pl.BlockSpec((tm, LANES), lambda i, *_: (i, 0)),
                pl.BlockSpec((1, d), lambda i, *_: (0, 0)),
            ],
            out_specs=pl.BlockSpec((tm, d), lambda i, *_: (i, 0)),
            scratch_shapes=[pltpu.VMEM((2, tm, d), F32), pltpu.SemaphoreType.DMA],
        ),
        out_shape=jax.ShapeDtypeStruct((n, d), F32),
        compiler_params=_params("arbitrary"),
        name="moe_combine",
    )(code1, code2, p_start, y, h1, route, final_norm)


def _prepare_weights(attn_norm, w_in, q_norm, w_uq, kv_norm, w_ukv, b_gate, hg_norm, w_br_mla, w_br_hgrn,
                     w_out, ffn_norm, w_group, b_group, w_route, b_route, final_norm):
    d = w_in.shape[0]
    half = MLA_ROPE // 2
    pe0, pe1 = Q_LORA + KV_LORA, Q_LORA + KV_LORA + MLA_ROPE
    w_pe = w_in[:, pe0:pe1]
    zeros = lambda r, c: jnp.zeros((r, c), F32)
    kpe_slot = jnp.concatenate([zeros(d, MLA_NOPE), w_pe, zeros(d, SLOT - MLA_NOPE - MLA_ROPE)], axis=1)
    kpe_swap = jnp.concatenate([zeros(d, MLA_NOPE), -w_pe[:, half:], w_pe[:, :half],
                                zeros(d, SLOT - MLA_NOPE - MLA_ROPE)], axis=1)
    w_wide = jnp.concatenate([w_in[:, :pe0], kpe_slot, kpe_swap, w_in[:, pe1:]], axis=1).astype(BF16)

    wq3 = w_uq.reshape(Q_LORA, MLA_HEADS, MLA_NOPE + MLA_ROPE)
    q_nope, q_pe = wq3[..., :MLA_NOPE], wq3[..., MLA_NOPE:]
    zq = jnp.zeros((Q_LORA, MLA_HEADS, SLOT - MLA_NOPE - MLA_ROPE), F32)
    wq = jnp.concatenate([q_nope, q_pe, zq], axis=-1).reshape(Q_LORA, MLA_HEADS * SLOT).astype(BF16)
    wq_sw = jnp.concatenate([jnp.zeros_like(q_nope), -q_pe[..., half:], q_pe[..., :half], zq],
                            axis=-1).reshape(Q_LORA, MLA_HEADS * SLOT).astype(BF16)

    wkv3 = w_ukv.reshape(KV_LORA, MLA_HEADS, MLA_NOPE + MLA_V)
    zk = jnp.zeros((KV_LORA, MLA_HEADS, SLOT - MLA_NOPE), F32)
    wk = jnp.concatenate([wkv3[..., :MLA_NOPE], zk], axis=-1).reshape(KV_LORA, MLA_HEADS * SLOT).astype(BF16)
    zv = jnp.zeros((KV_LORA, MLA_HEADS, SLOT - MLA_V), F32)
    wv_t = jnp.concatenate([wkv3[..., MLA_NOPE:], zv], axis=-1).reshape(KV_LORA, MLA_HEADS * SLOT).T.astype(BF16)

    w_router = jnp.concatenate([w_group, w_route, zeros(d, LANES - N_GROUPS - N_EXPERTS)], axis=1)
    wr_hi = w_router.astype(BF16)
    wr_lo = (w_router - wr_hi.astype(F32)).astype(BF16)
    b_router = jnp.concatenate([b_group, b_route, jnp.zeros((LANES - N_GROUPS - N_EXPERTS,), F32)])[None, :]
    return {
        "attn_norm": attn_norm[None, :], "w_in": w_wide, "q_norm": q_norm[None, :], "kv_norm": kv_norm[None, :],
        "wq": wq, "wq_sw": wq_sw, "wk": wk, "wv_t": wv_t, "b_gate": b_gate,
        "hg_norm": hg_norm[None, :], "w_br_mla": w_br_mla.astype(BF16), "w_br_hgrn": w_br_hgrn.astype(BF16),
        "w_out": w_out.astype(BF16), "ffn_norm": ffn_norm[None, :], "w_router_hi": wr_hi, "w_router_lo": wr_lo,
        "b_router": b_router, "final_norm": final_norm[None, :],
    }


def _rope_tables(length):
    inv = ROPE_BASE ** (-jnp.arange(0, MLA_ROPE, 2, dtype=F32) / MLA_ROPE)
    ang = jnp.arange(length, dtype=F32)[:, None] * inv[None, :]
    cos, sin = jnp.cos(ang), jnp.sin(ang)
    ones = jnp.ones((length, MLA_NOPE), F32)
    tail = SLOT - MLA_NOPE - MLA_ROPE
    cos_t = jnp.concatenate([ones, cos, cos, jnp.ones((length, tail), F32)], axis=1)
    sin_t = jnp.concatenate([0.0 * ones, sin, sin, jnp.zeros((length, tail), F32)], axis=1)
    return cos_t, sin_t


def kernel(x, meta_tokens, attn_norm, w_in, q_norm, w_uq, kv_norm, w_ukv, lb_table, hg_norm, w_br_mla, w_br_hgrn, b_gate, w_out, ffn_norm, w_group, b_group, w_route, b_route, w1, w3, w2, final_norm):
    batch, seq, d = x.shape
    assert attn_norm.shape[0] == 1, "one layer"
    n = batch * seq
    x2 = x.reshape(n, d)
    wts = _prepare_weights(attn_norm[0], w_in[0], q_norm[0], w_uq[0], kv_norm[0], w_ukv[0], b_gate[0],
                           hg_norm[0], w_br_mla[0], w_br_hgrn[0], w_out[0], ffn_norm[0], w_group[0],
                           b_group[0], w_route[0], b_route[0], final_norm)
    cos_t, sin_t = _rope_tables(N_META + seq)

    meta = _in_proj(meta_tokens.astype(x.dtype), N_META, 1, cos_t[:N_META], sin_t[:N_META], wts)
    s0 = _hgrn_meta_state(meta[4], meta[5], lb_table)

    tm = min(ROW_TILE, seq)
    q, k, vt, hq, hf, hi, hg, gm, gh = _in_proj(x2, tm, seq // tm, cos_t[N_META:], sin_t[N_META:], wts)
    o_mla = _attention(q, k, vt, meta[1], meta[2], batch, seq, min(ATTN_TILE, seq))
    o_hg = _hgrn(hq, hf, hi, hg, lb_table, wts["hg_norm"], s0, batch, seq)
    h1, u, route, counts = _merge(o_mla, o_hg, gm, gh, x2, wts, tm)

    code1 = route[:, ROUTE_C1].astype(I32)
    code2 = route[:, ROUTE_C2].astype(I32)
    cnt = counts[0, N_GROUPS:N_GROUPS + N_EXPERTS].astype(I32)
    padded = (cnt + MOE_BLOCK - 1) // MOE_BLOCK * MOE_BLOCK
    p_end = jnp.cumsum(padded)
    p_start = p_end - padded
    n_blocks = -(-2 * n // MOE_BLOCK) + N_EXPERTS
    blk_row = jnp.arange(n_blocks, dtype=I32) * MOE_BLOCK
    blk_expert = jnp.minimum(jnp.sum((p_end[None, :] <= blk_row[:, None]).astype(I32), axis=1), N_EXPERTS - 1)
    n_used = p_end[-1:] // MOE_BLOCK

    tmove = min(MOVE_TILE, seq)
    xs = _dispatch(code1, code2, p_start, u, jnp.zeros((n_blocks * MOE_BLOCK, d), F32), tmove)
    y = _experts(blk_expert, n_used, xs, w1[0], w3[0], w2[0])
    out = _combine(code1, code2, p_start, y, h1, route, wts["final_norm"], tmove)
    return out.reshape(batch, seq, d)
```

```python
import functools

import jax
import jax.numpy as jnp
from jax import lax
from jax.experimental import pallas as pl
from jax.experimental.pallas import tpu as pltpu

F32 = jnp.float32
BF16 = jnp.bfloat16
I32 = jnp.int32

N_META = 16
EPS = 1e-6
MLA_HEADS = 8
MLA_NOPE = 64
MLA_ROPE = 32
MLA_V = 64
Q_LORA = 384
KV_LORA = 256
ROPE_BASE = 10000.0
HG_HEADS = 4
HG_DK = 128
HG_DV = 128
N_GROUPS = 8
EXPERTS_PER_GROUP = 8
N_EXPERTS = N_GROUPS * EXPERTS_PER_GROUP
D_EXPERT = 256
MOE_BLOCK = 256

LANES = 128
VMEM_LIMIT_BYTES = 56 * 1024 * 1024

SLOT = LANES
ONE_LANE = MLA_V
ROW_TILE = 512
ATTN_TILE = 256
MOVE_TILE = 256
HG_CHUNK = 64
HG_SUB = 16
LOG2_E = 1.4426950408889634

SEG_CQ = (0, 384)
SEG_CKV = (384, 640)
SEG_KPE = (640, 768)
SEG_KPE_SW = (768, 896)
SEG_HQ = (896, 1408)
SEG_HF = (1408, 1920)
SEG_HI = (1920, 2432)
SEG_HG = (2432, 2944)
SEG_GM = (2944, 3968)
SEG_GH = (3968, 4992)


def _params(*sem):
    return pltpu.CompilerParams(dimension_semantics=sem, vmem_limit_bytes=VMEM_LIMIT_BYTES)


def _rms(x, g):
    return x * lax.rsqrt(jnp.mean(x * x, axis=-1, keepdims=True) + EPS) * g


def _dot(a, b):
    return jnp.dot(a, b, preferred_element_type=F32)


def _dot_nt(a, b):
    return lax.dot_general(a, b, (((1,), (1,)), ((), ())), preferred_element_type=F32)


def _dot_tn(a, b):
    return lax.dot_general(a, b, (((0,), (0,)), ((), ())), preferred_element_type=F32)


def _split2(x):
    hi = x.astype(BF16)
    lo = (x - hi.astype(F32)).astype(BF16)
    return hi, lo


def _split3(x):
    hi = x.astype(BF16)
    r = x - hi.astype(F32)
    mid = r.astype(BF16)
    lo = (r - mid.astype(F32)).astype(BF16)
    return hi, mid, lo


def _const_spec(shape):
    nd = len(shape)
    return pl.BlockSpec(shape, lambda *_: (0,) * nd, pipeline_mode=pl.Buffered(1))


def _in_proj_kernel(x_ref, g_ref, w_ref, qn_ref, kvn_ref, wq_ref, wqs_ref, wk_ref, wvt_ref,
                    cos_ref, sin_ref, bg_ref,
                    q_out, k_out, vt_out, hq_out, hf_out, hi_out, hg_out, gm_out, gh_out):
    u = _rms(x_ref[...], g_ref[...]).astype(BF16)

    def seg(s):
        return _dot(u, w_ref[:, s[0]:s[1]])

    cos = cos_ref[...]
    sin = sin_ref[...]
    cos_t = jnp.tile(cos, (1, MLA_HEADS))
    sin_t = jnp.tile(sin, (1, MLA_HEADS))
    scale = (MLA_NOPE + MLA_ROPE) ** -0.5 * LOG2_E

    cqn = _rms(seg(SEG_CQ), qn_ref[...]).astype(BF16)
    q = _dot(cqn, wq_ref[...]) * cos_t + _dot(cqn, wqs_ref[...]) * sin_t
    q_out[...] = (q * scale).astype(BF16)

    ckvn = _rms(seg(SEG_CKV), kvn_ref[...]).astype(BF16)
    k_slot = seg(SEG_KPE) * cos + seg(SEG_KPE_SW) * sin
    k_out[...] = (_dot(ckvn, wk_ref[...]) + jnp.tile(k_slot, (1, MLA_HEADS))).astype(BF16)
    vt = _dot_nt(wvt_ref[...], ckvn)
    slot_row = lax.broadcasted_iota(I32, vt.shape, 0) & (SLOT - 1)
    vt = jnp.where(slot_row == ONE_LANE, 1.0, vt).astype(BF16)
    tv = vt_out.shape[2]
    for piece in range(vt_out.shape[0]):
        vt_out[piece] = vt[:, piece * tv:(piece + 1) * tv]

    hq_out[...] = seg(SEG_HQ).astype(BF16)
    hf_out[...] = seg(SEG_HF)
    hi_out[...] = seg(SEG_HI).astype(BF16)
    hg_out[...] = seg(SEG_HG).astype(BF16)
    gm_out[...] = jax.nn.sigmoid(seg(SEG_GM) + bg_ref[0:1, :]).astype(BF16)
    gh_out[...] = jax.nn.sigmoid(seg(SEG_GH) + bg_ref[1:2, :]).astype(BF16)


def _in_proj(x2, tm, tiles_per_seq, cos, sin, wts):
    n, d = x2.shape
    wide = MLA_HEADS * SLOT
    hw = HG_HEADS * HG_DK
    row = lambda width: pl.BlockSpec((tm, width), lambda i: (i, 0))
    tab = pl.BlockSpec((tm, SLOT), lambda i: (i % tiles_per_seq, 0))
    consts = (wts["attn_norm"], wts["w_in"], wts["q_norm"], wts["kv_norm"], wts["wq"], wts["wq_sw"],
              wts["wk"], wts["wv_t"])
    in_specs = [row(d)] + [_const_spec(c.shape) for c in consts] + [tab, tab, _const_spec(wts["b_gate"].shape)]
    widths = (wide, wide, None, hw, hw, hw, hw, d, d)
    dtypes = (BF16, BF16, BF16, BF16, F32, BF16, BF16, BF16, BF16)
    tv = min(tm, ATTN_TILE)
    vt_spec = pl.BlockSpec((tm // tv, wide, tv), lambda i: (i, 0, 0))
    return pl.pallas_call(
        _in_proj_kernel,
        grid=(n // tm,),
        in_specs=in_specs,
        out_specs=[vt_spec if w is None else row(w) for w in widths],
        out_shape=[jax.ShapeDtypeStruct((n // tv, wide, tv) if w is None else (n, w), t)
                   for w, t in zip(widths, dtypes)],
        compiler_params=_params("parallel"),
        name="in_proj",
    )(x2, *consts, cos, sin, wts["b_gate"])


def _attn_kernel(q_ref, k_ref, vt_ref, km_ref, vtm_ref, o_ref, m_ref, acc_ref):
    i = pl.program_id(1)
    tq = q_ref.shape[0]

    heads = [slice(h * SLOT, (h + 1) * SLOT) for h in range(MLA_HEADS)]

    def tile(k_of, vt_of, mask, first=False):
        s = [_dot_nt(k_of(sl), q_ref[:, sl]) for sl in heads]
        if mask is not None:
            s = [jnp.where(mask, x, -jnp.inf) for x in s]
        mx = [jnp.max(x, axis=0, keepdims=True) for x in s]
        if first:
            m_new = mx
        else:
            m_old = [m_ref[h] for h in range(MLA_HEADS)]
            m_new = [jnp.maximum(a, b) for a, b in zip(m_old, mx)]
        p = [jnp.exp2(x - m).astype(BF16) for x, m in zip(s, m_new)]
        for h, sl in enumerate(heads):
            m_ref[h] = m_new[h]
            pv = _dot(vt_of(sl), p[h])
            acc_ref[h] = pv if first else jnp.exp2(m_old[h] - m_new[h]) * acc_ref[h] + pv

    tile(lambda sl: km_ref[:, sl], lambda sl: vtm_ref[0, sl, :], None, first=True)

    def x_tile(j, mask):
        rows = pl.ds(pl.multiple_of(j * tq, tq), tq)
        tile(lambda sl: k_ref[rows, sl], lambda sl: vt_ref[j, sl, :], mask)

    def body(j, carry):
        x_tile(j, None)
        return carry

    lax.fori_loop(0, i, body, 0)
    x_tile(i, lax.broadcasted_iota(I32, (tq, tq), 0) <= lax.broadcasted_iota(I32, (tq, tq), 1))

    outs = []
    for h in range(MLA_HEADS):
        acc = acc_ref[h]
        outs.append(acc[:MLA_V, :] / acc[ONE_LANE:ONE_LANE + 1, :])
    o_ref[...] = jnp.concatenate(outs, axis=0).T.astype(BF16)


def _attention(q, k, vt, k_meta, vt_meta, batch, seq, tq):
    wide = MLA_HEADS * SLOT
    nq = seq // tq
    assert vt.shape == (batch * nq, wide, tq)
    return pl.pallas_call(
        _attn_kernel,
        grid=(batch, nq),
        in_specs=[
            pl.BlockSpec((tq, wide), lambda b, i: (b * nq + i, 0)),
            pl.BlockSpec((seq, wide), lambda b, i: (b, 0)),
            pl.BlockSpec((nq, wide, tq), lambda b, i: (b, 0, 0)),
            _const_spec(k_meta.shape),
            _const_spec(vt_meta.shape),
        ],
        out_specs=pl.BlockSpec((tq, MLA_HEADS * MLA_V), lambda b, i: (b * nq + i, 0)),
        out_shape=jax.ShapeDtypeStruct((batch * seq, MLA_HEADS * MLA_V), BF16),
        scratch_shapes=[pltpu.VMEM((MLA_HEADS, 1, tq), F32), pltpu.VMEM((MLA_HEADS, SLOT, tq), F32)],
        compiler_params=_params("parallel", "arbitrary"),
        name="mla_attention",
    )(q, k, vt, k_meta, vt_meta)


def _lower_bound(lbt_ref):
    t0 = lbt_ref[0:1, :]
    t1 = lbt_ref[1:2, :]
    mx = jnp.maximum(t0, t1)
    e0 = jnp.exp(t0 - mx)
    return e0 / (e0 + jnp.exp(t1 - mx))


def _forget(hf, lb):
    f = lb + (1.0 - lb) * jax.nn.sigmoid(hf)
    return jnp.log(f) * LOG2_E, 1.0 - f


def _cumsum_rows(tril, lf):
    parts = _split3(lf)
    return _dot(tril, parts[0]) + _dot(tril, parts[1]) + _dot(tril, parts[2])


def _tril(n):
    return (lax.broadcasted_iota(I32, (n, n), 0) >= lax.broadcasted_iota(I32, (n, n), 1)).astype(BF16)


def _hgrn_meta_kernel(hf_ref, hi_ref, lbt_ref, s_out):
    lb_all = _lower_bound(lbt_ref)
    tril = _tril(hf_ref.shape[0])
    for h in range(HG_HEADS):
        sl = slice(h * HG_DK, (h + 1) * HG_DK)
        lf, kk = _forget(hf_ref[:, sl], lb_all[:, sl])
        b = _cumsum_rows(tril, lf)
        kdec = kk * jnp.exp2(b[-1:, :] - b)
        s_out[h] = _dot_tn(hi_ref[:, sl], kdec.astype(BF16))


def _hgrn_meta_state(hf, hi, lb_table):
    return pl.pallas_call(
        _hgrn_meta_kernel,
        out_shape=jax.ShapeDtypeStruct((HG_HEADS, HG_DV, HG_DK), F32),
        compiler_params=pltpu.CompilerParams(vmem_limit_bytes=VMEM_LIMIT_BYTES),
        name="hgrn_meta_state",
    )(hf, hi, lb_table)


def _hgrn_kernel(hq_ref, hf_ref, hi_ref, hg_ref, lbt_ref, hgn_ref, s0_ref, o_ref,
                 st_ref, pb_ref, pk_ref, pv_ref):
    c_rows = HG_CHUNK
    n_chunks = hq_ref.shape[0] // c_rows
    st_ref[...] = s0_ref[...]
    pad = jnp.zeros((HG_HEADS, HG_SUB, HG_DK), F32)
    pb_ref[:, 0:HG_SUB, :] = pad
    pk_ref[:, 0:HG_SUB, :] = pad
    pv_ref[:, 0:HG_SUB, :] = pad
    lb_all = _lower_bound(lbt_ref)
    hgn = hgn_ref[...]
    tril = _tril(c_rows)
    ones = jnp.ones((HG_DK, LANES), BF16)
    n_sub = c_rows // HG_SUB
    far = (lax.broadcasted_iota(I32, (c_rows, c_rows), 0) - lax.broadcasted_iota(I32, (c_rows, c_rows), 1)) >= HG_SUB
    heads = range(HG_HEADS)
    cols = [slice(h * HG_DK, (h + 1) * HG_DK) for h in heads]
    stash = slice(HG_SUB, HG_SUB + c_rows)

    def chunk(c, carry):
        rows = pl.ds(pl.multiple_of(c * c_rows, c_rows), c_rows)
        gates = [_forget(hf_ref[rows, sl], lb_all[:, sl]) for sl in cols]
        kk = [g[1] for g in gates]
        b = [_cumsum_rows(tril, g[0]) for g in gates]
        q = [jax.nn.silu(hq_ref[rows, sl].astype(F32)) for sl in cols]
        v_bf = [hi_ref[rows, sl] for sl in cols]
        st = [st_ref[h] for h in heads]
        for h in heads:
            pb_ref[h, stash, :] = b[h]
            pk_ref[h, stash, :] = kk[h]
            pv_ref[h, stash, :] = v_bf[h].astype(F32)

        y = []
        for h in heads:
            xs = [(q[h] * kk[h]).astype(BF16)]
            for d in range(1, HG_SUB):
                shifted = slice(HG_SUB - d, HG_SUB - d + c_rows)
                xs.append((q[h] * pk_ref[h, shifted, :] * jnp.exp2(b[h] - pb_ref[h, shifted, :])).astype(BF16))
            y.append(_dot(jnp.concatenate(xs, axis=0), ones))

        o = [_dot_nt((q[h] * jnp.exp2(b[h])).astype(BF16), st[h].astype(BF16)) for h in heads]

        a_off = []
        for h in heads:
            a = jnp.zeros((c_rows, c_rows), F32)
            for j in range(n_sub - 1):
                k0, k1 = HG_SUB * j, HG_SUB * (j + 1)
                rj = b[h][k1 - 1:k1, :]
                qp = (q[h][k1:, :] * jnp.exp2(b[h][k1:, :] - rj)).astype(BF16)
                kp = (kk[h][k0:k1, :] * jnp.exp2(rj - b[h][k0:k1, :])).astype(BF16)
                zero_rows = lambda r: [jnp.zeros((r, HG_DK), BF16)] if r else []
                qp = jnp.concatenate(zero_rows(k1) + [qp], axis=0)
                kp = jnp.concatenate(zero_rows(k0) + [kp] + zero_rows(c_rows - k1), axis=0)
                a = a + _dot_nt(qp, kp)
            a_off.append(jnp.where(far, a, 0.0).astype(BF16))

        for h in heads:
            o[h] = o[h] + _dot(a_off[h], v_bf[h])
            for d in range(HG_SUB):
                shifted = slice(HG_SUB - d, HG_SUB - d + c_rows)
                o[h] = o[h] + y[h][d * c_rows:(d + 1) * c_rows, :] * pv_ref[h, shifted, :]

        for h, sl in enumerate(cols):
            on = o[h] * lax.rsqrt(jnp.mean(o[h] * o[h], axis=-1, keepdims=True) + EPS)
            on = on * hgn[:, sl] * jax.nn.silu(hg_ref[rows, sl].astype(F32))
            o_ref[rows, sl] = on.astype(BF16)

        for h in heads:
            b_end = b[h][c_rows - 1:c_rows, :]
            kdec = kk[h] * jnp.exp2(b_end - b[h])
            st_ref[h] = st[h] * jnp.exp2(b_end) + _dot_tn(v_bf[h], kdec.astype(BF16))
        return carry

    lax.fori_loop(0, n_chunks, chunk, 0)


def _hgrn(hq, hf, hi, hg, lb_table, hg_norm, s0, batch, seq):
    hw = HG_HEADS * HG_DK
    seq_spec = pl.BlockSpec((seq, hw), lambda b: (b, 0))
    pad_rows = HG_SUB + HG_CHUNK
    return pl.pallas_call(
        _hgrn_kernel,
        grid=(batch,),
        in_specs=[seq_spec, seq_spec, seq_spec, seq_spec,
                  _const_spec(lb_table.shape), _const_spec(hg_norm.shape), _const_spec(s0.shape)],
        out_specs=seq_spec,
        out_shape=jax.ShapeDtypeStruct((batch * seq, hw), BF16),
        scratch_shapes=[pltpu.VMEM((HG_HEADS, HG_DV, HG_DK), F32)]
        + [pltpu.VMEM((HG_HEADS, pad_rows, HG_DK), F32)] * 3,
        compiler_params=_params("parallel"),
        name="hgrn2",
    )(hq, hf, hi, hg, lb_table, hg_norm, s0)


ROUTE_CODE, ROUTE_E1, ROUTE_E2, ROUTE_G1, ROUTE_G2 = range(5)
RANK_BITS = 16


def _slot(code, pstart_ref):
    return pstart_ref[code >> RANK_BITS] + (code & ((1 << RANK_BITS) - 1))


def _merge_kernel(om_ref, oh_ref, gm_ref, gh_ref, x_ref, wbm_ref, wbh_ref, wo_ref, fn_ref,
                  wr_hi_ref, wr_lo_ref, br_ref, h_out, u_out, cnt_out, carry_ref):
    i = pl.program_id(0)
    tm, d = x_ref.shape

    @pl.when(i == 0)
    def _():
        carry_ref[...] = jnp.zeros_like(carry_ref)

    a = _dot(om_ref[...], wbm_ref[...])
    g = _dot(oh_ref[...], wbh_ref[...])
    merged = gm_ref[...].astype(F32) * a + gh_ref[...].astype(F32) * g
    h1 = x_ref[...] + _dot(merged.astype(BF16), wo_ref[...])
    h_out[...] = h1
    u = _rms(h1, fn_ref[...])
    u_out[:, 0:d] = u

    u_hi, u_lo = _split2(u)
    logits = (_dot(u_hi, wr_hi_ref[...]) + _dot(u_hi, wr_lo_ref[...]) + _dot(u_lo, wr_hi_ref[...])
              + br_ref[...])
    lane = lax.broadcasted_iota(I32, (tm, LANES), 1)
    lane_f = lane.astype(F32)
    big = float(2 * LANES)

    def first_max(vals):
        mx = jnp.max(vals, axis=-1, keepdims=True)
        idx = jnp.min(jnp.where(vals == mx, lane_f, big), axis=-1, keepdims=True)
        return mx, idx

    gl = jnp.where(lane < N_GROUPS, logits, -jnp.inf)
    g_max, g_sel = first_max(gl)
    p_sel = 1.0 / jnp.sum(jnp.exp(gl - g_max), axis=-1, keepdims=True)
    lo = N_GROUPS + g_sel * EXPERTS_PER_GROUP
    el = jnp.where((lane_f >= lo) & (lane_f < lo + EXPERTS_PER_GROUP), logits, -jnp.inf)
    v1, i1 = first_max(el)
    el2 = jnp.where(lane_f == i1, -jnp.inf, el)
    v2, i2 = first_max(el2)
    t = jnp.exp(v2 - v1)
    g1 = p_sel * (1.0 / (1.0 + t))
    g2 = p_sel * (t / (1.0 + t))

    hit = lane_f == g_sel
    onehot = jnp.where(hit, 1.0, 0.0)
    strict = (lax.broadcasted_iota(I32, (tm, tm), 0) > lax.broadcasted_iota(I32, (tm, tm), 1)).astype(BF16)
    before = _dot(strict, onehot.astype(BF16)) + carry_ref[0:1, :]
    rank = jnp.sum(jnp.where(hit, before, 0.0), axis=-1, keepdims=True)
    total = carry_ref[0:1, :] + jnp.sum(onehot, axis=0, keepdims=True)
    carry_ref[...] = jnp.broadcast_to(total, carry_ref.shape)
    cnt_out[...] = jnp.broadcast_to(total, cnt_out.shape)

    route = jnp.zeros((tm, LANES), F32)
    code = g_sel * float(1 << RANK_BITS) + rank
    for pos, val in ((ROUTE_CODE, code), (ROUTE_E1, i1 - N_GROUPS), (ROUTE_E2, i2 - N_GROUPS),
                     (ROUTE_G1, g1), (ROUTE_G2, g2)):
        route = jnp.where(lane == pos, val, route)
    u_out[:, d:d + LANES] = route


def _merge(o_mla, o_hg, gm, gh, x2, wts, tm):
    n, d = x2.shape
    row = lambda width: pl.BlockSpec((tm, width), lambda i: (i, 0))
    consts = (wts["w_br_mla"], wts["w_br_hgrn"], wts["w_out"], wts["ffn_norm"],
              wts["w_router_hi"], wts["w_router_lo"], wts["b_router"])
    return pl.pallas_call(
        _merge_kernel,
        grid=(n // tm,),
        in_specs=[row(o_mla.shape[1]), row(o_hg.shape[1]), row(d), row(d), row(d)]
        + [_const_spec(c.shape) for c in consts],
        out_specs=[row(d), row(d + LANES), pl.BlockSpec((8, LANES), lambda i: (0, 0))],
        out_shape=[jax.ShapeDtypeStruct((n, d), F32), jax.ShapeDtypeStruct((n, d + LANES), F32),
                   jax.ShapeDtypeStruct((8, LANES), F32)],
        scratch_shapes=[pltpu.VMEM((8, LANES), F32)],
        compiler_params=_params("arbitrary"),
        name="merge_route",
    )(o_mla, o_hg, gm, gh, x2, *consts)


ROW_UNROLL = 8


def _row_copy(src_ref, src_row, dst_ref, dst_row, sem):
    return pltpu.make_async_copy(src_ref.at[pl.ds(src_row, 1)], dst_ref.at[pl.ds(dst_row, 1)], sem)


def _dispatch_kernel(code_ref, ps_ref, u_ref, xg_in, xg_out, sem):
    del xg_in
    tm = u_ref.shape[0]
    base = pl.program_id(0) * tm

    def start(g, carry):
        r0 = pl.multiple_of(g * ROW_UNROLL, ROW_UNROLL)
        for k in range(ROW_UNROLL):
            _row_copy(u_ref, r0 + k, xg_out, _slot(code_ref[base + r0 + k], ps_ref), sem).start()
        return carry

    def wait(g, carry):
        for _ in range(ROW_UNROLL):
            _row_copy(u_ref, 0, xg_out, 0, sem).wait()
        return carry

    lax.fori_loop(0, tm // ROW_UNROLL, start, 0)
    lax.fori_loop(0, tm // ROW_UNROLL, wait, 0)


def _dispatch(code, p_start, u, xg_zero, tm):
    n, w = u.shape
    return pl.pallas_call(
        _dispatch_kernel,
        grid_spec=pltpu.PrefetchScalarGridSpec(
            num_scalar_prefetch=2,
            grid=(n // tm,),
            in_specs=[pl.BlockSpec((tm, w), lambda i, *_: (i, 0)), pl.BlockSpec(memory_space=pl.ANY)],
            out_specs=pl.BlockSpec(memory_space=pl.ANY),
            scratch_shapes=[pltpu.SemaphoreType.DMA],
        ),
        out_shape=jax.ShapeDtypeStruct(xg_zero.shape, xg_zero.dtype),
        input_output_aliases={3: 0},
        compiler_params=_params("arbitrary"),
        name="moe_dispatch",
    )(code, p_start, u, xg_zero)


def _expert_kernel(bg_ref, nu_ref, xg_ref, w1_ref, w3_ref, w2_ref, y_ref, w13_s, w2_s):
    i = pl.program_id(0)
    d = y_ref.shape[1]
    used = i < nu_ref[0]

    @pl.when(used)
    def _():
        group = bg_ref[i]

        @pl.when((i == 0) | (group != bg_ref[jnp.maximum(i - 1, 0)]))
        def _():
            for e in range(EXPERTS_PER_GROUP):
                w13_s[e, :, 0:D_EXPERT] = w1_ref[e].astype(BF16)
                w13_s[e, :, D_EXPERT:2 * D_EXPERT] = w3_ref[e].astype(BF16)
                w2_s[e] = w2_ref[e].astype(BF16)

        x = xg_ref[:, 0:d].astype(BF16)
        route = xg_ref[:, d:d + LANES]
        e1 = route[:, ROUTE_E1:ROUTE_E1 + 1]
        e2 = route[:, ROUTE_E2:ROUTE_E2 + 1]
        g1 = route[:, ROUTE_G1:ROUTE_G1 + 1]
        g2 = route[:, ROUTE_G2:ROUTE_G2 + 1]
        first = (group * EXPERTS_PER_GROUP).astype(F32)
        hidden = []
        for e in range(EXPERTS_PER_GROUP):
            hcat = _dot(x, w13_s[e])
            hidden.append((jax.nn.silu(hcat[:, 0:D_EXPERT]) * hcat[:, D_EXPERT:2 * D_EXPERT]).astype(BF16))
        for e in range(EXPERTS_PER_GROUP):
            is1 = e1 == first + e
            is2 = e2 == first + e
            gate = jnp.where(is1, g1, 0.0) + jnp.where(is2, g2, 0.0)
            part = jnp.where(is1 | is2, gate * _dot(hidden[e], w2_s[e]), 0.0)
            if e == 0:
                y_ref[...] = part
            else:
                y_ref[...] += part

    @pl.when(jnp.logical_not(used))
    def _():
        y_ref[...] = jnp.zeros_like(y_ref)


def _experts(blk_group, n_used, xg, w1, w3, w2):
    r, w = xg.shape
    d = w - LANES
    n_blocks = r // MOE_BLOCK
    last_used = lambda i, nu: jnp.minimum(i, nu[0] - 1)
    group_weights = lambda shape: pl.BlockSpec((EXPERTS_PER_GROUP,) + shape, lambda i, bg, nu: (bg[i], 0, 0),
                                               pipeline_mode=pl.Buffered(1))
    return pl.pallas_call(
        _expert_kernel,
        grid_spec=pltpu.PrefetchScalarGridSpec(
            num_scalar_prefetch=2,
            grid=(n_blocks,),
            in_specs=[
                pl.BlockSpec((MOE_BLOCK, w), lambda i, bg, nu: (last_used(i, nu), 0)),
                group_weights((d, D_EXPERT)),
                group_weights((d, D_EXPERT)),
                group_weights((D_EXPERT, d)),
            ],
            out_specs=pl.BlockSpec((MOE_BLOCK, d), lambda i, bg, nu: (i, 0)),
            scratch_shapes=[pltpu.VMEM((EXPERTS_PER_GROUP, d, 2 * D_EXPERT), BF16),
                            pltpu.VMEM((EXPERTS_PER_GROUP, D_EXPERT, d), BF16)],
        ),
        out_shape=jax.ShapeDtypeStruct((r, d), F32),
        compiler_params=_params("arbitrary"),
        name="moe_experts",
    )(blk_group, n_used, xg, w1, w3, w2)


def _combine_kernel(code_ref, ps_ref, y_ref, h_ref, fn_ref, o_ref, buf_ref, sem):
    tm = h_ref.shape[0]
    base = pl.program_id(0) * tm

    def start(g, carry):
        r0 = pl.multiple_of(g * ROW_UNROLL, ROW_UNROLL)
        for k in range(ROW_UNROLL):
            _row_copy(y_ref, _slot(code_ref[base + r0 + k], ps_ref), buf_ref, r0 + k, sem).start()
        return carry

    def wait(g, carry):
        for _ in range(ROW_UNROLL):
            _row_copy(y_ref, 0, buf_ref, 0, sem).wait()
        return carry

    lax.fori_loop(0, tm // ROW_UNROLL, start, 0)
    lax.fori_loop(0, tm // ROW_UNROLL, wait, 0)
    o_ref[...] = _rms(h_ref[...] + buf_ref[...], fn_ref[...])


def _combine(code, p_start, y, h1, final_norm, tm):
    n, d = h1.shape
    return pl.pallas_call(
        _combine_kernel,
        grid_spec=pltpu.PrefetchScalarGridSpec(
            num_scalar_prefetch=2,
            grid=(n // tm,),
            in_specs=[
                pl.BlockSpec(memory_space=pl.ANY),
                pl.BlockSpec((tm, d), lambda i, *_: (i, 0)),
                pl.BlockSpec((1, d), lambda i, *_: (0, 0)),
            ],
            out_specs=pl.BlockSpec((tm, d), lambda i, *_: (i, 0)),
            scratch_shapes=[pltpu.VMEM((tm, d), F32), pltpu.SemaphoreType.DMA],
        ),
        out_shape=jax.ShapeDtypeStruct((n, d), F32),
        compiler_params=_params("arbitrary"),
        name="moe_combine",
    )(code, p_start, y, h1, final_norm)


def _prepare_weights(attn_norm, w_in, q_norm, w_uq, kv_norm, w_ukv, b_gate, hg_norm, w_br_mla, w_br_hgrn,
                     w_out, ffn_norm, w_group, b_group, w_route, b_route, final_norm):
    d = w_in.shape[0]
    half = MLA_ROPE // 2
    pe0, pe1 = Q_LORA + KV_LORA, Q_LORA + KV_LORA + MLA_ROPE
    w_pe = w_in[:, pe0:pe1]
    zeros = lambda r, c: jnp.zeros((r, c), F32)
    kpe_slot = jnp.concatenate([zeros(d, MLA_NOPE), w_pe, zeros(d, SLOT - MLA_NOPE - MLA_ROPE)], axis=1)
    kpe_swap = jnp.concatenate([zeros(d, MLA_NOPE), -w_pe[:, half:], w_pe[:, :half],
                                zeros(d, SLOT - MLA_NOPE - MLA_ROPE)], axis=1)
    w_wide = jnp.concatenate([w_in[:, :pe0], kpe_slot, kpe_swap, w_in[:, pe1:]], axis=1).astype(BF16)

    wq3 = w_uq.reshape(Q_LORA, MLA_HEADS, MLA_NOPE + MLA_ROPE)
    q_nope, q_pe = wq3[..., :MLA_NOPE], wq3[..., MLA_NOPE:]
    zq = jnp.zeros((Q_LORA, MLA_HEADS, SLOT - MLA_NOPE - MLA_ROPE), F32)
    wq = jnp.concatenate([q_nope, q_pe, zq], axis=-1).reshape(Q_LORA, MLA_HEADS * SLOT).astype(BF16)
    wq_sw = jnp.concatenate([jnp.zeros_like(q_nope), -q_pe[..., half:], q_pe[..., :half], zq],
                            axis=-1).reshape(Q_LORA, MLA_HEADS * SLOT).astype(BF16)

    wkv3 = w_ukv.reshape(KV_LORA, MLA_HEADS, MLA_NOPE + MLA_V)
    zk = jnp.zeros((KV_LORA, MLA_HEADS, SLOT - MLA_NOPE), F32)
    wk = jnp.concatenate([wkv3[..., :MLA_NOPE], zk], axis=-1).reshape(KV_LORA, MLA_HEADS * SLOT).astype(BF16)
    zv = jnp.zeros((KV_LORA, MLA_HEADS, SLOT - MLA_V), F32)
    wv_t = jnp.concatenate([wkv3[..., MLA_NOPE:], zv], axis=-1).reshape(KV_LORA, MLA_HEADS * SLOT).T.astype(BF16)

    w_router = jnp.concatenate([w_group, w_route, zeros(d, LANES - N_GROUPS - N_EXPERTS)], axis=1)
    wr_hi = w_router.astype(BF16)
    wr_lo = (w_router - wr_hi.astype(F32)).astype(BF16)
    b_router = jnp.concatenate([b_group, b_route, jnp.zeros((LANES - N_GROUPS - N_EXPERTS,), F32)])[None, :]
    return {
        "attn_norm": attn_norm[None, :], "w_in": w_wide, "q_norm": q_norm[None, :], "kv_norm": kv_norm[None, :],
        "wq": wq, "wq_sw": wq_sw, "wk": wk, "wv_t": wv_t, "b_gate": b_gate,
        "hg_norm": hg_norm[None, :], "w_br_mla": w_br_mla.astype(BF16), "w_br_hgrn": w_br_hgrn.astype(BF16),
        "w_out": w_out.astype(BF16), "ffn_norm": ffn_norm[None, :], "w_router_hi": wr_hi, "w_router_lo": wr_lo,
        "b_router": b_router, "final_norm": final_norm[None, :],
    }


def _rope_tables(length):
    inv = ROPE_BASE ** (-jnp.arange(0, MLA_ROPE, 2, dtype=F32) / MLA_ROPE)
    ang = jnp.arange(length, dtype=F32)[:, None] * inv[None, :]
    cos, sin = jnp.cos(ang), jnp.sin(ang)
    ones = jnp.ones((length, MLA_NOPE), F32)
    tail = SLOT - MLA_NOPE - MLA_ROPE
    cos_t = jnp.concatenate([ones, cos, cos, jnp.ones((length, tail), F32)], axis=1)
    sin_t = jnp.concatenate([0.0 * ones, sin, sin, jnp.zeros((length, tail), F32)], axis=1)
    return cos_t, sin_t


def kernel(x, meta_tokens, attn_norm, w_in, q_norm, w_uq, kv_norm, w_ukv, lb_table, hg_norm, w_br_mla, w_br_hgrn, b_gate, w_out, ffn_norm, w_group, b_group, w_route, b_route, w1, w3, w2, final_norm):
    batch, seq, d = x.shape
    assert attn_norm.shape[0] == 1, "one layer"
    n = batch * seq
    x2 = x.reshape(n, d)
    wts = _prepare_weights(attn_norm[0], w_in[0], q_norm[0], w_uq[0], kv_norm[0], w_ukv[0], b_gate[0],
                           hg_norm[0], w_br_mla[0], w_br_hgrn[0], w_out[0], ffn_norm[0], w_group[0],
                           b_group[0], w_route[0], b_route[0], final_norm)
    cos_t, sin_t = _rope_tables(N_META + seq)

    meta = _in_proj(meta_tokens.astype(x.dtype), N_META, 1, cos_t[:N_META], sin_t[:N_META], wts)
    s0 = _hgrn_meta_state(meta[4], meta[5], lb_table)

    tm = min(ROW_TILE, seq)
    q, k, vt, hq, hf, hi, hg, gm, gh = _in_proj(x2, tm, seq // tm, cos_t[N_META:], sin_t[N_META:], wts)
    o_mla = _attention(q, k, vt, meta[1], meta[2], batch, seq, min(ATTN_TILE, seq))
    o_hg = _hgrn(hq, hf, hi, hg, lb_table, wts["hg_norm"], s0, batch, seq)
    h1, u, counts = _merge(o_mla, o_hg, gm, gh, x2, wts, tm)

    code = u[:, d + ROUTE_CODE].astype(I32)
    cnt = counts[0, :N_GROUPS].astype(I32)
    padded = (cnt + MOE_BLOCK - 1) // MOE_BLOCK * MOE_BLOCK
    p_end = jnp.cumsum(padded)
    p_start = p_end - padded
    n_blocks = -(-n // MOE_BLOCK) + N_GROUPS
    blk_row = jnp.arange(n_blocks, dtype=I32) * MOE_BLOCK
    blk_group = jnp.minimum(jnp.sum((p_end[None, :] <= blk_row[:, None]).astype(I32), axis=1), N_GROUPS - 1)
    n_used = p_end[-1:] // MOE_BLOCK

    tmove = min(MOVE_TILE, seq)
    xg = _dispatch(code, p_start, u, jnp.zeros((n_blocks * MOE_BLOCK, d + LANES), F32), tmove)
    y = _experts(blk_group, n_used, xg, w1[0], w3[0], w2[0])
    out = _combine(code, p_start, y, h1, wts["final_norm"], tmove)
    return out.reshape(batch, seq, d)
```

```python
import jax
import jax.numpy as jnp
from jax import lax
from jax.experimental import pallas as pl
from jax.experimental.pallas import tpu as pltpu

F32 = jnp.float32
BF16 = jnp.bfloat16
I32 = jnp.int32

N_META = 16
EPS = 1e-6
MLA_HEADS = 8
MLA_NOPE = 64
MLA_ROPE = 32
MLA_V = 64
Q_LORA = 384
KV_LORA = 256
ROPE_BASE = 10000.0
HG_HEADS = 4
HG_DK = 128
HG_DV = 128
N_GROUPS = 8
EXPERTS_PER_GROUP = 8
N_EXPERTS = N_GROUPS * EXPERTS_PER_GROUP
D_EXPERT = 256
MOE_BLOCK = 256

LANES = 128
SUBLANES = 8
VMEM_LIMIT_BYTES = 56 * 1024 * 1024

SLOT = LANES
ONE_LANE = MLA_V
ROW_TILE = 512
ATTN_TILE = 256
MOVE_TILE = 256
HG_CHUNK = 64
HG_SUB = 16
LOG2_E = 1.4426950408889634

SEG_CQ = (0, 384)
SEG_CKV = (384, 640)
SEG_KPE = (640, 768)
SEG_KPE_SW = (768, 896)
SEG_HQ = (896, 1408)
SEG_HF = (1408, 1920)
SEG_HI = (1920, 2432)
SEG_HG = (2432, 2944)
SEG_GM = (2944, 3968)
SEG_GH = (3968, 4992)


def _params(*sem):
    return pltpu.CompilerParams(dimension_semantics=sem, vmem_limit_bytes=VMEM_LIMIT_BYTES)


def _rms(x, g):
    return x * lax.rsqrt(jnp.mean(x * x, axis=-1, keepdims=True) + EPS) * g


def _dot(a, b):
    return jnp.dot(a, b, preferred_element_type=F32)


def _dot_nt(a, b):
    return lax.dot_general(a, b, (((1,), (1,)), ((), ())), preferred_element_type=F32)


def _dot_tn(a, b):
    return lax.dot_general(a, b, (((0,), (0,)), ((), ())), preferred_element_type=F32)


def _split2(x):
    hi = x.astype(BF16)
    lo = (x - hi.astype(F32)).astype(BF16)
    return hi, lo


def _split3(x):
    hi = x.astype(BF16)
    r = x - hi.astype(F32)
    mid = r.astype(BF16)
    lo = (r - mid.astype(F32)).astype(BF16)
    return hi, mid, lo


def _const_spec(shape):
    nd = len(shape)
    return pl.BlockSpec(shape, lambda *_: (0,) * nd, pipeline_mode=pl.Buffered(1))


def _rows_to_tiles(x):
    return x.reshape(x.shape[0], SUBLANES, LANES)


def _tiles_to_rows(x3):
    return x3.reshape(x3.shape[0], SUBLANES * LANES)


def _lower_bound(lbt_ref):
    t0 = lbt_ref[0:1, :]
    t1 = lbt_ref[1:2, :]
    mx = jnp.maximum(t0, t1)
    e0 = jnp.exp(t0 - mx)
    return e0 / (e0 + jnp.exp(t1 - mx))


def _in_proj_kernel(x_ref, g_ref, w_ref, qn_ref, kvn_ref, wq_ref, wqs_ref, wk_ref, wvt_ref,
                    cos_ref, sin_ref, bg_ref, lbt_ref,
                    q_out, k_out, vt_out, hq_out, lf_out, hi_out, hg_out, gm_out, gh_out):
    u = _rms(x_ref[...], g_ref[...]).astype(BF16)

    def seg(s):
        return _dot(u, w_ref[:, s[0]:s[1]])

    cos = cos_ref[...]
    sin = sin_ref[...]
    cos_t = jnp.tile(cos, (1, MLA_HEADS))
    sin_t = jnp.tile(sin, (1, MLA_HEADS))
    scale = (MLA_NOPE + MLA_ROPE) ** -0.5 * LOG2_E

    cqn = _rms(seg(SEG_CQ), qn_ref[...]).astype(BF16)
    q = _dot(cqn, wq_ref[...]) * cos_t + _dot(cqn, wqs_ref[...]) * sin_t
    q_out[...] = (q * scale).astype(BF16)

    ckvn = _rms(seg(SEG_CKV), kvn_ref[...]).astype(BF16)
    k_slot = seg(SEG_KPE) * cos + seg(SEG_KPE_SW) * sin
    k_out[...] = (_dot(ckvn, wk_ref[...]) + jnp.tile(k_slot, (1, MLA_HEADS))).astype(BF16)
    vt = _dot_nt(wvt_ref[...], ckvn)
    slot_row = lax.broadcasted_iota(I32, vt.shape, 0) & (SLOT - 1)
    vt = jnp.where(slot_row == ONE_LANE, 1.0, vt).astype(BF16)
    tv = vt_out.shape[2]
    for piece in range(vt_out.shape[0]):
        vt_out[piece] = vt[:, piece * tv:(piece + 1) * tv]

    lb = _lower_bound(lbt_ref)
    hq_out[...] = jax.nn.silu(seg(SEG_HQ)).astype(BF16)
    lf_out[...] = jnp.log(lb + (1.0 - lb) * jax.nn.sigmoid(seg(SEG_HF))) * LOG2_E
    hi_out[...] = seg(SEG_HI).astype(BF16)
    hg_out[...] = jax.nn.silu(seg(SEG_HG)).astype(BF16)
    gm_out[...] = jax.nn.sigmoid(seg(SEG_GM) + bg_ref[0:1, :]).astype(BF16)
    gh_out[...] = jax.nn.sigmoid(seg(SEG_GH) + bg_ref[1:2, :]).astype(BF16)


def _in_proj(x2, tm, tiles_per_seq, cos, sin, wts):
    n, d = x2.shape
    wide = MLA_HEADS * SLOT
    hw = HG_HEADS * HG_DK
    row = lambda width: pl.BlockSpec((tm, width), lambda i: (i, 0))
    tab = pl.BlockSpec((tm, SLOT), lambda i: (i % tiles_per_seq, 0))
    consts = (wts["attn_norm"], wts["w_in"], wts["q_norm"], wts["kv_norm"], wts["wq"], wts["wq_sw"],
              wts["wk"], wts["wv_t"])
    tail = (wts["b_gate"], wts["lb_table"])
    in_specs = ([row(d)] + [_const_spec(c.shape) for c in consts] + [tab, tab]
                + [_const_spec(c.shape) for c in tail])
    widths = (wide, wide, None, hw, hw, hw, hw, d, d)
    dtypes = (BF16, BF16, BF16, BF16, F32, BF16, BF16, BF16, BF16)
    tv = min(tm, ATTN_TILE)
    vt_spec = pl.BlockSpec((tm // tv, wide, tv), lambda i: (i, 0, 0))
    return pl.pallas_call(
        _in_proj_kernel,
        grid=(n // tm,),
        in_specs=in_specs,
        out_specs=[vt_spec if w is None else row(w) for w in widths],
        out_shape=[jax.ShapeDtypeStruct((n // tv, wide, tv) if w is None else (n, w), t)
                   for w, t in zip(widths, dtypes)],
        compiler_params=_params("parallel"),
        name="in_proj",
    )(x2, *consts, cos, sin, *tail)


def _attn_kernel(q_ref, k_ref, vt_ref, km_ref, vtm_ref, o_ref, m_ref, acc_ref):
    i = pl.program_id(1)
    tq = q_ref.shape[0]

    heads = [slice(h * SLOT, (h + 1) * SLOT) for h in range(MLA_HEADS)]

    def tile(k_of, vt_of, mask, first=False):
        s = [_dot_nt(k_of(sl), q_ref[:, sl]) for sl in heads]
        if mask is not None:
            s = [jnp.where(mask, x, -jnp.inf) for x in s]
        mx = [jnp.max(x, axis=0, keepdims=True) for x in s]
        if first:
            m_new = mx
        else:
            m_old = [m_ref[h] for h in range(MLA_HEADS)]
            m_new = [jnp.maximum(a, b) for a, b in zip(m_old, mx)]
        p = [jnp.exp2(x - m).astype(BF16) for x, m in zip(s, m_new)]
        for h, sl in enumerate(heads):
            m_ref[h] = m_new[h]
            pv = _dot(vt_of(sl), p[h])
            acc_ref[h] = pv if first else jnp.exp2(m_old[h] - m_new[h]) * acc_ref[h] + pv

    tile(lambda sl: km_ref[:, sl], lambda sl: vtm_ref[0, sl, :], None, first=True)

    def x_tile(j, mask):
        rows = pl.ds(pl.multiple_of(j * tq, tq), tq)
        tile(lambda sl: k_ref[rows, sl], lambda sl: vt_ref[j, sl, :], mask)

    def body(j, carry):
        x_tile(j, None)
        return carry

    lax.fori_loop(0, i, body, 0)
    x_tile(i, lax.broadcasted_iota(I32, (tq, tq), 0) <= lax.broadcasted_iota(I32, (tq, tq), 1))

    outs = []
    for h in range(MLA_HEADS):
        acc = acc_ref[h]
        outs.append(acc[:MLA_V, :] / acc[ONE_LANE:ONE_LANE + 1, :])
    o_ref[...] = jnp.concatenate(outs, axis=0).T.astype(BF16)


def _attention(q, k, vt, k_meta, vt_meta, batch, seq, tq):
    wide = MLA_HEADS * SLOT
    nq = seq // tq
    assert vt.shape == (batch * nq, wide, tq)
    return pl.pallas_call(
        _attn_kernel,
        grid=(batch, nq),
        in_specs=[
            pl.BlockSpec((tq, wide), lambda b, i: (b * nq + i, 0)),
            pl.BlockSpec((seq, wide), lambda b, i: (b, 0)),
            pl.BlockSpec((nq, wide, tq), lambda b, i: (b, 0, 0)),
            _const_spec(k_meta.shape),
            _const_spec(vt_meta.shape),
        ],
        out_specs=pl.BlockSpec((tq, MLA_HEADS * MLA_V), lambda b, i: (b * nq + i, 0)),
        out_shape=jax.ShapeDtypeStruct((batch * seq, MLA_HEADS * MLA_V), BF16),
        scratch_shapes=[pltpu.VMEM((MLA_HEADS, 1, tq), F32), pltpu.VMEM((MLA_HEADS, SLOT, tq), F32)],
        compiler_params=_params("parallel", "arbitrary"),
        name="mla_attention",
    )(q, k, vt, k_meta, vt_meta)


def _cumsum_rows(tril, lf):
    parts = _split3(lf)
    return _dot(tril, parts[0]) + _dot(tril, parts[1]) + _dot(tril, parts[2])


def _tril(n):
    return (lax.broadcasted_iota(I32, (n, n), 0) >= lax.broadcasted_iota(I32, (n, n), 1)).astype(BF16)


def _hgrn_meta_kernel(lf_ref, hi_ref, s_out):
    tril = _tril(lf_ref.shape[0])
    for h in range(HG_HEADS):
        sl = slice(h * HG_DK, (h + 1) * HG_DK)
        lf = lf_ref[:, sl]
        b = _cumsum_rows(tril, lf)
        kdec = (1.0 - jnp.exp2(lf)) * jnp.exp2(b[-1:, :] - b)
        s_out[h] = _dot_tn(hi_ref[:, sl], kdec.astype(BF16))


def _hgrn_meta_state(lf, hi):
    return pl.pallas_call(
        _hgrn_meta_kernel,
        out_shape=jax.ShapeDtypeStruct((HG_HEADS, HG_DV, HG_DK), F32),
        compiler_params=pltpu.CompilerParams(vmem_limit_bytes=VMEM_LIMIT_BYTES),
        name="hgrn_meta_state",
    )(lf, hi)


def _hgrn_kernel(hq_ref, lf_ref, hi_ref, hg_ref, hgn_ref, s0_ref, o_ref,
                 st_ref, pb_ref, pk_ref, pv_ref):
    c_rows = HG_CHUNK
    n_chunks = hq_ref.shape[0] // c_rows
    st_ref[...] = s0_ref[...]
    pad = jnp.zeros((HG_HEADS, HG_SUB, HG_DK), F32)
    pb_ref[:, 0:HG_SUB, :] = pad
    pk_ref[:, 0:HG_SUB, :] = pad
    pv_ref[:, 0:HG_SUB, :] = pad
    hgn = hgn_ref[...]
    tril = _tril(c_rows)
    ones = jnp.ones((HG_DK, LANES), BF16)
    n_sub = c_rows // HG_SUB
    far = (lax.broadcasted_iota(I32, (c_rows, c_rows), 0) - lax.broadcasted_iota(I32, (c_rows, c_rows), 1)) >= HG_SUB
    heads = range(HG_HEADS)
    cols = [slice(h * HG_DK, (h + 1) * HG_DK) for h in heads]
    stash = slice(HG_SUB, HG_SUB + c_rows)

    def chunk(c, carry):
        rows = pl.ds(pl.multiple_of(c * c_rows, c_rows), c_rows)
        lf = [lf_ref[rows, sl] for sl in cols]
        kk = [1.0 - jnp.exp2(x) for x in lf]
        b = [_cumsum_rows(tril, x) for x in lf]
        q = [hq_ref[rows, sl].astype(F32) for sl in cols]
        v_bf = [hi_ref[rows, sl] for sl in cols]
        st = [st_ref[h] for h in heads]
        for h in heads:
            pb_ref[h, stash, :] = b[h]
            pk_ref[h, stash, :] = kk[h]
            pv_ref[h, stash, :] = v_bf[h].astype(F32)

        y = []
        for h in heads:
            xs = [(q[h] * kk[h]).astype(BF16)]
            for d in range(1, HG_SUB):
                shifted = slice(HG_SUB - d, HG_SUB - d + c_rows)
                xs.append((q[h] * pk_ref[h, shifted, :] * jnp.exp2(b[h] - pb_ref[h, shifted, :])).astype(BF16))
            y.append(_dot(jnp.concatenate(xs, axis=0), ones))

        o = [_dot_nt((q[h] * jnp.exp2(b[h])).astype(BF16), st[h].astype(BF16)) for h in heads]

        a_off = []
        for h in heads:
            a = jnp.zeros((c_rows, c_rows), F32)
            for j in range(n_sub - 1):
                k0, k1 = HG_SUB * j, HG_SUB * (j + 1)
                rj = b[h][k1 - 1:k1, :]
                qp = (q[h][k1:, :] * jnp.exp2(b[h][k1:, :] - rj)).astype(BF16)
                kp = (kk[h][k0:k1, :] * jnp.exp2(rj - b[h][k0:k1, :])).astype(BF16)
                zero_rows = lambda r: [jnp.zeros((r, HG_DK), BF16)] if r else []
                qp = jnp.concatenate(zero_rows(k1) + [qp], axis=0)
                kp = jnp.concatenate(zero_rows(k0) + [kp] + zero_rows(c_rows - k1), axis=0)
                a = a + _dot_nt(qp, kp)
            a_off.append(jnp.where(far, a, 0.0).astype(BF16))

        for h in heads:
            o[h] = o[h] + _dot(a_off[h], v_bf[h])
            for d in range(HG_SUB):
                shifted = slice(HG_SUB - d, HG_SUB - d + c_rows)
                o[h] = o[h] + y[h][d * c_rows:(d + 1) * c_rows, :] * pv_ref[h, shifted, :]

        for h, sl in enumerate(cols):
            on = o[h] * lax.rsqrt(jnp.mean(o[h] * o[h], axis=-1, keepdims=True) + EPS)
            on = on * hgn[:, sl] * hg_ref[rows, sl].astype(F32)
            o_ref[rows, sl] = on.astype(BF16)

        for h in heads:
            b_end = b[h][c_rows - 1:c_rows, :]
            kdec = kk[h] * jnp.exp2(b_end - b[h])
            st_ref[h] = st[h] * jnp.exp2(b_end) + _dot_tn(v_bf[h], kdec.astype(BF16))
        return carry

    lax.fori_loop(0, n_chunks, chunk, 0)


def _hgrn(hq, lf, hi, hg, hg_norm, s0, batch, seq):
    hw = HG_HEADS * HG_DK
    seq_spec = pl.BlockSpec((seq, hw), lambda b: (b, 0))
    pad_rows = HG_SUB + HG_CHUNK
    return pl.pallas_call(
        _hgrn_kernel,
        grid=(batch,),
        in_specs=[seq_spec, seq_spec, seq_spec, seq_spec, _const_spec(hg_norm.shape), _const_spec(s0.shape)],
        out_specs=seq_spec,
        out_shape=jax.ShapeDtypeStruct((batch * seq, hw), BF16),
        scratch_shapes=[pltpu.VMEM((HG_HEADS, HG_DV, HG_DK), F32)]
        + [pltpu.VMEM((HG_HEADS, pad_rows, HG_DK), F32)] * 3,
        compiler_params=_params("parallel"),
        name="hgrn2",
    )(hq, lf, hi, hg, hg_norm, s0)


ROUTE_CODE, ROUTE_E1, ROUTE_E2, ROUTE_G1, ROUTE_G2 = range(5)
RANK_BITS = 16
ROW_SUBLANES = 2 * SUBLANES


def _slot(code, pstart_ref):
    return pstart_ref[code >> RANK_BITS] + (code & ((1 << RANK_BITS) - 1))


def _merge_kernel(om_ref, oh_ref, gm_ref, gh_ref, x_ref, wbm_ref, wbh_ref, wo_ref, fn_ref,
                  wr_hi_ref, wr_lo_ref, br_ref, h_out, u_out, route_out, cnt_out, carry_ref):
    i = pl.program_id(0)
    tm, d = x_ref.shape

    @pl.when(i == 0)
    def _():
        carry_ref[...] = jnp.zeros_like(carry_ref)

    a = _dot(om_ref[...], wbm_ref[...])
    g = _dot(oh_ref[...], wbh_ref[...])
    merged = gm_ref[...].astype(F32) * a + gh_ref[...].astype(F32) * g
    h1 = x_ref[...] + _dot(merged.astype(BF16), wo_ref[...])
    h_out[...] = h1
    u = _rms(h1, fn_ref[...])
    u_out[:, 0:SUBLANES, :] = _rows_to_tiles(u)

    u_hi, u_lo = _split2(u)
    logits = (_dot(u_hi, wr_hi_ref[...]) + _dot(u_hi, wr_lo_ref[...]) + _dot(u_lo, wr_hi_ref[...])
              + br_ref[...])
    lane = lax.broadcasted_iota(I32, (tm, LANES), 1)
    lane_f = lane.astype(F32)
    big = float(2 * LANES)

    def first_max(vals):
        mx = jnp.max(vals, axis=-1, keepdims=True)
        idx = jnp.min(jnp.where(vals == mx, lane_f, big), axis=-1, keepdims=True)
        return mx, idx

    gl = jnp.where(lane < N_GROUPS, logits, -jnp.inf)
    g_max, g_sel = first_max(gl)
    p_sel = 1.0 / jnp.sum(jnp.exp(gl - g_max), axis=-1, keepdims=True)
    lo = N_GROUPS + g_sel * EXPERTS_PER_GROUP
    el = jnp.where((lane_f >= lo) & (lane_f < lo + EXPERTS_PER_GROUP), logits, -jnp.inf)
    v1, i1 = first_max(el)
    el2 = jnp.where(lane_f == i1, -jnp.inf, el)
    v2, i2 = first_max(el2)
    t = jnp.exp(v2 - v1)
    g1 = p_sel * (1.0 / (1.0 + t))
    g2 = p_sel * (t / (1.0 + t))

    hit = lane_f == g_sel
    onehot = jnp.where(hit, 1.0, 0.0)
    strict = (lax.broadcasted_iota(I32, (tm, tm), 0) > lax.broadcasted_iota(I32, (tm, tm), 1)).astype(BF16)
    before = _dot(strict, onehot.astype(BF16)) + carry_ref[0:1, :]
    rank = jnp.sum(jnp.where(hit, before, 0.0), axis=-1, keepdims=True)
    total = carry_ref[0:1, :] + jnp.sum(onehot, axis=0, keepdims=True)
    carry_ref[...] = jnp.broadcast_to(total, carry_ref.shape)
    cnt_out[...] = jnp.broadcast_to(total, cnt_out.shape)

    route = jnp.zeros((tm, LANES), F32)
    code = g_sel * float(1 << RANK_BITS) + rank
    for pos, val in ((ROUTE_CODE, code), (ROUTE_E1, i1 - N_GROUPS), (ROUTE_E2, i2 - N_GROUPS),
                     (ROUTE_G1, g1), (ROUTE_G2, g2)):
        route = jnp.where(lane == pos, val, route)
    route_out[...] = route
    record = jnp.concatenate([route, jnp.zeros((tm, d - LANES), F32)], axis=1)
    u_out[:, SUBLANES:ROW_SUBLANES, :] = _rows_to_tiles(record)


def _merge(o_mla, o_hg, gm, gh, x2, wts, tm):
    n, d = x2.shape
    row = lambda width: pl.BlockSpec((tm, width), lambda i: (i, 0))
    consts = (wts["w_br_mla"], wts["w_br_hgrn"], wts["w_out"], wts["ffn_norm"],
              wts["w_router_hi"], wts["w_router_lo"], wts["b_router"])
    return pl.pallas_call(
        _merge_kernel,
        grid=(n // tm,),
        in_specs=[row(o_mla.shape[1]), row(o_hg.shape[1]), row(d), row(d), row(d)]
        + [_const_spec(c.shape) for c in consts],
        out_specs=[row(d), pl.BlockSpec((tm, ROW_SUBLANES, LANES), lambda i: (i, 0, 0)), row(LANES),
                   pl.BlockSpec((8, LANES), lambda i: (0, 0))],
        out_shape=[jax.ShapeDtypeStruct((n, d), F32), jax.ShapeDtypeStruct((n, ROW_SUBLANES, LANES), F32),
                   jax.ShapeDtypeStruct((n, LANES), F32), jax.ShapeDtypeStruct((8, LANES), F32)],
        scratch_shapes=[pltpu.VMEM((8, LANES), F32)],
        compiler_params=_params("arbitrary"),
        name="merge_route",
    )(o_mla, o_hg, gm, gh, x2, *consts)


ROW_UNROLL = 8


def _row_copy(src_ref, src_row, dst_ref, dst_row, sem):
    return pltpu.make_async_copy(src_ref.at[pl.ds(src_row, 1)], dst_ref.at[pl.ds(dst_row, 1)], sem)


def _dispatch_kernel(code_ref, ps_ref, pad_lo_ref, pad_hi_ref, nu_ref, u_ref, xg_out, zero_ref, sem, pad_sem):
    i = pl.program_id(0)
    tm = u_ref.shape[0]
    base = i * tm

    @pl.when(i == 0)
    def _():
        zero_ref[...] = jnp.zeros_like(zero_ref)
        for g in range(N_GROUPS):
            def fill(r, carry):
                _row_copy(zero_ref, 0, xg_out, r, pad_sem).start()
                return carry

            def drain(r, carry):
                _row_copy(zero_ref, 0, xg_out, 0, pad_sem).wait()
                return carry

            lax.fori_loop(pad_lo_ref[g], pad_hi_ref[g], fill, 0)
            lax.fori_loop(pad_lo_ref[g], pad_hi_ref[g], drain, 0)

        def block_copy(b):
            rows = pl.ds(pl.multiple_of(b * MOE_BLOCK, MOE_BLOCK), MOE_BLOCK)
            return pltpu.make_async_copy(zero_ref, xg_out.at[rows], pad_sem)

        def fill_block(b, carry):
            block_copy(b).start()
            return carry

        def drain_block(b, carry):
            block_copy(b).wait()
            return carry

        n_blocks = xg_out.shape[0] // MOE_BLOCK
        lax.fori_loop(nu_ref[0], n_blocks, fill_block, 0)
        lax.fori_loop(nu_ref[0], n_blocks, drain_block, 0)

    def start(g, carry):
        r0 = pl.multiple_of(g * ROW_UNROLL, ROW_UNROLL)
        for k in range(ROW_UNROLL):
            _row_copy(u_ref, r0 + k, xg_out, _slot(code_ref[base + r0 + k], ps_ref), sem).start()
        return carry

    def wait(g, carry):
        for _ in range(ROW_UNROLL):
            _row_copy(u_ref, 0, xg_out, 0, sem).wait()
        return carry

    lax.fori_loop(0, tm // ROW_UNROLL, start, 0)
    lax.fori_loop(0, tm // ROW_UNROLL, wait, 0)


def _dispatch(code, p_start, pad_lo, pad_hi, n_used, u3, n_slots, tm):
    n = u3.shape[0]
    tile = u3.shape[1:]
    return pl.pallas_call(
        _dispatch_kernel,
        grid_spec=pltpu.PrefetchScalarGridSpec(
            num_scalar_prefetch=5,
            grid=(n // tm,),
            in_specs=[pl.BlockSpec((tm,) + tile, lambda i, *_: (i, 0, 0))],
            out_specs=pl.BlockSpec(memory_space=pl.ANY),
            scratch_shapes=[pltpu.VMEM((MOE_BLOCK,) + tile, F32), pltpu.SemaphoreType.DMA,
                            pltpu.SemaphoreType.DMA],
        ),
        out_shape=jax.ShapeDtypeStruct((n_slots,) + tile, F32),
        compiler_params=_params("arbitrary"),
        name="moe_dispatch",
    )(code, p_start, pad_lo, pad_hi, n_used, u3)


def _expert_kernel(bg_ref, nu_ref, xg_ref, w1_ref, w3_ref, w2_ref, y_ref, w13_s, w2_s, acc_ref):
    i = pl.program_id(0)
    used = i < nu_ref[0]

    @pl.when(used)
    def _():
        group = bg_ref[i]

        @pl.when((i == 0) | (group != bg_ref[jnp.maximum(i - 1, 0)]))
        def _():
            for e in range(EXPERTS_PER_GROUP):
                w13_s[e, :, 0:D_EXPERT] = w1_ref[e].astype(BF16)
                w13_s[e, :, D_EXPERT:2 * D_EXPERT] = w3_ref[e].astype(BF16)
                w2_s[e] = w2_ref[e].astype(BF16)

        x = _tiles_to_rows(xg_ref[:, 0:SUBLANES, :]).astype(BF16)
        route = _tiles_to_rows(xg_ref[:, SUBLANES:ROW_SUBLANES, :])[:, 0:LANES]
        e1 = route[:, ROUTE_E1:ROUTE_E1 + 1]
        e2 = route[:, ROUTE_E2:ROUTE_E2 + 1]
        g1 = route[:, ROUTE_G1:ROUTE_G1 + 1]
        g2 = route[:, ROUTE_G2:ROUTE_G2 + 1]
        first = (group * EXPERTS_PER_GROUP).astype(F32)
        hidden = []
        for e in range(EXPERTS_PER_GROUP):
            hcat = _dot(x, w13_s[e])
            hidden.append((jax.nn.silu(hcat[:, 0:D_EXPERT]) * hcat[:, D_EXPERT:2 * D_EXPERT]).astype(BF16))
        for e in range(EXPERTS_PER_GROUP):
            is1 = e1 == first + e
            is2 = e2 == first + e
            gate = jnp.where(is1, g1, 0.0) + jnp.where(is2, g2, 0.0)
            part = jnp.where(is1 | is2, gate * _dot(hidden[e], w2_s[e]), 0.0)
            if e == 0:
                acc_ref[...] = part
            else:
                acc_ref[...] += part
        y_ref[...] = _rows_to_tiles(acc_ref[...])

    @pl.when(jnp.logical_not(used))
    def _():
        y_ref[...] = jnp.zeros_like(y_ref)


def _experts(blk_group, n_used, xg, w1, w3, w2):
    r = xg.shape[0]
    d = SUBLANES * LANES
    n_blocks = r // MOE_BLOCK
    last_used = lambda i, nu: jnp.minimum(i, nu[0] - 1)
    group_weights = lambda shape: pl.BlockSpec((EXPERTS_PER_GROUP,) + shape, lambda i, bg, nu: (bg[i], 0, 0),
                                               pipeline_mode=pl.Buffered(1))
    return pl.pallas_call(
        _expert_kernel,
        grid_spec=pltpu.PrefetchScalarGridSpec(
            num_scalar_prefetch=2,
            grid=(n_blocks,),
            in_specs=[
                pl.BlockSpec((MOE_BLOCK, ROW_SUBLANES, LANES), lambda i, bg, nu: (last_used(i, nu), 0, 0)),
                group_weights((d, D_EXPERT)),
                group_weights((d, D_EXPERT)),
                group_weights((D_EXPERT, d)),
            ],
            out_specs=pl.BlockSpec((MOE_BLOCK, SUBLANES, LANES), lambda i, bg, nu: (i, 0, 0)),
            scratch_shapes=[pltpu.VMEM((EXPERTS_PER_GROUP, d, 2 * D_EXPERT), BF16),
                            pltpu.VMEM((EXPERTS_PER_GROUP, D_EXPERT, d), BF16),
                            pltpu.VMEM((MOE_BLOCK, d), F32)],
        ),
        out_shape=jax.ShapeDtypeStruct((r, SUBLANES, LANES), F32),
        compiler_params=_params("arbitrary"),
        name="moe_experts",
    )(blk_group, n_used, xg, w1, w3, w2)


def _combine_kernel(code_ref, ps_ref, y_ref, h_ref, fn_ref, o_ref, buf_ref, sem):
    tm = h_ref.shape[0]
    base = pl.program_id(0) * tm

    def start(g, carry):
        r0 = pl.multiple_of(g * ROW_UNROLL, ROW_UNROLL)
        for k in range(ROW_UNROLL):
            _row_copy(y_ref, _slot(code_ref[base + r0 + k], ps_ref), buf_ref, r0 + k, sem).start()
        return carry

    def wait(g, carry):
        for _ in range(ROW_UNROLL):
            _row_copy(y_ref, 0, buf_ref, 0, sem).wait()
        return carry

    lax.fori_loop(0, tm // ROW_UNROLL, start, 0)
    lax.fori_loop(0, tm // ROW_UNROLL, wait, 0)
    o_ref[...] = _rms(h_ref[...] + _tiles_to_rows(buf_ref[...]), fn_ref[...])


def _combine(code, p_start, y3, h1, final_norm, tm):
    n, d = h1.shape
    return pl.pallas_call(
        _combine_kernel,
        grid_spec=pltpu.PrefetchScalarGridSpec(
            num_scalar_prefetch=2,
            grid=(n // tm,),
            in_specs=[
                pl.BlockSpec(memory_space=pl.ANY),
                pl.BlockSpec((tm, d), lambda i, *_: (i, 0)),
                pl.BlockSpec((1, d), lambda i, *_: (0, 0)),
            ],
            out_specs=pl.BlockSpec((tm, d), lambda i, *_: (i, 0)),
            scratch_shapes=[pltpu.VMEM((tm, SUBLANES, LANES), F32), pltpu.SemaphoreType.DMA],
        ),
        out_shape=jax.ShapeDtypeStruct((n, d), F32),
        compiler_params=_params("arbitrary"),
        name="moe_combine",
    )(code, p_start, y3, h1, final_norm)


def _prepare_weights(attn_norm, w_in, q_norm, w_uq, kv_norm, w_ukv, b_gate, hg_norm, w_br_mla, w_br_hgrn,
                     w_out, ffn_norm, w_group, b_group, w_route, b_route, final_norm, lb_table):
    d = w_in.shape[0]
    half = MLA_ROPE // 2
    pe0, pe1 = Q_LORA + KV_LORA, Q_LORA + KV_LORA + MLA_ROPE
    w_pe = w_in[:, pe0:pe1]
    zeros = lambda r, c: jnp.zeros((r, c), F32)
    kpe_slot = jnp.concatenate([zeros(d, MLA_NOPE), w_pe, zeros(d, SLOT - MLA_NOPE - MLA_ROPE)], axis=1)
    kpe_swap = jnp.concatenate([zeros(d, MLA_NOPE), -w_pe[:, half:], w_pe[:, :half],
                                zeros(d, SLOT - MLA_NOPE - MLA_ROPE)], axis=1)
    w_wide = jnp.concatenate([w_in[:, :pe0], kpe_slot, kpe_swap, w_in[:, pe1:]], axis=1).astype(BF16)

    wq3 = w_uq.reshape(Q_LORA, MLA_HEADS, MLA_NOPE + MLA_ROPE)
    q_nope, q_pe = wq3[..., :MLA_NOPE], wq3[..., MLA_NOPE:]
    zq = jnp.zeros((Q_LORA, MLA_HEADS, SLOT - MLA_NOPE - MLA_ROPE), F32)
    wq = jnp.concatenate([q_nope, q_pe, zq], axis=-1).reshape(Q_LORA, MLA_HEADS * SLOT).astype(BF16)
    wq_sw = jnp.concatenate([jnp.zeros_like(q_nope), -q_pe[..., half:], q_pe[..., :half], zq],
                            axis=-1).reshape(Q_LORA, MLA_HEADS * SLOT).astype(BF16)

    wkv3 = w_ukv.reshape(KV_LORA, MLA_HEADS, MLA_NOPE + MLA_V)
    zk = jnp.zeros((KV_LORA, MLA_HEADS, SLOT - MLA_NOPE), F32)
    wk = jnp.concatenate([wkv3[..., :MLA_NOPE], zk], axis=-1).reshape(KV_LORA, MLA_HEADS * SLOT).astype(BF16)
    zv = jnp.zeros((KV_LORA, MLA_HEADS, SLOT - MLA_V), F32)
    wv_t = jnp.concatenate([wkv3[..., MLA_NOPE:], zv], axis=-1).reshape(KV_LORA, MLA_HEADS * SLOT).T.astype(BF16)

    w_router = jnp.concatenate([w_group, w_route, zeros(d, LANES - N_GROUPS - N_EXPERTS)], axis=1)
    wr_hi = w_router.astype(BF16)
    wr_lo = (w_router - wr_hi.astype(F32)).astype(BF16)
    b_router = jnp.concatenate([b_group, b_route, jnp.zeros((LANES - N_GROUPS - N_EXPERTS,), F32)])[None, :]
    return {
        "attn_norm": attn_norm[None, :], "w_in": w_wide, "q_norm": q_norm[None, :], "kv_norm": kv_norm[None, :],
        "wq": wq, "wq_sw": wq_sw, "wk": wk, "wv_t": wv_t, "b_gate": b_gate, "lb_table": lb_table,
        "hg_norm": hg_norm[None, :], "w_br_mla": w_br_mla.astype(BF16), "w_br_hgrn": w_br_hgrn.astype(BF16),
        "w_out": w_out.astype(BF16), "ffn_norm": ffn_norm[None, :], "w_router_hi": wr_hi, "w_router_lo": wr_lo,
        "b_router": b_router, "final_norm": final_norm[None, :],
    }


def _rope_tables(length):
    inv = ROPE_BASE ** (-jnp.arange(0, MLA_ROPE, 2, dtype=F32) / MLA_ROPE)
    ang = jnp.arange(length, dtype=F32)[:, None] * inv[None, :]
    cos, sin = jnp.cos(ang), jnp.sin(ang)
    ones = jnp.ones((length, MLA_NOPE), F32)
    tail = SLOT - MLA_NOPE - MLA_ROPE
    cos_t = jnp.concatenate([ones, cos, cos, jnp.ones((length, tail), F32)], axis=1)
    sin_t = jnp.concatenate([0.0 * ones, sin, sin, jnp.zeros((length, tail), F32)], axis=1)
    return cos_t, sin_t


def kernel(x, meta_tokens, attn_norm, w_in, q_norm, w_uq, kv_norm, w_ukv, lb_table, hg_norm, w_br_mla, w_br_hgrn, b_gate, w_out, ffn_norm, w_group, b_group, w_route, b_route, w1, w3, w2, final_norm):
    batch, seq, d = x.shape
    assert attn_norm.shape[0] == 1, "one layer"
    assert d == SUBLANES * LANES, "a token row is moved as one (8, 128) tile"
    n = batch * seq
    x2 = x.reshape(n, d)
    wts = _prepare_weights(attn_norm[0], w_in[0], q_norm[0], w_uq[0], kv_norm[0], w_ukv[0], b_gate[0],
                           hg_norm[0], w_br_mla[0], w_br_hgrn[0], w_out[0], ffn_norm[0], w_group[0],
                           b_group[0], w_route[0], b_route[0], final_norm, lb_table)
    cos_t, sin_t = _rope_tables(N_META + seq)

    meta = _in_proj(meta_tokens.astype(x.dtype), N_META, 1, cos_t[:N_META], sin_t[:N_META], wts)
    s0 = _hgrn_meta_state(meta[4], meta[5])

    tm = min(ROW_TILE, seq)
    q, k, vt, hq, lf, hi, hg, gm, gh = _in_proj(x2, tm, seq // tm, cos_t[N_META:], sin_t[N_META:], wts)
    o_mla = _attention(q, k, vt, meta[1], meta[2], batch, seq, min(ATTN_TILE, seq))
    o_hg = _hgrn(hq, lf, hi, hg, wts["hg_norm"], s0, batch, seq)
    h1, u3, route, counts = _merge(o_mla, o_hg, gm, gh, x2, wts, tm)

    code = route[:, ROUTE_CODE].astype(I32)
    cnt = counts[0, :N_GROUPS].astype(I32)
    padded = (cnt + MOE_BLOCK - 1) // MOE_BLOCK * MOE_BLOCK
    p_end = jnp.cumsum(padded)
    p_start = p_end - padded
    n_blocks = -(-n // MOE_BLOCK) + N_GROUPS
    blk_row = jnp.arange(n_blocks, dtype=I32) * MOE_BLOCK
    blk_group = jnp.minimum(jnp.sum((p_end[None, :] <= blk_row[:, None]).astype(I32), axis=1), N_GROUPS - 1)
    n_used = p_end[-1:] // MOE_BLOCK

    tmove = min(MOVE_TILE, seq)
    xg = _dispatch(code, p_start, p_start + cnt, p_end, n_used, u3, n_blocks * MOE_BLOCK, tmove)
    y3 = _experts(blk_group, n_used, xg, w1[0], w3[0], w2[0])
    out = _combine(code, p_start, y3, h1, wts["final_norm"], tmove)
    return out.reshape(batch, seq, d)
```

```python
import jax
import jax.numpy as jnp
from jax import lax
from jax.experimental import pallas as pl
from jax.experimental.pallas import tpu as pltpu

F32 = jnp.float32
BF16 = jnp.bfloat16
I32 = jnp.int32

N_META = 16
EPS = 1e-6
MLA_HEADS = 8
MLA_NOPE = 64
MLA_ROPE = 32
MLA_V = 64
Q_LORA = 384
KV_LORA = 256
ROPE_BASE = 10000.0
HG_HEADS = 4
HG_DK = 128
HG_DV = 128
N_GROUPS = 8
EXPERTS_PER_GROUP = 8
N_EXPERTS = N_GROUPS * EXPERTS_PER_GROUP
D_EXPERT = 256
MOE_BLOCK = 256

LANES = 128
SUBLANES = 8
VMEM_LIMIT_BYTES = 56 * 1024 * 1024

SLOT = LANES
ONE_LANE = MLA_V
ROW_TILE = 512
ATTN_TILE = 256
MOVE_TILE = 512
HG_CHUNK = 64
HG_SUB = 8
BAND_GROUP = 4
HEAD_SET = HG_HEADS
LOG2_E = 1.4426950408889634

SEG_CQ = (0, 384)
SEG_CKV = (384, 640)
SEG_KPE = (640, 768)
SEG_KPE_SW = (768, 896)
SEG_HQ = (896, 1408)
SEG_HF = (1408, 1920)
SEG_HI = (1920, 2432)
SEG_HG = (2432, 2944)
SEG_GM = (2944, 3968)
SEG_GH = (3968, 4992)


def _params(*sem):
    return pltpu.CompilerParams(dimension_semantics=sem, vmem_limit_bytes=VMEM_LIMIT_BYTES)


def _rms(x, g):
    return x * lax.rsqrt(jnp.mean(x * x, axis=-1, keepdims=True) + EPS) * g


def _dot(a, b):
    return jnp.dot(a, b, preferred_element_type=F32)


def _dot_nt(a, b):
    return lax.dot_general(a, b, (((1,), (1,)), ((), ())), preferred_element_type=F32)


def _dot_tn(a, b):
    return lax.dot_general(a, b, (((0,), (0,)), ((), ())), preferred_element_type=F32)


def _split2(x):
    hi = x.astype(BF16)
    lo = (x - hi.astype(F32)).astype(BF16)
    return hi, lo


def _split3(x):
    hi = x.astype(BF16)
    r = x - hi.astype(F32)
    mid = r.astype(BF16)
    lo = (r - mid.astype(F32)).astype(BF16)
    return hi, mid, lo


def _const_spec(shape):
    nd = len(shape)
    return pl.BlockSpec(shape, lambda *_: (0,) * nd, pipeline_mode=pl.Buffered(1))


def _rows_to_tiles(x):
    return x.reshape(x.shape[0], SUBLANES, LANES)


def _tiles_to_rows(x3):
    return x3.reshape(x3.shape[0], SUBLANES * LANES)


def _lower_bound(lbt_ref):
    t0 = lbt_ref[0:1, :]
    t1 = lbt_ref[1:2, :]
    mx = jnp.maximum(t0, t1)
    e0 = jnp.exp(t0 - mx)
    return e0 / (e0 + jnp.exp(t1 - mx))


def _in_proj_kernel(x_ref, g_ref, w_ref, qn_ref, kvn_ref, wq_ref, wqs_ref, wk_ref, wvt_ref,
                    cos_ref, sin_ref, bg_ref, lbt_ref,
                    q_out, k_out, vt_out, hq_out, lf_out, hi_out, hg_out, gm_out, gh_out):
    u = _rms(x_ref[...], g_ref[...]).astype(BF16)

    def seg(s):
        return _dot(u, w_ref[:, s[0]:s[1]])

    cos = cos_ref[...]
    sin = sin_ref[...]
    cos_t = jnp.tile(cos, (1, MLA_HEADS))
    sin_t = jnp.tile(sin, (1, MLA_HEADS))
    scale = (MLA_NOPE + MLA_ROPE) ** -0.5 * LOG2_E

    cqn = _rms(seg(SEG_CQ), qn_ref[...]).astype(BF16)
    q = _dot(cqn, wq_ref[...]) * cos_t + _dot(cqn, wqs_ref[...]) * sin_t
    q_out[...] = (q * scale).astype(BF16)

    ckvn = _rms(seg(SEG_CKV), kvn_ref[...]).astype(BF16)
    k_slot = seg(SEG_KPE) * cos + seg(SEG_KPE_SW) * sin
    k_out[...] = (_dot(ckvn, wk_ref[...]) + jnp.tile(k_slot, (1, MLA_HEADS))).astype(BF16)
    vt = _dot_nt(wvt_ref[...], ckvn)
    slot_row = lax.broadcasted_iota(I32, vt.shape, 0) & (SLOT - 1)
    vt = jnp.where(slot_row == ONE_LANE, 1.0, vt).astype(BF16)
    tv = vt_out.shape[2]
    for piece in range(vt_out.shape[0]):
        vt_out[piece] = vt[:, piece * tv:(piece + 1) * tv]

    lb = _lower_bound(lbt_ref)
    hq_out[...] = jax.nn.silu(seg(SEG_HQ)).astype(BF16)
    lf_out[...] = jnp.log(lb + (1.0 - lb) * jax.nn.sigmoid(seg(SEG_HF))) * LOG2_E
    hi_out[...] = seg(SEG_HI).astype(BF16)
    hg_out[...] = jax.nn.silu(seg(SEG_HG)).astype(BF16)
    gm_out[...] = jax.nn.sigmoid(seg(SEG_GM) + bg_ref[0:1, :]).astype(BF16)
    gh_out[...] = jax.nn.sigmoid(seg(SEG_GH) + bg_ref[1:2, :]).astype(BF16)


def _in_proj(x2, tm, tiles_per_seq, cos, sin, wts):
    n, d = x2.shape
    wide = MLA_HEADS * SLOT
    hw = HG_HEADS * HG_DK
    row = lambda width: pl.BlockSpec((tm, width), lambda i: (i, 0))
    tab = pl.BlockSpec((tm, SLOT), lambda i: (i % tiles_per_seq, 0))
    consts = (wts["attn_norm"], wts["w_in"], wts["q_norm"], wts["kv_norm"], wts["wq"], wts["wq_sw"],
              wts["wk"], wts["wv_t"])
    tail = (wts["b_gate"], wts["lb_table"])
    in_specs = ([row(d)] + [_const_spec(c.shape) for c in consts] + [tab, tab]
                + [_const_spec(c.shape) for c in tail])
    widths = (wide, wide, None, hw, hw, hw, hw, d, d)
    dtypes = (BF16, BF16, BF16, BF16, F32, BF16, BF16, BF16, BF16)
    tv = min(tm, ATTN_TILE)
    vt_spec = pl.BlockSpec((tm // tv, wide, tv), lambda i: (i, 0, 0))
    return pl.pallas_call(
        _in_proj_kernel,
        grid=(n // tm,),
        in_specs=in_specs,
        out_specs=[vt_spec if w is None else row(w) for w in widths],
        out_shape=[jax.ShapeDtypeStruct((n // tv, wide, tv) if w is None else (n, w), t)
                   for w, t in zip(widths, dtypes)],
        compiler_params=_params("parallel"),
        name="in_proj",
    )(x2, *consts, cos, sin, *tail)


def _attn_kernel(q_ref, k_ref, vt_ref, km_ref, vtm_ref, o_ref, m_ref, acc_ref):
    i = pl.program_id(1)
    tq = q_ref.shape[0]

    heads = [slice(h * SLOT, (h + 1) * SLOT) for h in range(MLA_HEADS)]

    def tile(k_of, vt_of, mask, first=False):
        s = [_dot_nt(k_of(sl), q_ref[:, sl]) for sl in heads]
        if mask is not None:
            s = [jnp.where(mask, x, -jnp.inf) for x in s]
        mx = [jnp.max(x, axis=0, keepdims=True) for x in s]
        if first:
            m_new = mx
        else:
            m_old = [m_ref[h] for h in range(MLA_HEADS)]
            m_new = [jnp.maximum(a, b) for a, b in zip(m_old, mx)]
        p = [jnp.exp2(x - m).astype(BF16) for x, m in zip(s, m_new)]
        for h, sl in enumerate(heads):
            m_ref[h] = m_new[h]
            pv = _dot(vt_of(sl), p[h])
            acc_ref[h] = pv if first else jnp.exp2(m_old[h] - m_new[h]) * acc_ref[h] + pv

    tile(lambda sl: km_ref[:, sl], lambda sl: vtm_ref[0, sl, :], None, first=True)

    def x_tile(j, mask):
        rows = pl.ds(pl.multiple_of(j * tq, tq), tq)
        tile(lambda sl: k_ref[rows, sl], lambda sl: vt_ref[j, sl, :], mask)

    def body(j, carry):
        x_tile(j, None)
        return carry

    lax.fori_loop(0, i, body, 0)
    x_tile(i, lax.broadcasted_iota(I32, (tq, tq), 0) <= lax.broadcasted_iota(I32, (tq, tq), 1))

    outs = []
    for h in range(MLA_HEADS):
        acc = acc_ref[h]
        outs.append(acc[:MLA_V, :] / acc[ONE_LANE:ONE_LANE + 1, :])
    o_ref[...] = jnp.concatenate(outs, axis=0).T.astype(BF16)


def _attention(q, k, vt, k_meta, vt_meta, batch, seq, tq):
    wide = MLA_HEADS * SLOT
    nq = seq // tq
    assert vt.shape == (batch * nq, wide, tq)
    return pl.pallas_call(
        _attn_kernel,
        grid=(batch, nq),
        in_specs=[
            pl.BlockSpec((tq, wide), lambda b, i: (b * nq + i, 0)),
            pl.BlockSpec((seq, wide), lambda b, i: (b, 0)),
            pl.BlockSpec((nq, wide, tq), lambda b, i: (b, 0, 0)),
            _const_spec(k_meta.shape),
            _const_spec(vt_meta.shape),
        ],
        out_specs=pl.BlockSpec((tq, MLA_HEADS * MLA_V), lambda b, i: (b * nq + i, 0)),
        out_shape=jax.ShapeDtypeStruct((batch * seq, MLA_HEADS * MLA_V), BF16),
        scratch_shapes=[pltpu.VMEM((MLA_HEADS, 1, tq), F32), pltpu.VMEM((MLA_HEADS, SLOT, tq), F32)],
        compiler_params=_params("parallel", "arbitrary"),
        name="mla_attention",
    )(q, k, vt, k_meta, vt_meta)


def _cumsum_rows(tril, lf):
    parts = _split3(lf)
    return _dot(tril, parts[0]) + _dot(tril, parts[1]) + _dot(tril, parts[2])


def _tril(n):
    return (lax.broadcasted_iota(I32, (n, n), 0) >= lax.broadcasted_iota(I32, (n, n), 1)).astype(BF16)


def _hgrn_meta_kernel(lf_ref, hi_ref, s_out):
    tril = _tril(lf_ref.shape[0])
    for h in range(HG_HEADS):
        sl = slice(h * HG_DK, (h + 1) * HG_DK)
        lf = lf_ref[:, sl]
        b = _cumsum_rows(tril, lf)
        kdec = (1.0 - jnp.exp2(lf)) * jnp.exp2(b[-1:, :] - b)
        s_out[h] = _dot_tn(hi_ref[:, sl], kdec.astype(BF16))


def _hgrn_meta_state(lf, hi):
    return pl.pallas_call(
        _hgrn_meta_kernel,
        out_shape=jax.ShapeDtypeStruct((HG_HEADS, HG_DV, HG_DK), F32),
        compiler_params=pltpu.CompilerParams(vmem_limit_bytes=VMEM_LIMIT_BYTES),
        name="hgrn_meta_state",
    )(lf, hi)


def _hgrn_kernel(hq_ref, lf_ref, hi_ref, hg_ref, hgn_ref, s0_ref, o_ref,
                 st_ref, pb_ref, pk_ref, pv_ref):
    c_rows = HG_CHUNK
    n_chunks = hq_ref.shape[0] // c_rows
    st_ref[...] = s0_ref[...]
    pad = jnp.zeros((HG_HEADS, HG_SUB, HG_DK), F32)
    pb_ref[:, 0:HG_SUB, :] = pad
    pk_ref[:, 0:HG_SUB, :] = pad
    pv_ref[:, 0:HG_SUB, :] = pad
    hgn = hgn_ref[...]
    tril = _tril(c_rows)
    ones = jnp.ones((HG_DK, LANES), BF16)
    n_sub = c_rows // HG_SUB
    far = (lax.broadcasted_iota(I32, (c_rows, c_rows), 0) - lax.broadcasted_iota(I32, (c_rows, c_rows), 1)) >= HG_SUB
    heads = range(HG_HEADS)
    cols = [slice(h * HG_DK, (h + 1) * HG_DK) for h in heads]
    stash = slice(HG_SUB, HG_SUB + c_rows)

    def stages(rows, hs):
        lf = {h: lf_ref[rows, cols[h]] for h in hs}
        kk = {h: 1.0 - jnp.exp2(lf[h]) for h in hs}
        b = {h: _cumsum_rows(tril, lf[h]) for h in hs}
        q = {h: hq_ref[rows, cols[h]].astype(F32) for h in hs}
        v_bf = {h: hi_ref[rows, cols[h]] for h in hs}
        st = {h: st_ref[h] for h in hs}
        for h in hs:
            pb_ref[h, stash, :] = b[h]
            pk_ref[h, stash, :] = kk[h]
            pv_ref[h, stash, :] = v_bf[h].astype(F32)

        y = {}
        for h in hs:
            xs = [(q[h] * kk[h]).astype(BF16)]
            for d in range(1, HG_SUB):
                shifted = slice(HG_SUB - d, HG_SUB - d + c_rows)
                xs.append((q[h] * pk_ref[h, shifted, :] * jnp.exp2(b[h] - pb_ref[h, shifted, :])).astype(BF16))
            y[h] = [_dot(jnp.concatenate(xs[d0:d0 + BAND_GROUP], axis=0), ones)
                    for d0 in range(0, HG_SUB, BAND_GROUP)]

        o = {h: _dot_nt((q[h] * jnp.exp2(b[h])).astype(BF16), st[h].astype(BF16)) for h in hs}

        a_off = {}
        zero_rows = lambda r: [jnp.zeros((r, HG_DK), BF16)] if r else []
        for h in hs:
            qps, kps = [], []
            for j in range(n_sub - 1):
                k0, k1 = HG_SUB * j, HG_SUB * (j + 1)
                rj = b[h][k1 - 1:k1, :]
                qp = (q[h][k1:, :] * jnp.exp2(b[h][k1:, :] - rj)).astype(BF16)
                kp = (kk[h][k0:k1, :] * jnp.exp2(rj - b[h][k0:k1, :])).astype(BF16)
                qps.append(jnp.concatenate(zero_rows(k1) + [qp], axis=0))
                kps.append(jnp.concatenate(zero_rows(k0) + [kp] + zero_rows(c_rows - k1), axis=0))
            a = _dot_nt(jnp.concatenate(qps, axis=1), jnp.concatenate(kps, axis=1))
            a_off[h] = jnp.where(far, a, 0.0).astype(BF16)

        for h in hs:
            o[h] = o[h] + _dot(a_off[h], v_bf[h])
            for d in range(HG_SUB):
                shifted = slice(HG_SUB - d, HG_SUB - d + c_rows)
                part = y[h][d // BAND_GROUP]
                r0 = (d % BAND_GROUP) * c_rows
                o[h] = o[h] + part[r0:r0 + c_rows, :] * pv_ref[h, shifted, :]

        for h in hs:
            on = o[h] * lax.rsqrt(jnp.mean(o[h] * o[h], axis=-1, keepdims=True) + EPS)
            on = on * hgn[:, cols[h]] * hg_ref[rows, cols[h]].astype(F32)
            o_ref[rows, cols[h]] = on.astype(BF16)

        for h in hs:
            b_end = b[h][c_rows - 1:c_rows, :]
            kdec = kk[h] * jnp.exp2(b_end - b[h])
            st_ref[h] = st[h] * jnp.exp2(b_end) + _dot_tn(v_bf[h], kdec.astype(BF16))

    def chunk(c, carry):
        rows = pl.ds(pl.multiple_of(c * c_rows, c_rows), c_rows)
        for h0 in range(0, HG_HEADS, HEAD_SET):
            stages(rows, tuple(range(h0, h0 + HEAD_SET)))
        return carry

    lax.fori_loop(0, n_chunks, chunk, 0)


def _hgrn(hq, lf, hi, hg, hg_norm, s0, batch, seq):
    hw = HG_HEADS * HG_DK
    seq_spec = pl.BlockSpec((seq, hw), lambda b: (b, 0))
    pad_rows = HG_SUB + HG_CHUNK
    return pl.pallas_call(
        _hgrn_kernel,
        grid=(batch,),
        in_specs=[seq_spec, seq_spec, seq_spec, seq_spec, _const_spec(hg_norm.shape), _const_spec(s0.shape)],
        out_specs=seq_spec,
        out_shape=jax.ShapeDtypeStruct((batch * seq, hw), BF16),
        scratch_shapes=[pltpu.VMEM((HG_HEADS, HG_DV, HG_DK), F32)]
        + [pltpu.VMEM((HG_HEADS, pad_rows, HG_DK), F32)] * 3,
        compiler_params=_params("parallel"),
        name="hgrn2",
    )(hq, lf, hi, hg, hg_norm, s0)


ROUTE_CODE, ROUTE_E1, ROUTE_E2, ROUTE_G1, ROUTE_G2 = range(5)
RANK_BITS = 16
ROW_SUBLANES = 2 * SUBLANES


def _slot(code, pstart_ref):
    return pstart_ref[code >> RANK_BITS] + (code & ((1 << RANK_BITS) - 1))


def _merge_kernel(om_ref, oh_ref, gm_ref, gh_ref, x_ref, wbm_ref, wbh_ref, wo_ref, fn_ref,
                  wr_hi_ref, wr_lo_ref, br_ref, h_out, u_out, route_out, cnt_out, carry_ref):
    i = pl.program_id(0)
    tm, d = x_ref.shape

    @pl.when(i == 0)
    def _():
        carry_ref[...] = jnp.zeros_like(carry_ref)

    a = _dot(om_ref[...], wbm_ref[...])
    g = _dot(oh_ref[...], wbh_ref[...])
    merged = gm_ref[...].astype(F32) * a + gh_ref[...].astype(F32) * g
    h1 = x_ref[...] + _dot(merged.astype(BF16), wo_ref[...])
    h_out[...] = h1
    u = _rms(h1, fn_ref[...])
    u_out[:, 0:SUBLANES, :] = _rows_to_tiles(u)

    u_hi, u_lo = _split2(u)
    logits = (_dot(u_hi, wr_hi_ref[...]) + _dot(u_hi, wr_lo_ref[...]) + _dot(u_lo, wr_hi_ref[...])
              + br_ref[...])
    lane = lax.broadcasted_iota(I32, (tm, LANES), 1)
    lane_f = lane.astype(F32)
    big = float(2 * LANES)

    def first_max(vals):
        mx = jnp.max(vals, axis=-1, keepdims=True)
        idx = jnp.min(jnp.where(vals == mx, lane_f, big), axis=-1, keepdims=True)
        return mx, idx

    gl = jnp.where(lane < N_GROUPS, logits, -jnp.inf)
    g_max, g_sel = first_max(gl)
    p_sel = 1.0 / jnp.sum(jnp.exp(gl - g_max), axis=-1, keepdims=True)
    lo = N_GROUPS + g_sel * EXPERTS_PER_GROUP
    el = jnp.where((lane_f >= lo) & (lane_f < lo + EXPERTS_PER_GROUP), logits, -jnp.inf)
    v1, i1 = first_max(el)
    el2 = jnp.where(lane_f == i1, -jnp.inf, el)
    v2, i2 = first_max(el2)
    t = jnp.exp(v2 - v1)
    g1 = p_sel * (1.0 / (1.0 + t))
    g2 = p_sel * (t / (1.0 + t))

    hit = lane_f == g_sel
    onehot = jnp.where(hit, 1.0, 0.0)
    strict = (lax.broadcasted_iota(I32, (tm, tm), 0) > lax.broadcasted_iota(I32, (tm, tm), 1)).astype(BF16)
    before = _dot(strict, onehot.astype(BF16)) + carry_ref[0:1, :]
    rank = jnp.sum(jnp.where(hit, before, 0.0), axis=-1, keepdims=True)
    total = carry_ref[0:1, :] + jnp.sum(onehot, axis=0, keepdims=True)
    carry_ref[...] = jnp.broadcast_to(total, carry_ref.shape)
    cnt_out[...] = jnp.broadcast_to(total, cnt_out.shape)

    route = jnp.zeros((tm, LANES), F32)
    code = g_sel * float(1 << RANK_BITS) + rank
    for pos, val in ((ROUTE_CODE, code), (ROUTE_E1, i1 - N_GROUPS), (ROUTE_E2, i2 - N_GROUPS),
                     (ROUTE_G1, g1), (ROUTE_G2, g2)):
        route = jnp.where(lane == pos, val, route)
    route_out[...] = route
    record = jnp.concatenate([route, jnp.zeros((tm, d - LANES), F32)], axis=1)
    u_out[:, SUBLANES:ROW_SUBLANES, :] = _rows_to_tiles(record)


def _merge(o_mla, o_hg, gm, gh, x2, wts, tm):
    n, d = x2.shape
    row = lambda width: pl.BlockSpec((tm, width), lambda i: (i, 0))
    consts = (wts["w_br_mla"], wts["w_br_hgrn"], wts["w_out"], wts["ffn_norm"],
              wts["w_router_hi"], wts["w_router_lo"], wts["b_router"])
    return pl.pallas_call(
        _merge_kernel,
        grid=(n // tm,),
        in_specs=[row(o_mla.shape[1]), row(o_hg.shape[1]), row(d), row(d), row(d)]
        + [_const_spec(c.shape) for c in consts],
        out_specs=[row(d), pl.BlockSpec((tm, ROW_SUBLANES, LANES), lambda i: (i, 0, 0)), row(LANES),
                   pl.BlockSpec((8, LANES), lambda i: (0, 0))],
        out_shape=[jax.ShapeDtypeStruct((n, d), F32), jax.ShapeDtypeStruct((n, ROW_SUBLANES, LANES), F32),
                   jax.ShapeDtypeStruct((n, LANES), F32), jax.ShapeDtypeStruct((8, LANES), F32)],
        scratch_shapes=[pltpu.VMEM((8, LANES), F32)],
        compiler_params=_params("arbitrary"),
        name="merge_route",
    )(o_mla, o_hg, gm, gh, x2, *consts)


ROW_UNROLL = 8


def _row_copy(src_ref, src_row, dst_ref, dst_row, sem):
    return pltpu.make_async_copy(src_ref.at[pl.ds(src_row, 1)], dst_ref.at[pl.ds(dst_row, 1)], sem)


def _dispatch_kernel(code_ref, ps_ref, pad_lo_ref, pad_hi_ref, nu_ref, u_ref, xg_out, zero_ref, sem, pad_sem):
    i = pl.program_id(0)
    tm = u_ref.shape[0]
    base = i * tm

    @pl.when(i == 0)
    def _():
        zero_ref[...] = jnp.zeros_like(zero_ref)
        for g in range(N_GROUPS):
            def fill(r, carry):
                _row_copy(zero_ref, 0, xg_out, r, pad_sem).start()
                return carry

            def drain(r, carry):
                _row_copy(zero_ref, 0, xg_out, 0, pad_sem).wait()
                return carry

            lax.fori_loop(pad_lo_ref[g], pad_hi_ref[g], fill, 0)
            lax.fori_loop(pad_lo_ref[g], pad_hi_ref[g], drain, 0)

        def block_copy(b):
            rows = pl.ds(pl.multiple_of(b * MOE_BLOCK, MOE_BLOCK), MOE_BLOCK)
            return pltpu.make_async_copy(zero_ref, xg_out.at[rows], pad_sem)

        def fill_block(b, carry):
            block_copy(b).start()
            return carry

        def drain_block(b, carry):
            block_copy(b).wait()
            return carry

        n_blocks = xg_out.shape[0] // MOE_BLOCK
        lax.fori_loop(nu_ref[0], n_blocks, fill_block, 0)
        lax.fori_loop(nu_ref[0], n_blocks, drain_block, 0)

    def start(g, carry):
        r0 = pl.multiple_of(g * ROW_UNROLL, ROW_UNROLL)
        for k in range(ROW_UNROLL):
            _row_copy(u_ref, r0 + k, xg_out, _slot(code_ref[base + r0 + k], ps_ref), sem).start(priority=k % 2)
        return carry

    def wait(g, carry):
        for _ in range(ROW_UNROLL):
            _row_copy(u_ref, 0, xg_out, 0, sem).wait()
        return carry

    lax.fori_loop(0, tm // ROW_UNROLL, start, 0)
    lax.fori_loop(0, tm // ROW_UNROLL, wait, 0)


def _dispatch(code, p_start, pad_lo, pad_hi, n_used, u3, n_slots, tm):
    n = u3.shape[0]
    tile = u3.shape[1:]
    return pl.pallas_call(
        _dispatch_kernel,
        grid_spec=pltpu.PrefetchScalarGridSpec(
            num_scalar_prefetch=5,
            grid=(n // tm,),
            in_specs=[pl.BlockSpec((tm,) + tile, lambda i, *_: (i, 0, 0))],
            out_specs=pl.BlockSpec(memory_space=pl.ANY),
            scratch_shapes=[pltpu.VMEM((MOE_BLOCK,) + tile, F32), pltpu.SemaphoreType.DMA,
                            pltpu.SemaphoreType.DMA],
        ),
        out_shape=jax.ShapeDtypeStruct((n_slots,) + tile, F32),
        compiler_params=_params("arbitrary"),
        name="moe_dispatch",
    )(code, p_start, pad_lo, pad_hi, n_used, u3)


def _expert_kernel(bg_ref, nu_ref, xg_ref, w1_ref, w3_ref, w2_ref, y_ref, w13_s, w2_s, acc_ref):
    i = pl.program_id(0)
    used = i < nu_ref[0]

    @pl.when(used)
    def _():
        group = bg_ref[i]

        @pl.when((i == 0) | (group != bg_ref[jnp.maximum(i - 1, 0)]))
        def _():
            for e in range(EXPERTS_PER_GROUP):
                w13_s[e, :, 0:D_EXPERT] = w1_ref[e].astype(BF16)
                w13_s[e, :, D_EXPERT:2 * D_EXPERT] = w3_ref[e].astype(BF16)
                w2_s[e] = w2_ref[e].astype(BF16)

        x = _tiles_to_rows(xg_ref[:, 0:SUBLANES, :]).astype(BF16)
        route = _tiles_to_rows(xg_ref[:, SUBLANES:ROW_SUBLANES, :])[:, 0:LANES]
        e1 = route[:, ROUTE_E1:ROUTE_E1 + 1]
        e2 = route[:, ROUTE_E2:ROUTE_E2 + 1]
        g1 = route[:, ROUTE_G1:ROUTE_G1 + 1]
        g2 = route[:, ROUTE_G2:ROUTE_G2 + 1]
        first = (group * EXPERTS_PER_GROUP).astype(F32)
        hidden = []
        for e in range(EXPERTS_PER_GROUP):
            hcat = _dot(x, w13_s[e])
            hidden.append((jax.nn.silu(hcat[:, 0:D_EXPERT]) * hcat[:, D_EXPERT:2 * D_EXPERT]).astype(BF16))
        for e in range(EXPERTS_PER_GROUP):
            is1 = e1 == first + e
            is2 = e2 == first + e
            gate = jnp.where(is1, g1, 0.0) + jnp.where(is2, g2, 0.0)
            part = jnp.where(is1 | is2, gate * _dot(hidden[e], w2_s[e]), 0.0)
            if e == 0:
                acc_ref[...] = part
            else:
                acc_ref[...] += part
        y_ref[...] = _rows_to_tiles(acc_ref[...])

    @pl.when(jnp.logical_not(used))
    def _():
        y_ref[...] = jnp.zeros_like(y_ref)


def _experts(blk_group, n_used, xg, w1, w3, w2):
    r = xg.shape[0]
    d = SUBLANES * LANES
    n_blocks = r // MOE_BLOCK
    last_used = lambda i, nu: jnp.minimum(i, nu[0] - 1)
    group_weights = lambda shape: pl.BlockSpec((EXPERTS_PER_GROUP,) + shape, lambda i, bg, nu: (bg[i], 0, 0),
                                               pipeline_mode=pl.Buffered(1))
    return pl.pallas_call(
        _expert_kernel,
        grid_spec=pltpu.PrefetchScalarGridSpec(
            num_scalar_prefetch=2,
            grid=(n_blocks,),
            in_specs=[
                pl.BlockSpec((MOE_BLOCK, ROW_SUBLANES, LANES), lambda i, bg, nu: (last_used(i, nu), 0, 0)),
                group_weights((d, D_EXPERT)),
                group_weights((d, D_EXPERT)),
                group_weights((D_EXPERT, d)),
            ],
            out_specs=pl.BlockSpec((MOE_BLOCK, SUBLANES, LANES), lambda i, bg, nu: (i, 0, 0)),
            scratch_shapes=[pltpu.VMEM((EXPERTS_PER_GROUP, d, 2 * D_EXPERT), BF16),
                            pltpu.VMEM((EXPERTS_PER_GROUP, D_EXPERT, d), BF16),
                            pltpu.VMEM((MOE_BLOCK, d), F32)],
        ),
        out_shape=jax.ShapeDtypeStruct((r, SUBLANES, LANES), F32),
        compiler_params=_params("arbitrary"),
        name="moe_experts",
    )(blk_group, n_used, xg, w1, w3, w2)


def _combine_kernel(code_ref, ps_ref, y_ref, h_ref, fn_ref, o_ref, buf_ref, sem):
    tm = h_ref.shape[0]
    base = pl.program_id(0) * tm

    def start(g, carry):
        r0 = pl.multiple_of(g * ROW_UNROLL, ROW_UNROLL)
        for k in range(ROW_UNROLL):
            _row_copy(y_ref, _slot(code_ref[base + r0 + k], ps_ref), buf_ref, r0 + k, sem).start(priority=k % 2)
        return carry

    def wait(g, carry):
        for _ in range(ROW_UNROLL):
            _row_copy(y_ref, 0, buf_ref, 0, sem).wait()
        return carry

    lax.fori_loop(0, tm // ROW_UNROLL, start, 0)
    lax.fori_loop(0, tm // ROW_UNROLL, wait, 0)
    o_ref[...] = _rms(h_ref[...] + _tiles_to_rows(buf_ref[...]), fn_ref[...])


def _combine(code, p_start, y3, h1, final_norm, tm):
    n, d = h1.shape
    return pl.pallas_call(
        _combine_kernel,
        grid_spec=pltpu.PrefetchScalarGridSpec(
            num_scalar_prefetch=2,
            grid=(n // tm,),
            in_specs=[
                pl.BlockSpec(memory_space=pl.ANY),
                pl.BlockSpec((tm, d), lambda i, *_: (i, 0)),
                pl.BlockSpec((1, d), lambda i, *_: (0, 0)),
            ],
            out_specs=pl.BlockSpec((tm, d), lambda i, *_: (i, 0)),
            scratch_shapes=[pltpu.VMEM((tm, SUBLANES, LANES), F32), pltpu.SemaphoreType.DMA],
        ),
        out_shape=jax.ShapeDtypeStruct((n, d), F32),
        compiler_params=_params("arbitrary"),
        name="moe_combine",
    )(code, p_start, y3, h1, final_norm)


def _prepare_weights(attn_norm, w_in, q_norm, w_uq, kv_norm, w_ukv, b_gate, hg_norm, w_br_mla, w_br_hgrn,
                     w_out, ffn_norm, w_group, b_group, w_route, b_route, final_norm, lb_table):
    d = w_in.shape[0]
    half = MLA_ROPE // 2
    pe0, pe1 = Q_LORA + KV_LORA, Q_LORA + KV_LORA + MLA_ROPE
    w_pe = w_in[:, pe0:pe1]
    zeros = lambda r, c: jnp.zeros((r, c), F32)
    kpe_slot = jnp.concatenate([zeros(d, MLA_NOPE), w_pe, zeros(d, SLOT - MLA_NOPE - MLA_ROPE)], axis=1)
    kpe_swap = jnp.concatenate([zeros(d, MLA_NOPE), -w_pe[:, half:], w_pe[:, :half],
                                zeros(d, SLOT - MLA_NOPE - MLA_ROPE)], axis=1)
    w_wide = jnp.concatenate([w_in[:, :pe0].astype(BF16), kpe_slot.astype(BF16), kpe_swap.astype(BF16),
                              w_in[:, pe1:].astype(BF16)], axis=1)

    wq3 = w_uq.reshape(Q_LORA, MLA_HEADS, MLA_NOPE + MLA_ROPE)
    q_nope, q_pe = wq3[..., :MLA_NOPE], wq3[..., MLA_NOPE:]
    zq = jnp.zeros((Q_LORA, MLA_HEADS, SLOT - MLA_NOPE - MLA_ROPE), F32)
    wq = jnp.concatenate([q_nope, q_pe, zq], axis=-1).reshape(Q_LORA, MLA_HEADS * SLOT).astype(BF16)
    wq_sw = jnp.concatenate([jnp.zeros_like(q_nope), -q_pe[..., half:], q_pe[..., :half], zq],
                            axis=-1).reshape(Q_LORA, MLA_HEADS * SLOT).astype(BF16)

    wkv3 = w_ukv.reshape(KV_LORA, MLA_HEADS, MLA_NOPE + MLA_V)
    zk = jnp.zeros((KV_LORA, MLA_HEADS, SLOT - MLA_NOPE), F32)
    wk = jnp.concatenate([wkv3[..., :MLA_NOPE], zk], axis=-1).reshape(KV_LORA, MLA_HEADS * SLOT).astype(BF16)
    zv = jnp.zeros((KV_LORA, MLA_HEADS, SLOT - MLA_V), F32)
    wv_t = jnp.concatenate([wkv3[..., MLA_NOPE:], zv], axis=-1).reshape(KV_LORA, MLA_HEADS * SLOT).T.astype(BF16)

    w_router = jnp.concatenate([w_group, w_route, zeros(d, LANES - N_GROUPS - N_EXPERTS)], axis=1)
    wr_hi = w_router.astype(BF16)
    wr_lo = (w_router - wr_hi.astype(F32)).astype(BF16)
    b_router = jnp.concatenate([b_group, b_route, jnp.zeros((LANES - N_GROUPS - N_EXPERTS,), F32)])[None, :]
    return {
        "attn_norm": attn_norm[None, :], "w_in": w_wide, "q_norm": q_norm[None, :], "kv_norm": kv_norm[None, :],
        "wq": wq, "wq_sw": wq_sw, "wk": wk, "wv_t": wv_t, "b_gate": b_gate, "lb_table": lb_table,
        "hg_norm": hg_norm[None, :], "w_br_mla": w_br_mla.astype(BF16), "w_br_hgrn": w_br_hgrn.astype(BF16),
        "w_out": w_out.astype(BF16), "ffn_norm": ffn_norm[None, :], "w_router_hi": wr_hi, "w_router_lo": wr_lo,
        "b_router": b_router, "final_norm": final_norm[None, :],
    }


def _rope_tables(length):
    inv = ROPE_BASE ** (-jnp.arange(0, MLA_ROPE, 2, dtype=F32) / MLA_ROPE)
    ang = jnp.arange(length, dtype=F32)[:, None] * inv[None, :]
    cos, sin = jnp.cos(ang), jnp.sin(ang)
    ones = jnp.ones((length, MLA_NOPE), F32)
    tail = SLOT - MLA_NOPE - MLA_ROPE
    cos_t = jnp.concatenate([ones, cos, cos, jnp.ones((length, tail), F32)], axis=1)
    sin_t = jnp.concatenate([0.0 * ones, sin, sin, jnp.zeros((length, tail), F32)], axis=1)
    return cos_t, sin_t


def kernel(x, meta_tokens, attn_norm, w_in, q_norm, w_uq, kv_norm, w_ukv, lb_table, hg_norm, w_br_mla, w_br_hgrn, b_gate, w_out, ffn_norm, w_group, b_group, w_route, b_route, w1, w3, w2, final_norm):
    batch, seq, d = x.shape
    assert attn_norm.shape[0] == 1, "one layer"
    assert d == SUBLANES * LANES, "a token row is moved as one (8, 128) tile"
    n = batch * seq
    x2 = x.reshape(n, d)
    wts = _prepare_weights(attn_norm[0], w_in[0], q_norm[0], w_uq[0], kv_norm[0], w_ukv[0], b_gate[0],
                           hg_norm[0], w_br_mla[0], w_br_hgrn[0], w_out[0], ffn_norm[0], w_group[0],
                           b_group[0], w_route[0], b_route[0], final_norm, lb_table)
    cos_t, sin_t = _rope_tables(N_META + seq)

    meta = _in_proj(meta_tokens.astype(x.dtype), N_META, 1, cos_t[:N_META], sin_t[:N_META], wts)
    s0 = _hgrn_meta_state(meta[4], meta[5])

    tm = min(ROW_TILE, seq)
    q, k, vt, hq, lf, hi, hg, gm, gh = _in_proj(x2, tm, seq // tm, cos_t[N_META:], sin_t[N_META:], wts)
    o_mla = _attention(q, k, vt, meta[1], meta[2], batch, seq, min(ATTN_TILE, seq))
    o_hg = _hgrn(hq, lf, hi, hg, wts["hg_norm"], s0, batch, seq)
    h1, u3, route, counts = _merge(o_mla, o_hg, gm, gh, x2, wts, tm)

    code = route[:, ROUTE_CODE].astype(I32)
    cnt = counts[0, :N_GROUPS].astype(I32)
    padded = (cnt + MOE_BLOCK - 1) // MOE_BLOCK * MOE_BLOCK
    p_end = jnp.cumsum(padded)
    p_start = p_end - padded
    n_blocks = -(-n // MOE_BLOCK) + N_GROUPS
    blk_row = jnp.arange(n_blocks, dtype=I32) * MOE_BLOCK
    blk_group = jnp.minimum(jnp.sum((p_end[None, :] <= blk_row[:, None]).astype(I32), axis=1), N_GROUPS - 1)
    n_used = p_end[-1:] // MOE_BLOCK

    tmove = min(MOVE_TILE, seq)
    xg = _dispatch(code, p_start, p_start + cnt, p_end, n_used, u3, n_blocks * MOE_BLOCK, tmove)
    y3 = _experts(blk_group, n_used, xg, w1[0], w3[0], w2[0])
    out = _combine(code, p_start, y3, h1, wts["final_norm"], tmove)
    return out.reshape(batch, seq, d)
```

```python
import jax
import jax.numpy as jnp
from jax import lax
from jax.experimental import pallas as pl
from jax.experimental.pallas import tpu as pltpu

F32 = jnp.float32
BF16 = jnp.bfloat16
I32 = jnp.int32

N_META = 16
EPS = 1e-6
MLA_HEADS = 8
MLA_NOPE = 64
MLA_ROPE = 32
MLA_V = 64
Q_LORA = 384
KV_LORA = 256
ROPE_BASE = 10000.0
HG_HEADS = 4
HG_DK = 128
HG_DV = 128
N_GROUPS = 8
EXPERTS_PER_GROUP = 8
N_EXPERTS = N_GROUPS * EXPERTS_PER_GROUP
D_EXPERT = 256
MOE_BLOCK = 256

LANES = 128
SUBLANES = 8
VMEM_LIMIT_BYTES = 56 * 1024 * 1024

SLOT = LANES
ONE_LANE = MLA_V
ROW_TILE = 512
ATTN_TILE = 512
ATTN_HEAD_SET = 4
MOVE_TILE = 512
HG_CHUNK = 64
HG_SUB = 8
BAND_GROUP = 8
HEAD_SET = HG_HEADS
LOG2_E = 1.4426950408889634

SEG_CQ = (0, 384)
SEG_CKV = (384, 640)
SEG_HQ = (0, 512)
SEG_HF = (512, 1024)
SEG_HI = (1024, 1536)
SEG_HG = (1536, 2048)
SEG_GM = (2048, 3072)
SEG_GH = (3072, 4096)


def _params(*sem):
    return pltpu.CompilerParams(dimension_semantics=sem, vmem_limit_bytes=VMEM_LIMIT_BYTES)


def _rms(x, g):
    return x * lax.rsqrt(jnp.mean(x * x, axis=-1, keepdims=True) + EPS) * g


def _dot(a, b):
    return jnp.dot(a, b, preferred_element_type=F32)


def _dot_nt(a, b):
    return lax.dot_general(a, b, (((1,), (1,)), ((), ())), preferred_element_type=F32)


def _dot_tn(a, b):
    return lax.dot_general(a, b, (((0,), (0,)), ((), ())), preferred_element_type=F32)


def _split2(x):
    hi = x.astype(BF16)
    lo = (x - hi.astype(F32)).astype(BF16)
    return hi, lo


def _split3(x):
    hi = x.astype(BF16)
    r = x - hi.astype(F32)
    mid = r.astype(BF16)
    lo = (r - mid.astype(F32)).astype(BF16)
    return hi, mid, lo


def _const_spec(shape):
    nd = len(shape)
    return pl.BlockSpec(shape, lambda *_: (0,) * nd, pipeline_mode=pl.Buffered(1))


def _rows_to_tiles(x):
    return x.reshape(x.shape[0], SUBLANES, LANES)


def _tiles_to_rows(x3):
    return x3.reshape(x3.shape[0], SUBLANES * LANES)


def _lower_bound(lbt_ref):
    t0 = lbt_ref[0:1, :]
    t1 = lbt_ref[1:2, :]
    mx = jnp.maximum(t0, t1)
    e0 = jnp.exp(t0 - mx)
    return e0 / (e0 + jnp.exp(t1 - mx))


def _rope(x, cos, sin):
    half = MLA_ROPE // 2
    width = x.shape[1]
    slot_lane = lax.broadcasted_iota(I32, x.shape, 1) & (SLOT - 1)
    partner = jnp.where(slot_lane < MLA_NOPE + half, -pltpu.roll(x, width - half, axis=1),
                        pltpu.roll(x, half, axis=1))
    return x * cos + partner * sin


def _in_proj_kernel(x_ref, g_ref, wl_ref, wp_ref, wr_ref, qn_ref, kvn_ref, wq_ref, wqs_ref, wk_ref, wvt_ref,
                    cos_ref, sin_ref, bg_ref, lbt_ref,
                    q_out, k_out, vt_out, hq_out, lf_out, hi_out, hg_out, gm_out, gh_out):
    u = _rms(x_ref[...], g_ref[...]).astype(BF16)

    def lat(s):
        return _dot(u, wl_ref[:, s[0]:s[1]])

    def seg(s):
        return _dot(u, wr_ref[:, s[0]:s[1]])

    cos = cos_ref[...]
    sin = sin_ref[...]
    cos_t = jnp.tile(cos, (1, MLA_HEADS))
    sin_t = jnp.tile(sin, (1, MLA_HEADS))
    scale = (MLA_NOPE + MLA_ROPE) ** -0.5 * LOG2_E

    cqn = _rms(lat(SEG_CQ), qn_ref[...]).astype(BF16)
    q = _dot(cqn, wq_ref[...]) * cos_t + _dot(cqn, wqs_ref[...]) * sin_t
    q_out[...] = (q * scale).astype(BF16)

    ckvn = _rms(lat(SEG_CKV), kvn_ref[...]).astype(BF16)
    k_slot = _rope(_dot(u, wp_ref[...]), cos, sin)
    k_out[...] = (_dot(ckvn, wk_ref[...]) + jnp.tile(k_slot, (1, MLA_HEADS))).astype(BF16)
    vt = _dot_nt(wvt_ref[...], ckvn)
    slot_row = lax.broadcasted_iota(I32, vt.shape, 0) & (SLOT - 1)
    vt = jnp.where(slot_row == ONE_LANE, 1.0, vt).astype(BF16)
    tv = vt_out.shape[2]
    for piece in range(vt_out.shape[0]):
        vt_out[piece] = vt[:, piece * tv:(piece + 1) * tv]

    lb = _lower_bound(lbt_ref)
    hq_out[...] = jax.nn.silu(seg(SEG_HQ)).astype(BF16)
    lf_out[...] = jnp.log(lb + (1.0 - lb) * jax.nn.sigmoid(seg(SEG_HF))) * LOG2_E
    hi_out[...] = seg(SEG_HI).astype(BF16)
    hg_out[...] = jax.nn.silu(seg(SEG_HG)).astype(BF16)
    gm_out[...] = jax.nn.sigmoid(seg(SEG_GM) + bg_ref[0:1, :]).astype(BF16)
    gh_out[...] = jax.nn.sigmoid(seg(SEG_GH) + bg_ref[1:2, :]).astype(BF16)


def _in_proj(x2, tm, tiles_per_seq, cos, sin, wts):
    n, d = x2.shape
    wide = MLA_HEADS * SLOT
    hw = HG_HEADS * HG_DK
    row = lambda width: pl.BlockSpec((tm, width), lambda i: (i, 0))
    tab = pl.BlockSpec((tm, SLOT), lambda i: (i % tiles_per_seq, 0))
    consts = (wts["attn_norm"], wts["w_lat"], wts["w_kpe"], wts["w_rest"], wts["q_norm"], wts["kv_norm"],
              wts["wq"], wts["wq_sw"], wts["wk"], wts["wv_t"])
    tail = (wts["b_gate"], wts["lb_table"])
    in_specs = ([row(d)] + [_const_spec(c.shape) for c in consts] + [tab, tab]
                + [_const_spec(c.shape) for c in tail])
    widths = (wide, wide, None, hw, hw, hw, hw, d, d)
    dtypes = (BF16, BF16, BF16, BF16, F32, BF16, BF16, BF16, BF16)
    tv = min(tm, ATTN_TILE)
    vt_spec = pl.BlockSpec((tm // tv, wide, tv), lambda i: (i, 0, 0))
    return pl.pallas_call(
        _in_proj_kernel,
        grid=(n // tm,),
        in_specs=in_specs,
        out_specs=[vt_spec if w is None else row(w) for w in widths],
        out_shape=[jax.ShapeDtypeStruct((n // tv, wide, tv) if w is None else (n, w), t)
                   for w, t in zip(widths, dtypes)],
        compiler_params=_params("parallel"),
        name="in_proj",
    )(x2, *consts, cos, sin, *tail)


def _attn_kernel(q_ref, k_ref, vt_ref, km_ref, vtm_ref, o_ref, m_ref, acc_ref):
    i = pl.program_id(1)
    tq = q_ref.shape[0]

    slots = [slice(h * SLOT, (h + 1) * SLOT) for h in range(MLA_HEADS)]

    def head_set(hs, k_of, vt_of, mask, first):
        s = [_dot_nt(k_of(slots[h]), q_ref[:, slots[h]]) for h in hs]
        if mask is not None:
            s = [jnp.where(mask, x, -jnp.inf) for x in s]
        mx = [jnp.max(x, axis=0, keepdims=True) for x in s]
        if first:
            m_new = mx
        else:
            m_old = [m_ref[h] for h in hs]
            m_new = [jnp.maximum(a, b) for a, b in zip(m_old, mx)]
        p = [jnp.exp2(x - m).astype(BF16) for x, m in zip(s, m_new)]
        for n, h in enumerate(hs):
            m_ref[h] = m_new[n]
            pv = _dot(vt_of(slots[h]), p[n])
            acc_ref[h] = pv if first else jnp.exp2(m_old[n] - m_new[n]) * acc_ref[h] + pv

    def tile(k_of, vt_of, mask, first=False):
        for h0 in range(0, MLA_HEADS, ATTN_HEAD_SET):
            head_set(range(h0, h0 + ATTN_HEAD_SET), k_of, vt_of, mask, first)

    tile(lambda sl: km_ref[:, sl], lambda sl: vtm_ref[0, sl, :], None, first=True)

    def x_tile(j, mask):
        rows = pl.ds(pl.multiple_of(j * tq, tq), tq)
        tile(lambda sl: k_ref[rows, sl], lambda sl: vt_ref[j, sl, :], mask)

    def body(j, carry):
        x_tile(j, None)
        return carry

    lax.fori_loop(0, i, body, 0)
    x_tile(i, lax.broadcasted_iota(I32, (tq, tq), 0) <= lax.broadcasted_iota(I32, (tq, tq), 1))

    outs = []
    for h in range(MLA_HEADS):
        acc = acc_ref[h]
        outs.append(acc[:MLA_V, :] / acc[ONE_LANE:ONE_LANE + 1, :])
    o_ref[...] = jnp.concatenate(outs, axis=0).T.astype(BF16)


def _attention(q, k, vt, k_meta, vt_meta, batch, seq, tq):
    wide = MLA_HEADS * SLOT
    nq = seq // tq
    assert vt.shape == (batch * nq, wide, tq)
    return pl.pallas_call(
        _attn_kernel,
        grid=(batch, nq),
        in_specs=[
            pl.BlockSpec((tq, wide), lambda b, i: (b * nq + i, 0)),
            pl.BlockSpec((seq, wide), lambda b, i: (b, 0)),
            pl.BlockSpec((nq, wide, tq), lambda b, i: (b, 0, 0)),
            _const_spec(k_meta.shape),
            _const_spec(vt_meta.shape),
        ],
        out_specs=pl.BlockSpec((tq, MLA_HEADS * MLA_V), lambda b, i: (b * nq + i, 0)),
        out_shape=jax.ShapeDtypeStruct((batch * seq, MLA_HEADS * MLA_V), BF16),
        scratch_shapes=[pltpu.VMEM((MLA_HEADS, 1, tq), F32), pltpu.VMEM((MLA_HEADS, SLOT, tq), F32)],
        compiler_params=_params("parallel", "arbitrary"),
        name="mla_attention",
    )(q, k, vt, k_meta, vt_meta)


def _cumsum_rows(tril, lf):
    parts = _split3(lf)
    return _dot(tril, parts[0]) + _dot(tril, parts[1]) + _dot(tril, parts[2])


def _tril(n):
    return (lax.broadcasted_iota(I32, (n, n), 0) >= lax.broadcasted_iota(I32, (n, n), 1)).astype(BF16)


def _hgrn_meta_kernel(lf_ref, hi_ref, s_out):
    tril = _tril(lf_ref.shape[0])
    for h in range(HG_HEADS):
        sl = slice(h * HG_DK, (h + 1) * HG_DK)
        lf = lf_ref[:, sl]
        b = _cumsum_rows(tril, lf)
        kdec = (1.0 - jnp.exp2(lf)) * jnp.exp2(b[-1:, :] - b)
        s_out[h] = _dot_tn(hi_ref[:, sl], kdec.astype(BF16))


def _hgrn_meta_state(lf, hi):
    return pl.pallas_call(
        _hgrn_meta_kernel,
        out_shape=jax.ShapeDtypeStruct((HG_HEADS, HG_DV, HG_DK), F32),
        compiler_params=pltpu.CompilerParams(vmem_limit_bytes=VMEM_LIMIT_BYTES),
        name="hgrn_meta_state",
    )(lf, hi)


def _hgrn_kernel(hq_ref, lf_ref, hi_ref, hg_ref, hgn_ref, s0_ref, o_ref,
                 st_ref, pb_ref, pk_ref, pv_ref):
    c_rows = HG_CHUNK
    n_chunks = hq_ref.shape[0] // c_rows
    st_ref[...] = s0_ref[...]
    pad = jnp.zeros((HG_HEADS, HG_SUB, HG_DK), F32)
    pb_ref[:, 0:HG_SUB, :] = pad
    pk_ref[:, 0:HG_SUB, :] = pad
    pv_ref[:, 0:HG_SUB, :] = pad
    hgn = hgn_ref[...]
    tril = _tril(c_rows)
    ones = jnp.ones((HG_DK, LANES), BF16)
    n_sub = c_rows // HG_SUB
    far = (lax.broadcasted_iota(I32, (c_rows, c_rows), 0) - lax.broadcasted_iota(I32, (c_rows, c_rows), 1)) >= HG_SUB
    heads = range(HG_HEADS)
    cols = [slice(h * HG_DK, (h + 1) * HG_DK) for h in heads]
    stash = slice(HG_SUB, HG_SUB + c_rows)

    def stages(rows, hs):
        lf = {h: lf_ref[rows, cols[h]] for h in hs}
        kk = {h: 1.0 - jnp.exp2(lf[h]) for h in hs}
        b = {h: _cumsum_rows(tril, lf[h]) for h in hs}
        q = {h: hq_ref[rows, cols[h]].astype(F32) for h in hs}
        v_bf = {h: hi_ref[rows, cols[h]] for h in hs}
        st = {h: st_ref[h] for h in hs}
        for h in hs:
            pb_ref[h, stash, :] = b[h]
            pk_ref[h, stash, :] = kk[h]
            pv_ref[h, stash, :] = v_bf[h].astype(F32)

        y = {}
        for h in hs:
            xs = [(q[h] * kk[h]).astype(BF16)]
            for d in range(1, HG_SUB):
                shifted = slice(HG_SUB - d, HG_SUB - d + c_rows)
                xs.append((q[h] * pk_ref[h, shifted, :] * jnp.exp2(b[h] - pb_ref[h, shifted, :])).astype(BF16))
            y[h] = [_dot(jnp.concatenate(xs[d0:d0 + BAND_GROUP], axis=0), ones)
                    for d0 in range(0, HG_SUB, BAND_GROUP)]

        o = {h: _dot_nt((q[h] * jnp.exp2(b[h])).astype(BF16), st[h].astype(BF16)) for h in hs}

        a_off = {}
        zero_rows = lambda r: [jnp.zeros((r, HG_DK), BF16)] if r else []
        for h in hs:
            qps, kps = [], []
            for j in range(n_sub - 1):
                k0, k1 = HG_SUB * j, HG_SUB * (j + 1)
                rj = b[h][k1 - 1:k1, :]
                qp = (q[h][k1:, :] * jnp.exp2(b[h][k1:, :] - rj)).astype(BF16)
                kp = (kk[h][k0:k1, :] * jnp.exp2(rj - b[h][k0:k1, :])).astype(BF16)
                qps.append(jnp.concatenate(zero_rows(k1) + [qp], axis=0))
                kps.append(jnp.concatenate(zero_rows(k0) + [kp] + zero_rows(c_rows - k1), axis=0))
            a = _dot_nt(jnp.concatenate(qps, axis=1), jnp.concatenate(kps, axis=1))
            a_off[h] = jnp.where(far, a, 0.0).astype(BF16)

        for h in hs:
            o[h] = o[h] + _dot(a_off[h], v_bf[h])
            for d in range(HG_SUB):
                shifted = slice(HG_SUB - d, HG_SUB - d + c_rows)
                part = y[h][d // BAND_GROUP]
                r0 = (d % BAND_GROUP) * c_rows
                o[h] = o[h] + part[r0:r0 + c_rows, :] * pv_ref[h, shifted, :]

        for h in hs:
            on = o[h] * lax.rsqrt(jnp.mean(o[h] * o[h], axis=-1, keepdims=True) + EPS)
            on = on * hgn[:, cols[h]] * hg_ref[rows, cols[h]].astype(F32)
            o_ref[rows, cols[h]] = on.astype(BF16)

        for h in hs:
            b_end = b[h][c_rows - 1:c_rows, :]
            kdec = kk[h] * jnp.exp2(b_end - b[h])
            st_ref[h] = st[h] * jnp.exp2(b_end) + _dot_tn(v_bf[h], kdec.astype(BF16))

    def chunk(c, carry):
        rows = pl.ds(pl.multiple_of(c * c_rows, c_rows), c_rows)
        for h0 in range(0, HG_HEADS, HEAD_SET):
            stages(rows, tuple(range(h0, h0 + HEAD_SET)))
        return carry

    lax.fori_loop(0, n_chunks, chunk, 0)


def _hgrn(hq, lf, hi, hg, hg_norm, s0, batch, seq):
    hw = HG_HEADS * HG_DK
    seq_spec = pl.BlockSpec((seq, hw), lambda b: (b, 0))
    pad_rows = HG_SUB + HG_CHUNK
    return pl.pallas_call(
        _hgrn_kernel,
        grid=(batch,),
        in_specs=[seq_spec, seq_spec, seq_spec, seq_spec, _const_spec(hg_norm.shape), _const_spec(s0.shape)],
        out_specs=seq_spec,
        out_shape=jax.ShapeDtypeStruct((batch * seq, hw), BF16),
        scratch_shapes=[pltpu.VMEM((HG_HEADS, HG_DV, HG_DK), F32)]
        + [pltpu.VMEM((HG_HEADS, pad_rows, HG_DK), F32)] * 3,
        compiler_params=_params("parallel"),
        name="hgrn2",
    )(hq, lf, hi, hg, hg_norm, s0)


ROUTE_CODE, ROUTE_E1, ROUTE_E2, ROUTE_G1, ROUTE_G2 = range(5)
RANK_BITS = 16
ROW_SUBLANES = 2 * SUBLANES


def _slot(code, pstart_ref):
    return pstart_ref[code >> RANK_BITS] + (code & ((1 << RANK_BITS) - 1))


def _merge_kernel(om_ref, oh_ref, gm_ref, gh_ref, x_ref, wbm_ref, wbh_ref, wo_ref, fn_ref,
                  wr_hi_ref, wr_lo_ref, br_ref, h_out, u_out, route_out, cnt_out, carry_ref):
    i = pl.program_id(0)
    tm, d = x_ref.shape

    @pl.when(i == 0)
    def _():
        carry_ref[...] = jnp.zeros_like(carry_ref)

    a = _dot(om_ref[...], wbm_ref[...])
    g = _dot(oh_ref[...], wbh_ref[...])
    merged = gm_ref[...].astype(F32) * a + gh_ref[...].astype(F32) * g
    h1 = x_ref[...] + _dot(merged.astype(BF16), wo_ref[...])
    h_out[...] = h1
    u = _rms(h1, fn_ref[...])
    u_out[:, 0:SUBLANES, :] = _rows_to_tiles(u)

    u_hi, u_lo = _split2(u)
    logits = (_dot(u_hi, wr_hi_ref[...]) + _dot(u_hi, wr_lo_ref[...]) + _dot(u_lo, wr_hi_ref[...])
              + br_ref[...])
    lane = lax.broadcasted_iota(I32, (tm, LANES), 1)
    lane_f = lane.astype(F32)
    big = float(2 * LANES)

    def first_max(vals):
        mx = jnp.max(vals, axis=-1, keepdims=True)
        idx = jnp.min(jnp.where(vals == mx, lane_f, big), axis=-1, keepdims=True)
        return mx, idx

    gl = jnp.where(lane < N_GROUPS, logits, -jnp.inf)
    g_max, g_sel = first_max(gl)
    p_sel = 1.0 / jnp.sum(jnp.exp(gl - g_max), axis=-1, keepdims=True)
    lo = N_GROUPS + g_sel * EXPERTS_PER_GROUP
    el = jnp.where((lane_f >= lo) & (lane_f < lo + EXPERTS_PER_GROUP), logits, -jnp.inf)
    v1, i1 = first_max(el)
    el2 = jnp.where(lane_f == i1, -jnp.inf, el)
    v2, i2 = first_max(el2)
    t = jnp.exp(v2 - v1)
    g1 = p_sel * (1.0 / (1.0 + t))
    g2 = p_sel * (t / (1.0 + t))

    hit = lane_f == g_sel
    onehot = jnp.where(hit, 1.0, 0.0)
    strict = (lax.broadcasted_iota(I32, (tm, tm), 0) > lax.broadcasted_iota(I32, (tm, tm), 1)).astype(BF16)
    before = _dot(strict, onehot.astype(BF16)) + carry_ref[0:1, :]
    rank = jnp.sum(jnp.where(hit, before, 0.0), axis=-1, keepdims=True)
    total = carry_ref[0:1, :] + jnp.sum(onehot, axis=0, keepdims=True)
    carry_ref[...] = jnp.broadcast_to(total, carry_ref.shape)
    cnt_out[...] = jnp.broadcast_to(total, cnt_out.shape)

    route = jnp.zeros((tm, LANES), F32)
    code = g_sel * float(1 << RANK_BITS) + rank
    for pos, val in ((ROUTE_CODE, code), (ROUTE_E1, i1 - N_GROUPS), (ROUTE_E2, i2 - N_GROUPS),
                     (ROUTE_G1, g1), (ROUTE_G2, g2)):
        route = jnp.where(lane == pos, val, route)
    route_out[...] = route
    record = jnp.concatenate([route, jnp.zeros((tm, d - LANES), F32)], axis=1)
    u_out[:, SUBLANES:ROW_SUBLANES, :] = _rows_to_tiles(record)


def _merge(o_mla, o_hg, gm, gh, x2, wts, tm):
    n, d = x2.shape
    row = lambda width: pl.BlockSpec((tm, width), lambda i: (i, 0))
    consts = (wts["w_br_mla"], wts["w_br_hgrn"], wts["w_out"], wts["ffn_norm"],
              wts["w_router_hi"], wts["w_router_lo"], wts["b_router"])
    return pl.pallas_call(
        _merge_kernel,
        grid=(n // tm,),
        in_specs=[row(o_mla.shape[1]), row(o_hg.shape[1]), row(d), row(d), row(d)]
        + [_const_spec(c.shape) for c in consts],
        out_specs=[row(d), pl.BlockSpec((tm, ROW_SUBLANES, LANES), lambda i: (i, 0, 0)), row(LANES),
                   pl.BlockSpec((8, LANES), lambda i: (0, 0))],
        out_shape=[jax.ShapeDtypeStruct((n, d), F32), jax.ShapeDtypeStruct((n, ROW_SUBLANES, LANES), F32),
                   jax.ShapeDtypeStruct((n, LANES), F32), jax.ShapeDtypeStruct((8, LANES), F32)],
        scratch_shapes=[pltpu.VMEM((8, LANES), F32)],
        compiler_params=_params("arbitrary"),
        name="merge_route",
    )(o_mla, o_hg, gm, gh, x2, *consts)


ROW_UNROLL = 8


def _row_copy(src_ref, src_row, dst_ref, dst_row, sem):
    return pltpu.make_async_copy(src_ref.at[pl.ds(src_row, 1)], dst_ref.at[pl.ds(dst_row, 1)], sem)


def _dispatch_kernel(code_ref, ps_ref, pad_lo_ref, pad_hi_ref, nu_ref, u_ref, xg_out, zero_ref, sem, pad_sem):
    i = pl.program_id(0)
    tm = u_ref.shape[0]
    base = i * tm

    @pl.when(i == 0)
    def _():
        zero_ref[...] = jnp.zeros_like(zero_ref)
        for g in range(N_GROUPS):
            def fill(r, carry):
                _row_copy(zero_ref, 0, xg_out, r, pad_sem).start()
                return carry

            def drain(r, carry):
                _row_copy(zero_ref, 0, xg_out, 0, pad_sem).wait()
                return carry

            lax.fori_loop(pad_lo_ref[g], pad_hi_ref[g], fill, 0)
            lax.fori_loop(pad_lo_ref[g], pad_hi_ref[g], drain, 0)

        def block_copy(b):
            rows = pl.ds(pl.multiple_of(b * MOE_BLOCK, MOE_BLOCK), MOE_BLOCK)
            return pltpu.make_async_copy(zero_ref, xg_out.at[rows], pad_sem)

        def fill_block(b, carry):
            block_copy(b).start()
            return carry

        def drain_block(b, carry):
            block_copy(b).wait()
            return carry

        n_blocks = xg_out.shape[0] // MOE_BLOCK
        lax.fori_loop(nu_ref[0], n_blocks, fill_block, 0)
        lax.fori_loop(nu_ref[0], n_blocks, drain_block, 0)

    def start(g, carry):
        r0 = pl.multiple_of(g * ROW_UNROLL, ROW_UNROLL)
        for k in range(ROW_UNROLL):
            _row_copy(u_ref, r0 + k, xg_out, _slot(code_ref[base + r0 + k], ps_ref), sem).start(priority=k % 2)
        return carry

    def wait(g, carry):
        for _ in range(ROW_UNROLL):
            _row_copy(u_ref, 0, xg_out, 0, sem).wait()
        return carry

    lax.fori_loop(0, tm // ROW_UNROLL, start, 0)
    lax.fori_loop(0, tm // ROW_UNROLL, wait, 0)


def _dispatch(code, p_start, pad_lo, pad_hi, n_used, u3, n_slots, tm):
    n = u3.shape[0]
    tile = u3.shape[1:]
    return pl.pallas_call(
        _dispatch_kernel,
        grid_spec=pltpu.PrefetchScalarGridSpec(
            num_scalar_prefetch=5,
            grid=(n // tm,),
            in_specs=[pl.BlockSpec((tm,) + tile, lambda i, *_: (i, 0, 0))],
            out_specs=pl.BlockSpec(memory_space=pl.ANY),
            scratch_shapes=[pltpu.VMEM((MOE_BLOCK,) + tile, F32), pltpu.SemaphoreType.DMA,
                            pltpu.SemaphoreType.DMA],
        ),
        out_shape=jax.ShapeDtypeStruct((n_slots,) + tile, F32),
        compiler_params=_params("arbitrary"),
        name="moe_dispatch",
    )(code, p_start, pad_lo, pad_hi, n_used, u3)


def _expert_kernel(bg_ref, nu_ref, xg_ref, w1_ref, w3_ref, w2_ref, y_ref, w13_s, w2_s, acc_ref):
    i = pl.program_id(0)
    used = i < nu_ref[0]

    @pl.when(used)
    def _():
        group = bg_ref[i]

        @pl.when((i == 0) | (group != bg_ref[jnp.maximum(i - 1, 0)]))
        def _():
            for e in range(EXPERTS_PER_GROUP):
                w13_s[e, :, 0:D_EXPERT] = w1_ref[e].astype(BF16)
                w13_s[e, :, D_EXPERT:2 * D_EXPERT] = w3_ref[e].astype(BF16)
                w2_s[e] = w2_ref[e].astype(BF16)

        x = _tiles_to_rows(xg_ref[:, 0:SUBLANES, :]).astype(BF16)
        route = _tiles_to_rows(xg_ref[:, SUBLANES:ROW_SUBLANES, :])[:, 0:LANES]
        e1 = route[:, ROUTE_E1:ROUTE_E1 + 1]
        e2 = route[:, ROUTE_E2:ROUTE_E2 + 1]
        g1 = route[:, ROUTE_G1:ROUTE_G1 + 1]
        g2 = route[:, ROUTE_G2:ROUTE_G2 + 1]
        first = (group * EXPERTS_PER_GROUP).astype(F32)
        hidden = []
        for e in range(EXPERTS_PER_GROUP):
            hcat = _dot(x, w13_s[e])
            hidden.append((jax.nn.silu(hcat[:, 0:D_EXPERT]) * hcat[:, D_EXPERT:2 * D_EXPERT]).astype(BF16))
        for e in range(EXPERTS_PER_GROUP):
            is1 = e1 == first + e
            is2 = e2 == first + e
            gate = jnp.where(is1, g1, 0.0) + jnp.where(is2, g2, 0.0)
            part = jnp.where(is1 | is2, gate * _dot(hidden[e], w2_s[e]), 0.0)
            if e == 0:
                acc_ref[...] = part
            else:
                acc_ref[...] += part
        y_ref[...] = _rows_to_tiles(acc_ref[...])

    @pl.when(jnp.logical_not(used))
    def _():
        y_ref[...] = jnp.zeros_like(y_ref)


def _experts(blk_group, n_used, xg, w1, w3, w2):
    r = xg.shape[0]
    d = SUBLANES * LANES
    n_blocks = r // MOE_BLOCK
    last_used = lambda i, nu: jnp.minimum(i, nu[0] - 1)
    group_weights = lambda shape: pl.BlockSpec((EXPERTS_PER_GROUP,) + shape, lambda i, bg, nu: (bg[i], 0, 0),
                                               pipeline_mode=pl.Buffered(1))
    return pl.pallas_call(
        _expert_kernel,
        grid_spec=pltpu.PrefetchScalarGridSpec(
            num_scalar_prefetch=2,
            grid=(n_blocks,),
            in_specs=[
                pl.BlockSpec((MOE_BLOCK, ROW_SUBLANES, LANES), lambda i, bg, nu: (last_used(i, nu), 0, 0)),
                group_weights((d, D_EXPERT)),
                group_weights((d, D_EXPERT)),
                group_weights((D_EXPERT, d)),
            ],
            out_specs=pl.BlockSpec((MOE_BLOCK, SUBLANES, LANES), lambda i, bg, nu: (i, 0, 0)),
            scratch_shapes=[pltpu.VMEM((EXPERTS_PER_GROUP, d, 2 * D_EXPERT), BF16),
                            pltpu.VMEM((EXPERTS_PER_GROUP, D_EXPERT, d), BF16),
                            pltpu.VMEM((MOE_BLOCK, d), F32)],
        ),
        out_shape=jax.ShapeDtypeStruct((r, SUBLANES, LANES), F32),
        compiler_params=_params("arbitrary"),
        name="moe_experts",
    )(blk_group, n_used, xg, w1, w3, w2)


def _combine_kernel(code_ref, ps_ref, y_ref, h_ref, fn_ref, o_ref, buf_ref, sem):
    tm = h_ref.shape[0]
    base = pl.program_id(0) * tm

    def start(g, carry):
        r0 = pl.multiple_of(g * ROW_UNROLL, ROW_UNROLL)
        for k in range(ROW_UNROLL):
            _row_copy(y_ref, _slot(code_ref[base + r0 + k], ps_ref), buf_ref, r0 + k, sem).start(priority=k % 2)
        return carry

    def wait(g, carry):
        for _ in range(ROW_UNROLL):
            _row_copy(y_ref, 0, buf_ref, 0, sem).wait()
        return carry

    lax.fori_loop(0, tm // ROW_UNROLL, start, 0)
    lax.fori_loop(0, tm // ROW_UNROLL, wait, 0)
    o_ref[...] = _rms(h_ref[...] + _tiles_to_rows(buf_ref[...]), fn_ref[...])


def _combine(code, p_start, y3, h1, final_norm, tm):
    n, d = h1.shape
    return pl.pallas_call(
        _combine_kernel,
        grid_spec=pltpu.PrefetchScalarGridSpec(
            num_scalar_prefetch=2,
            grid=(n // tm,),
            in_specs=[
                pl.BlockSpec(memory_space=pl.ANY),
                pl.BlockSpec((tm, d), lambda i, *_: (i, 0)),
                pl.BlockSpec((1, d), lambda i, *_: (0, 0)),
            ],
            out_specs=pl.BlockSpec((tm, d), lambda i, *_: (i, 0)),
            scratch_shapes=[pltpu.VMEM((tm, SUBLANES, LANES), F32), pltpu.SemaphoreType.DMA],
        ),
        out_shape=jax.ShapeDtypeStruct((n, d), F32),
        compiler_params=_params("arbitrary"),
        name="moe_combine",
    )(code, p_start, y3, h1, final_norm)


def _prepare_weights(attn_norm, w_in, q_norm, w_uq, kv_norm, w_ukv, b_gate, hg_norm, w_br_mla, w_br_hgrn,
                     w_out, ffn_norm, w_group, b_group, w_route, b_route, final_norm, lb_table):
    d = w_in.shape[0]
    half = MLA_ROPE // 2
    pe0, pe1 = Q_LORA + KV_LORA, Q_LORA + KV_LORA + MLA_ROPE
    zeros = lambda r, c: jnp.zeros((r, c), F32)
    w_lat = w_in[:, :pe0].astype(BF16)
    w_kpe = jnp.concatenate([zeros(d, MLA_NOPE), w_in[:, pe0:pe1], zeros(d, SLOT - MLA_NOPE - MLA_ROPE)],
                            axis=1).astype(BF16)
    w_rest = w_in[:, pe1:].astype(BF16)

    wq3 = w_uq.reshape(Q_LORA, MLA_HEADS, MLA_NOPE + MLA_ROPE)
    zq = jnp.zeros((Q_LORA, MLA_HEADS, SLOT - MLA_NOPE - MLA_ROPE), F32)
    wq = jnp.concatenate([wq3, zq], axis=-1).reshape(Q_LORA, MLA_HEADS * SLOT).astype(BF16)
    q_nope, q_pe = wq3[..., :MLA_NOPE], wq3[..., MLA_NOPE:]
    wq_sw = jnp.concatenate([jnp.zeros_like(q_nope), -q_pe[..., half:], q_pe[..., :half], zq],
                            axis=-1).reshape(Q_LORA, MLA_HEADS * SLOT).astype(BF16)

    wkv3 = w_ukv.reshape(KV_LORA, MLA_HEADS, MLA_NOPE + MLA_V)
    zk = jnp.zeros((KV_LORA, MLA_HEADS, SLOT - MLA_NOPE), F32)
    wk = jnp.concatenate([wkv3[..., :MLA_NOPE], zk], axis=-1).reshape(KV_LORA, MLA_HEADS * SLOT).astype(BF16)
    zv = jnp.zeros((KV_LORA, MLA_HEADS, SLOT - MLA_V), F32)
    wv_t = jnp.concatenate([wkv3[..., MLA_NOPE:], zv], axis=-1).reshape(KV_LORA, MLA_HEADS * SLOT).T.astype(BF16)

    w_router = jnp.concatenate([w_group, w_route, zeros(d, LANES - N_GROUPS - N_EXPERTS)], axis=1)
    wr_hi = w_router.astype(BF16)
    wr_lo = (w_router - wr_hi.astype(F32)).astype(BF16)
    b_router = jnp.concatenate([b_group, b_route, jnp.zeros((LANES - N_GROUPS - N_EXPERTS,), F32)])[None, :]
    return {
        "attn_norm": attn_norm[None, :], "w_lat": w_lat, "w_kpe": w_kpe, "w_rest": w_rest,
        "q_norm": q_norm[None, :], "kv_norm": kv_norm[None, :],
        "wq": wq, "wq_sw": wq_sw, "wk": wk, "wv_t": wv_t, "b_gate": b_gate, "lb_table": lb_table,
        "hg_norm": hg_norm[None, :], "w_br_mla": w_br_mla.astype(BF16), "w_br_hgrn": w_br_hgrn.astype(BF16),
        "w_out": w_out.astype(BF16), "ffn_norm": ffn_norm[None, :], "w_router_hi": wr_hi, "w_router_lo": wr_lo,
        "b_router": b_router, "final_norm": final_norm[None, :],
    }


def _rope_tables(length):
    inv = ROPE_BASE ** (-jnp.arange(0, MLA_ROPE, 2, dtype=F32) / MLA_ROPE)
    ang = jnp.arange(length, dtype=F32)[:, None] * inv[None, :]
    cos, sin = jnp.cos(ang), jnp.sin(ang)
    ones = jnp.ones((length, MLA_NOPE), F32)
    tail = SLOT - MLA_NOPE - MLA_ROPE
    cos_t = jnp.concatenate([ones, cos, cos, jnp.ones((length, tail), F32)], axis=1)
    sin_t = jnp.concatenate([0.0 * ones, sin, sin, jnp.zeros((length, tail), F32)], axis=1)
    return cos_t, sin_t


def kernel(x, meta_tokens, attn_norm, w_in, q_norm, w_uq, kv_norm, w_ukv, lb_table, hg_norm, w_br_mla, w_br_hgrn, b_gate, w_out, ffn_norm, w_group, b_group, w_route, b_route, w1, w3, w2, final_norm):
    batch, seq, d = x.shape
    assert attn_norm.shape[0] == 1, "one layer"
    assert d == SUBLANES * LANES, "a token row is moved as one (8, 128) tile"
    n = batch * seq
    x2 = x.reshape(n, d)
    wts = _prepare_weights(attn_norm[0], w_in[0], q_norm[0], w_uq[0], kv_norm[0], w_ukv[0], b_gate[0],
                           hg_norm[0], w_br_mla[0], w_br_hgrn[0], w_out[0], ffn_norm[0], w_group[0],
                           b_group[0], w_route[0], b_route[0], final_norm, lb_table)
    cos_t, sin_t = _rope_tables(N_META + seq)

    meta = _in_proj(meta_tokens.astype(x.dtype), N_META, 1, cos_t[:N_META], sin_t[:N_META], wts)
    s0 = _hgrn_meta_state(meta[4], meta[5])

    tm = min(ROW_TILE, seq)
    q, k, vt, hq, lf, hi, hg, gm, gh = _in_proj(x2, tm, seq // tm, cos_t[N_META:], sin_t[N_META:], wts)
    o_mla = _attention(q, k, vt, meta[1], meta[2], batch, seq, min(ATTN_TILE, seq))
    o_hg = _hgrn(hq, lf, hi, hg, wts["hg_norm"], s0, batch, seq)
    h1, u3, route, counts = _merge(o_mla, o_hg, gm, gh, x2, wts, tm)

    code = route[:, ROUTE_CODE].astype(I32)
    cnt = counts[0, :N_GROUPS].astype(I32)
    padded = (cnt + MOE_BLOCK - 1) // MOE_BLOCK * MOE_BLOCK
    p_end = jnp.cumsum(padded)
    p_start = p_end - padded
    n_blocks = -(-n // MOE_BLOCK) + N_GROUPS
    blk_row = jnp.arange(n_blocks, dtype=I32) * MOE_BLOCK
    blk_group = jnp.minimum(jnp.sum((p_end[None, :] <= blk_row[:, None]).astype(I32), axis=1), N_GROUPS - 1)
    n_used = p_end[-1:] // MOE_BLOCK

    tmove = min(MOVE_TILE, seq)
    xg = _dispatch(code, p_start, p_start + cnt, p_end, n_used, u3, n_blocks * MOE_BLOCK, tmove)
    y3 = _experts(blk_group, n_used, xg, w1[0], w3[0], w2[0])
    out = _combine(code, p_start, y3, h1, wts["final_norm"], tmove)
    return out.reshape(batch, seq, d)
```

```python
import jax
import jax.numpy as jnp
from jax import lax
from jax.experimental import pallas as pl
from jax.experimental.pallas import tpu as pltpu

F32 = jnp.float32
BF16 = jnp.bfloat16
I32 = jnp.int32

N_META = 16
EPS = 1e-6
MLA_HEADS = 8
MLA_NOPE = 64
MLA_ROPE = 32
MLA_V = 64
Q_LORA = 384
KV_LORA = 256
ROPE_BASE = 10000.0
HG_HEADS = 4
HG_DK = 128
HG_DV = 128
N_GROUPS = 8
EXPERTS_PER_GROUP = 8
N_EXPERTS = N_GROUPS * EXPERTS_PER_GROUP
D_EXPERT = 256
MOE_BLOCK = 256

LANES = 128
SUBLANES = 8
VMEM_LIMIT_BYTES = 56 * 1024 * 1024

SLOT = LANES
ONE_LANE = MLA_V
ROW_TILE = 512
ATTN_TILE = 512
ATTN_HEAD_SET = 4
MOVE_TILE = 512
HG_CHUNK = 64
HG_SUB = 8
BAND_GROUP = 8
HEAD_SET = HG_HEADS
LOG2_E = 1.4426950408889634

SEG_CQ = (0, 384)
SEG_CKV = (384, 640)
SEG_HQ = (0, 512)
SEG_HF = (512, 1024)
SEG_HI = (1024, 1536)
SEG_HG = (1536, 2048)
SEG_GM = (2048, 3072)
SEG_GH = (3072, 4096)


def _params(*sem):
    return pltpu.CompilerParams(dimension_semantics=sem, vmem_limit_bytes=VMEM_LIMIT_BYTES)


def _rms(x, g):
    return x * lax.rsqrt(jnp.mean(x * x, axis=-1, keepdims=True) + EPS) * g


def _dot(a, b):
    return jnp.dot(a, b, preferred_element_type=F32)


def _dot_nt(a, b):
    return lax.dot_general(a, b, (((1,), (1,)), ((), ())), preferred_element_type=F32)


def _dot_tn(a, b):
    return lax.dot_general(a, b, (((0,), (0,)), ((), ())), preferred_element_type=F32)


def _split2(x):
    hi = x.astype(BF16)
    lo = (x - hi.astype(F32)).astype(BF16)
    return hi, lo


def _split3(x):
    hi = x.astype(BF16)
    r = x - hi.astype(F32)
    mid = r.astype(BF16)
    lo = (r - mid.astype(F32)).astype(BF16)
    return hi, mid, lo


def _const_spec(shape):
    nd = len(shape)
    return pl.BlockSpec(shape, lambda *_: (0,) * nd, pipeline_mode=pl.Buffered(1))


def _rows_to_tiles(x):
    return x.reshape(x.shape[0], SUBLANES, LANES)


def _tiles_to_rows(x3):
    return x3.reshape(x3.shape[0], SUBLANES * LANES)


def _lower_bound(lbt_ref):
    t0 = lbt_ref[0:1, :]
    t1 = lbt_ref[1:2, :]
    mx = jnp.maximum(t0, t1)
    e0 = jnp.exp(t0 - mx)
    return e0 / (e0 + jnp.exp(t1 - mx))


def _in_proj_kernel(x_ref, g_ref, wl_ref, wp_ref, wr_ref, qn_ref, kvn_ref, wq_ref, wqs_ref, wk_ref, wvt_ref,
                    cos_ref, sin_ref, bg_ref, lbt_ref,
                    q_out, k_out, vt_out, hq_out, lf_out, hi_out, hg_out, gm_out, gh_out):
    u = _rms(x_ref[...], g_ref[...]).astype(BF16)

    def lat(s):
        return _dot(u, wl_ref[:, s[0]:s[1]])

    def seg(s):
        return _dot(u, wr_ref[:, s[0]:s[1]])

    cos = cos_ref[...]
    sin = sin_ref[...]
    cos_t = jnp.tile(cos, (1, MLA_HEADS))
    sin_t = jnp.tile(sin, (1, MLA_HEADS))
    scale = (MLA_NOPE + MLA_ROPE) ** -0.5 * LOG2_E

    cqn = _rms(lat(SEG_CQ), qn_ref[...]).astype(BF16)
    q = _dot(cqn, wq_ref[...]) * cos_t + _dot(cqn, wqs_ref[...]) * sin_t
    q_out[...] = (q * scale).astype(BF16)

    ckvn = _rms(lat(SEG_CKV), kvn_ref[...]).astype(BF16)
    k_both = _dot(u, wp_ref[...])
    k_slot = k_both[:, 0:SLOT] * cos + k_both[:, SLOT:2 * SLOT] * sin
    k_out[...] = (_dot(ckvn, wk_ref[...]) + jnp.tile(k_slot, (1, MLA_HEADS))).astype(BF16)
    vt = _dot_nt(wvt_ref[...], ckvn)
    slot_row = lax.broadcasted_iota(I32, vt.shape, 0) & (SLOT - 1)
    vt = jnp.where(slot_row == ONE_LANE, 1.0, vt).astype(BF16)
    tv = vt_out.shape[2]
    for piece in range(vt_out.shape[0]):
        vt_out[piece] = vt[:, piece * tv:(piece + 1) * tv]

    lb = _lower_bound(lbt_ref)
    hq_out[...] = jax.nn.silu(seg(SEG_HQ)).astype(BF16)
    lf_out[...] = jnp.log(lb + (1.0 - lb) * jax.nn.sigmoid(seg(SEG_HF))) * LOG2_E
    hi_out[...] = seg(SEG_HI).astype(BF16)
    hg_out[...] = jax.nn.silu(seg(SEG_HG)).astype(BF16)
    gm_out[...] = jax.nn.sigmoid(seg(SEG_GM) + bg_ref[0:1, :]).astype(BF16)
    gh_out[...] = jax.nn.sigmoid(seg(SEG_GH) + bg_ref[1:2, :]).astype(BF16)


def _in_proj(x2, tm, tiles_per_seq, cos, sin, wts):
    n, d = x2.shape
    wide = MLA_HEADS * SLOT
    hw = HG_HEADS * HG_DK
    row = lambda width: pl.BlockSpec((tm, width), lambda i: (i, 0))
    tab = pl.BlockSpec((tm, SLOT), lambda i: (i % tiles_per_seq, 0))
    consts = (wts["attn_norm"], wts["w_lat"], wts["w_kpe"], wts["w_rest"], wts["q_norm"], wts["kv_norm"],
              wts["wq"], wts["wq_sw"], wts["wk"], wts["wv_t"])
    tail = (wts["b_gate"], wts["lb_table"])
    in_specs = ([row(d)] + [_const_spec(c.shape) for c in consts] + [tab, tab]
                + [_const_spec(c.shape) for c in tail])
    widths = (wide, wide, None, hw, hw, hw, hw, d, d)
    dtypes = (BF16, BF16, BF16, BF16, F32, BF16, BF16, BF16, BF16)
    tv = min(tm, ATTN_TILE)
    vt_spec = pl.BlockSpec((tm // tv, wide, tv), lambda i: (i, 0, 0))
    return pl.pallas_call(
        _in_proj_kernel,
        grid=(n // tm,),
        in_specs=in_specs,
        out_specs=[vt_spec if w is None else row(w) for w in widths],
        out_shape=[jax.ShapeDtypeStruct((n // tv, wide, tv) if w is None else (n, w), t)
                   for w, t in zip(widths, dtypes)],
        compiler_params=_params("parallel"),
        name="in_proj",
    )(x2, *consts, cos, sin, *tail)


def _attn_kernel(q_ref, k_ref, vt_ref, km_ref, vtm_ref, o_ref, m_ref, acc_ref):
    i = pl.program_id(1)
    tq = q_ref.shape[0]

    slots = [slice(h * SLOT, (h + 1) * SLOT) for h in range(MLA_HEADS)]

    def head_set(hs, k_of, vt_of, mask, first):
        s = [_dot_nt(k_of(slots[h]), q_ref[:, slots[h]]) for h in hs]
        if mask is not None:
            s = [jnp.where(mask, x, -jnp.inf) for x in s]
        mx = [jnp.max(x, axis=0, keepdims=True) for x in s]
        if first:
            m_new = mx
        else:
            m_old = [m_ref[h] for h in hs]
            m_new = [jnp.maximum(a, b) for a, b in zip(m_old, mx)]
        p = [jnp.exp2(x - m).astype(BF16) for x, m in zip(s, m_new)]
        for n, h in enumerate(hs):
            m_ref[h] = m_new[n]
            pv = _dot(vt_of(slots[h]), p[n])
            acc_ref[h] = pv if first else jnp.exp2(m_old[n] - m_new[n]) * acc_ref[h] + pv

    def tile(k_of, vt_of, mask, first=False):
        for h0 in range(0, MLA_HEADS, ATTN_HEAD_SET):
            head_set(range(h0, h0 + ATTN_HEAD_SET), k_of, vt_of, mask, first)

    tile(lambda sl: km_ref[:, sl], lambda sl: vtm_ref[0, sl, :], None, first=True)

    def x_tile(j, mask):
        rows = pl.ds(pl.multiple_of(j * tq, tq), tq)
        tile(lambda sl: k_ref[rows, sl], lambda sl: vt_ref[j, sl, :], mask)

    def body(j, carry):
        x_tile(j, None)
        return carry

    lax.fori_loop(0, i, body, 0)
    x_tile(i, lax.broadcasted_iota(I32, (tq, tq), 0) <= lax.broadcasted_iota(I32, (tq, tq), 1))

    outs = []
    for h in range(MLA_HEADS):
        acc = acc_ref[h]
        outs.append(acc[:MLA_V, :] / acc[ONE_LANE:ONE_LANE + 1, :])
    o_ref[...] = jnp.concatenate(outs, axis=0).T.astype(BF16)


def _attention(q, k, vt, k_meta, vt_meta, batch, seq, tq):
    wide = MLA_HEADS * SLOT
    nq = seq // tq
    assert vt.shape == (batch * nq, wide, tq)
    return pl.pallas_call(
        _attn_kernel,
        grid=(batch, nq),
        in_specs=[
            pl.BlockSpec((tq, wide), lambda b, i: (b * nq + i, 0)),
            pl.BlockSpec((seq, wide), lambda b, i: (b, 0)),
            pl.BlockSpec((nq, wide, tq), lambda b, i: (b, 0, 0)),
            _const_spec(k_meta.shape),
            _const_spec(vt_meta.shape),
        ],
        out_specs=pl.BlockSpec((tq, MLA_HEADS * MLA_V), lambda b, i: (b * nq + i, 0)),
        out_shape=jax.ShapeDtypeStruct((batch * seq, MLA_HEADS * MLA_V), BF16),
        scratch_shapes=[pltpu.VMEM((MLA_HEADS, 1, tq), F32), pltpu.VMEM((MLA_HEADS, SLOT, tq), F32)],
        compiler_params=_params("parallel", "arbitrary"),
        name="mla_attention",
    )(q, k, vt, k_meta, vt_meta)


def _cumsum_rows(tril, lf):
    parts = _split3(lf)
    return _dot(tril, parts[0]) + _dot(tril, parts[1]) + _dot(tril, parts[2])


def _tril(n):
    return (lax.broadcasted_iota(I32, (n, n), 0) >= lax.broadcasted_iota(I32, (n, n), 1)).astype(BF16)


def _hgrn_meta_kernel(lf_ref, hi_ref, s_out):
    tril = _tril(lf_ref.shape[0])
    for h in range(HG_HEADS):
        sl = slice(h * HG_DK, (h + 1) * HG_DK)
        lf = lf_ref[:, sl]
        b = _cumsum_rows(tril, lf)
        kdec = (1.0 - jnp.exp2(lf)) * jnp.exp2(b[-1:, :] - b)
        s_out[h] = _dot_tn(hi_ref[:, sl], kdec.astype(BF16))


def _hgrn_meta_state(lf, hi):
    return pl.pallas_call(
        _hgrn_meta_kernel,
        out_shape=jax.ShapeDtypeStruct((HG_HEADS, HG_DV, HG_DK), F32),
        compiler_params=pltpu.CompilerParams(vmem_limit_bytes=VMEM_LIMIT_BYTES),
        name="hgrn_meta_state",
    )(lf, hi)


def _hgrn_kernel(hq_ref, lf_ref, hi_ref, hg_ref, hgn_ref, s0_ref, o_ref,
                 st_ref, pb_ref, pk_ref, pv_ref):
    c_rows = HG_CHUNK
    n_chunks = hq_ref.shape[0] // c_rows
    st_ref[...] = s0_ref[...]
    pad = jnp.zeros((HG_HEADS, HG_SUB, HG_DK), F32)
    pb_ref[:, 0:HG_SUB, :] = pad
    pk_ref[:, 0:HG_SUB, :] = pad
    pv_ref[:, 0:HG_SUB, :] = pad
    hgn = hgn_ref[...]
    tril = _tril(c_rows)
    ones = jnp.ones((HG_DK, LANES), BF16)
    n_sub = c_rows // HG_SUB
    far = (lax.broadcasted_iota(I32, (c_rows, c_rows), 0) - lax.broadcasted_iota(I32, (c_rows, c_rows), 1)) >= HG_SUB
    heads = range(HG_HEADS)
    cols = [slice(h * HG_DK, (h + 1) * HG_DK) for h in heads]
    stash = slice(HG_SUB, HG_SUB + c_rows)

    def stages(rows, hs):
        lf = {h: lf_ref[rows, cols[h]] for h in hs}
        kk = {h: 1.0 - jnp.exp2(lf[h]) for h in hs}
        b = {h: _cumsum_rows(tril, lf[h]) for h in hs}
        q = {h: hq_ref[rows, cols[h]].astype(F32) for h in hs}
        v_bf = {h: hi_ref[rows, cols[h]] for h in hs}
        st = {h: st_ref[h] for h in hs}
        for h in hs:
            pb_ref[h, stash, :] = b[h]
            pk_ref[h, stash, :] = kk[h]
            pv_ref[h, stash, :] = v_bf[h].astype(F32)

        y = {}
        for h in hs:
            xs = [(q[h] * kk[h]).astype(BF16)]
            for d in range(1, HG_SUB):
                shifted = slice(HG_SUB - d, HG_SUB - d + c_rows)
                xs.append((q[h] * pk_ref[h, shifted, :] * jnp.exp2(b[h] - pb_ref[h, shifted, :])).astype(BF16))
            y[h] = [_dot(jnp.concatenate(xs[d0:d0 + BAND_GROUP], axis=0), ones)
                    for d0 in range(0, HG_SUB, BAND_GROUP)]

        o = {h: _dot_nt((q[h] * jnp.exp2(b[h])).astype(BF16), st[h].astype(BF16)) for h in hs}

        a_off = {}
        zero_rows = lambda r: [jnp.zeros((r, HG_DK), BF16)] if r else []
        for h in hs:
            qps, kps = [], []
            for j in range(n_sub - 1):
                k0, k1 = HG_SUB * j, HG_SUB * (j + 1)
                rj = b[h][k1 - 1:k1, :]
                qp = (q[h][k1:, :] * jnp.exp2(b[h][k1:, :] - rj)).astype(BF16)
                kp = (kk[h][k0:k1, :] * jnp.exp2(rj - b[h][k0:k1, :])).astype(BF16)
                qps.append(jnp.concatenate(zero_rows(k1) + [qp], axis=0))
                kps.append(jnp.concatenate(zero_rows(k0) + [kp] + zero_rows(c_rows - k1), axis=0))
            a = _dot_nt(jnp.concatenate(qps, axis=1), jnp.concatenate(kps, axis=1))
            a_off[h] = jnp.where(far, a, 0.0).astype(BF16)

        for h in hs:
            o[h] = o[h] + _dot(a_off[h], v_bf[h])
            for d in range(HG_SUB):
                shifted = slice(HG_SUB - d, HG_SUB - d + c_rows)
                part = y[h][d // BAND_GROUP]
                r0 = (d % BAND_GROUP) * c_rows
                o[h] = o[h] + part[r0:r0 + c_rows, :] * pv_ref[h, shifted, :]

        for h in hs:
            on = o[h] * lax.rsqrt(jnp.mean(o[h] * o[h], axis=-1, keepdims=True) + EPS)
            on = on * hgn[:, cols[h]] * hg_ref[rows, cols[h]].astype(F32)
            o_ref[rows, cols[h]] = on.astype(BF16)

        for h in hs:
            b_end = b[h][c_rows - 1:c_rows, :]
            kdec = kk[h] * jnp.exp2(b_end - b[h])
            st_ref[h] = st[h] * jnp.exp2(b_end) + _dot_tn(v_bf[h], kdec.astype(BF16))

    def chunk(c, carry):
        rows = pl.ds(pl.multiple_of(c * c_rows, c_rows), c_rows)
        for h0 in range(0, HG_HEADS, HEAD_SET):
            stages(rows, tuple(range(h0, h0 + HEAD_SET)))
        return carry

    lax.fori_loop(0, n_chunks, chunk, 0)


def _hgrn(hq, lf, hi, hg, hg_norm, s0, batch, seq):
    hw = HG_HEADS * HG_DK
    seq_spec = pl.BlockSpec((seq, hw), lambda b: (b, 0))
    pad_rows = HG_SUB + HG_CHUNK
    return pl.pallas_call(
        _hgrn_kernel,
        grid=(batch,),
        in_specs=[seq_spec, seq_spec, seq_spec, seq_spec, _const_spec(hg_norm.shape), _const_spec(s0.shape)],
        out_specs=seq_spec,
        out_shape=jax.ShapeDtypeStruct((batch * seq, hw), BF16),
        scratch_shapes=[pltpu.VMEM((HG_HEADS, HG_DV, HG_DK), F32)]
        + [pltpu.VMEM((HG_HEADS, pad_rows, HG_DK), F32)] * 3,
        compiler_params=_params("parallel"),
        name="hgrn2",
    )(hq, lf, hi, hg, hg_norm, s0)


ROUTE_CODE, ROUTE_E1, ROUTE_E2, ROUTE_G1, ROUTE_G2 = range(5)
RANK_BITS = 16
ROW_SUBLANES = 2 * SUBLANES


def _slot(code, pstart_ref):
    return pstart_ref[code >> RANK_BITS] + (code & ((1 << RANK_BITS) - 1))


def _merge_kernel(om_ref, oh_ref, gm_ref, gh_ref, x_ref, wbm_ref, wbh_ref, wo_ref, fn_ref,
                  wr_hi_ref, wr_lo_ref, br_ref, h_out, u_out, route_out, cnt_out, carry_ref):
    i = pl.program_id(0)
    tm, d = x_ref.shape

    @pl.when(i == 0)
    def _():
        carry_ref[...] = jnp.zeros_like(carry_ref)

    a = _dot(om_ref[...], wbm_ref[...])
    g = _dot(oh_ref[...], wbh_ref[...])
    merged = gm_ref[...].astype(F32) * a + gh_ref[...].astype(F32) * g
    h1 = x_ref[...] + _dot(merged.astype(BF16), wo_ref[...])
    h_out[...] = h1
    u = _rms(h1, fn_ref[...])
    u_out[:, 0:SUBLANES, :] = _rows_to_tiles(u)

    u_hi, u_lo = _split2(u)
    logits = (_dot(u_hi, wr_hi_ref[...]) + _dot(u_hi, wr_lo_ref[...]) + _dot(u_lo, wr_hi_ref[...])
              + br_ref[...])
    lane = lax.broadcasted_iota(I32, (tm, LANES), 1)
    lane_f = lane.astype(F32)
    big = float(2 * LANES)

    def first_max(vals):
        mx = jnp.max(vals, axis=-1, keepdims=True)
        idx = jnp.min(jnp.where(vals == mx, lane_f, big), axis=-1, keepdims=True)
        return mx, idx

    gl = jnp.where(lane < N_GROUPS, logits, -jnp.inf)
    g_max, g_sel = first_max(gl)
    p_sel = 1.0 / jnp.sum(jnp.exp(gl - g_max), axis=-1, keepdims=True)
    lo = N_GROUPS + g_sel * EXPERTS_PER_GROUP
    el = jnp.where((lane_f >= lo) & (lane_f < lo + EXPERTS_PER_GROUP), logits, -jnp.inf)
    v1, i1 = first_max(el)
    el2 = jnp.where(lane_f == i1, -jnp.inf, el)
    v2, i2 = first_max(el2)
    t = jnp.exp(v2 - v1)
    g1 = p_sel * (1.0 / (1.0 + t))
    g2 = p_sel * (t / (1.0 + t))

    hit = lane_f == g_sel
    onehot = jnp.where(hit, 1.0, 0.0)
    strict = (lax.broadcasted_iota(I32, (tm, tm), 0) > lax.broadcasted_iota(I32, (tm, tm), 1)).astype(BF16)
    before = _dot(strict, onehot.astype(BF16)) + carry_ref[0:1, :]
    rank = jnp.sum(jnp.where(hit, before, 0.0), axis=-1, keepdims=True)
    total = carry_ref[0:1, :] + jnp.sum(onehot, axis=0, keepdims=True)
    carry_ref[...] = jnp.broadcast_to(total, carry_ref.shape)
    cnt_out[...] = jnp.broadcast_to(total, cnt_out.shape)

    route = jnp.zeros((tm, LANES), F32)
    code = g_sel * float(1 << RANK_BITS) + rank
    for pos, val in ((ROUTE_CODE, code), (ROUTE_E1, i1 - N_GROUPS), (ROUTE_E2, i2 - N_GROUPS),
                     (ROUTE_G1, g1), (ROUTE_G2, g2)):
        route = jnp.where(lane == pos, val, route)
    route_out[...] = route
    record = jnp.concatenate([route, jnp.zeros((tm, d - LANES), F32)], axis=1)
    u_out[:, SUBLANES:ROW_SUBLANES, :] = _rows_to_tiles(record)


def _merge(o_mla, o_hg, gm, gh, x2, wts, tm):
    n, d = x2.shape
    row = lambda width: pl.BlockSpec((tm, width), lambda i: (i, 0))
    consts = (wts["w_br_mla"], wts["w_br_hgrn"], wts["w_out"], wts["ffn_norm"],
              wts["w_router_hi"], wts["w_router_lo"], wts["b_router"])
    return pl.pallas_call(
        _merge_kernel,
        grid=(n // tm,),
        in_specs=[row(o_mla.shape[1]), row(o_hg.shape[1]), row(d), row(d), row(d)]
        + [_const_spec(c.shape) for c in consts],
        out_specs=[row(d), pl.BlockSpec((tm, ROW_SUBLANES, LANES), lambda i: (i, 0, 0)), row(LANES),
                   pl.BlockSpec((8, LANES), lambda i: (0, 0))],
        out_shape=[jax.ShapeDtypeStruct((n, d), F32), jax.ShapeDtypeStruct((n, ROW_SUBLANES, LANES), F32),
                   jax.ShapeDtypeStruct((n, LANES), F32), jax.ShapeDtypeStruct((8, LANES), F32)],
        scratch_shapes=[pltpu.VMEM((8, LANES), F32)],
        compiler_params=_params("arbitrary"),
        name="merge_route",
    )(o_mla, o_hg, gm, gh, x2, *consts)


ROW_UNROLL = 8


def _row_copy(src_ref, src_row, dst_ref, dst_row, sem):
    return pltpu.make_async_copy(src_ref.at[pl.ds(src_row, 1)], dst_ref.at[pl.ds(dst_row, 1)], sem)


def _dispatch_kernel(code_ref, ps_ref, pad_lo_ref, pad_hi_ref, nu_ref, u_ref, xg_out, zero_ref, sem, pad_sem):
    i = pl.program_id(0)
    tm = u_ref.shape[0]
    base = i * tm

    @pl.when(i == 0)
    def _():
        zero_ref[...] = jnp.zeros_like(zero_ref)
        for g in range(N_GROUPS):
            def fill(r, carry):
                _row_copy(zero_ref, 0, xg_out, r, pad_sem).start()
                return carry

            def drain(r, carry):
                _row_copy(zero_ref, 0, xg_out, 0, pad_sem).wait()
                return carry

            lax.fori_loop(pad_lo_ref[g], pad_hi_ref[g], fill, 0)
            lax.fori_loop(pad_lo_ref[g], pad_hi_ref[g], drain, 0)

        def block_copy(b):
            rows = pl.ds(pl.multiple_of(b * MOE_BLOCK, MOE_BLOCK), MOE_BLOCK)
            return pltpu.make_async_copy(zero_ref, xg_out.at[rows], pad_sem)

        def fill_block(b, carry):
            block_copy(b).start()
            return carry

        def drain_block(b, carry):
            block_copy(b).wait()
            return carry

        n_blocks = xg_out.shape[0] // MOE_BLOCK
        lax.fori_loop(nu_ref[0], n_blocks, fill_block, 0)
        lax.fori_loop(nu_ref[0], n_blocks, drain_block, 0)

    def start(g, carry):
        r0 = pl.multiple_of(g * ROW_UNROLL, ROW_UNROLL)
        for k in range(ROW_UNROLL):
            _row_copy(u_ref, r0 + k, xg_out, _slot(code_ref[base + r0 + k], ps_ref), sem).start(priority=k % 2)
        return carry

    def wait(g, carry):
        for _ in range(ROW_UNROLL):
            _row_copy(u_ref, 0, xg_out, 0, sem).wait()
        return carry

    lax.fori_loop(0, tm // ROW_UNROLL, start, 0)
    lax.fori_loop(0, tm // ROW_UNROLL, wait, 0)


def _dispatch(code, p_start, pad_lo, pad_hi, n_used, u3, n_slots, tm):
    n = u3.shape[0]
    tile = u3.shape[1:]
    return pl.pallas_call(
        _dispatch_kernel,
        grid_spec=pltpu.PrefetchScalarGridSpec(
            num_scalar_prefetch=5,
            grid=(n // tm,),
            in_specs=[pl.BlockSpec((tm,) + tile, lambda i, *_: (i, 0, 0))],
            out_specs=pl.BlockSpec(memory_space=pl.ANY),
            scratch_shapes=[pltpu.VMEM((MOE_BLOCK,) + tile, F32), pltpu.SemaphoreType.DMA,
                            pltpu.SemaphoreType.DMA],
        ),
        out_shape=jax.ShapeDtypeStruct((n_slots,) + tile, F32),
        compiler_params=_params("arbitrary"),
        name="moe_dispatch",
    )(code, p_start, pad_lo, pad_hi, n_used, u3)


CUR_SET, N_READY = 0, 1


def _expert_kernel(bg_ref, ng_ref, nu_ref, xg_ref, w1_hbm, w3_hbm, w2_hbm, y_ref,
                   w13_s, w2_s, acc_ref, st1, st3, st2, state, sem):
    i = pl.program_id(0)
    used = i < nu_ref[0]

    def fetch(expert):
        return (pltpu.make_async_copy(w1_hbm.at[expert], st1, sem),
                pltpu.make_async_copy(w3_hbm.at[expert], st3, sem),
                pltpu.make_async_copy(w2_hbm.at[expert], st2, sem))

    def fetch_start(expert):
        for c in fetch(expert):
            c.start()

    def fetch_finish(expert, wset, e):
        for c in fetch(expert):
            c.wait()
        w13_s[wset, e, :, 0:D_EXPERT] = st1[...].astype(BF16)
        w13_s[wset, e, :, D_EXPERT:2 * D_EXPERT] = st3[...].astype(BF16)
        w2_s[wset, e] = st2[...].astype(BF16)

    @pl.when(used)
    def _():
        group = bg_ref[i]

        @pl.when(i == 0)
        def _():
            state[CUR_SET] = 1
            state[N_READY] = 0

        @pl.when((i == 0) | (group != bg_ref[jnp.maximum(i - 1, 0)]))
        def _():
            wset = 1 - state[CUR_SET]
            ready = state[N_READY]

            def load(e, carry):
                fetch_start(group * EXPERTS_PER_GROUP + e)
                fetch_finish(group * EXPERTS_PER_GROUP + e, wset, e)
                return carry

            lax.fori_loop(ready, EXPERTS_PER_GROUP, load, 0)
            state[CUR_SET] = wset
            state[N_READY] = 0

        cur = state[CUR_SET]
        ahead = state[N_READY]
        next_group = ng_ref[i]
        prefetch = (next_group >= 0) & (ahead < EXPERTS_PER_GROUP)
        next_expert = jnp.maximum(next_group, 0) * EXPERTS_PER_GROUP + jnp.minimum(ahead, EXPERTS_PER_GROUP - 1)

        @pl.when(prefetch)
        def _():
            fetch_start(next_expert)

        x = _tiles_to_rows(xg_ref[:, 0:SUBLANES, :]).astype(BF16)
        route = _tiles_to_rows(xg_ref[:, SUBLANES:ROW_SUBLANES, :])[:, 0:LANES]
        e1 = route[:, ROUTE_E1:ROUTE_E1 + 1]
        e2 = route[:, ROUTE_E2:ROUTE_E2 + 1]
        g1 = route[:, ROUTE_G1:ROUTE_G1 + 1]
        g2 = route[:, ROUTE_G2:ROUTE_G2 + 1]
        first = (group * EXPERTS_PER_GROUP).astype(F32)
        hidden = []
        for e in range(EXPERTS_PER_GROUP):
            hcat = _dot(x, w13_s[cur, e])
            hidden.append((jax.nn.silu(hcat[:, 0:D_EXPERT]) * hcat[:, D_EXPERT:2 * D_EXPERT]).astype(BF16))
        for e in range(EXPERTS_PER_GROUP):
            is1 = e1 == first + e
            is2 = e2 == first + e
            gate = jnp.where(is1, g1, 0.0) + jnp.where(is2, g2, 0.0)
            part = jnp.where(is1 | is2, gate * _dot(hidden[e], w2_s[cur, e]), 0.0)
            if e == 0:
                acc_ref[...] = part
            else:
                acc_ref[...] += part
        y_ref[...] = _rows_to_tiles(acc_ref[...])

        @pl.when(prefetch)
        def _():
            fetch_finish(next_expert, 1 - cur, ahead)
            state[N_READY] = ahead + 1

    @pl.when(jnp.logical_not(used))
    def _():
        y_ref[...] = jnp.zeros_like(y_ref)


def _experts(blk_group, next_group, n_used, xg, w1, w3, w2):
    r = xg.shape[0]
    d = SUBLANES * LANES
    n_blocks = r // MOE_BLOCK
    last_used = lambda i, nu: jnp.minimum(i, nu[0] - 1)
    return pl.pallas_call(
        _expert_kernel,
        grid_spec=pltpu.PrefetchScalarGridSpec(
            num_scalar_prefetch=3,
            grid=(n_blocks,),
            in_specs=[
                pl.BlockSpec((MOE_BLOCK, ROW_SUBLANES, LANES), lambda i, bg, ng, nu: (last_used(i, nu), 0, 0)),
                pl.BlockSpec(memory_space=pl.ANY),
                pl.BlockSpec(memory_space=pl.ANY),
                pl.BlockSpec(memory_space=pl.ANY),
            ],
            out_specs=pl.BlockSpec((MOE_BLOCK, SUBLANES, LANES), lambda i, bg, ng, nu: (i, 0, 0)),
            scratch_shapes=[pltpu.VMEM((2, EXPERTS_PER_GROUP, d, 2 * D_EXPERT), BF16),
                            pltpu.VMEM((2, EXPERTS_PER_GROUP, D_EXPERT, d), BF16),
                            pltpu.VMEM((MOE_BLOCK, d), F32),
                            pltpu.VMEM((d, D_EXPERT), F32), pltpu.VMEM((d, D_EXPERT), F32),
                            pltpu.VMEM((D_EXPERT, d), F32),
                            pltpu.SMEM((2,), I32), pltpu.SemaphoreType.DMA],
        ),
        out_shape=jax.ShapeDtypeStruct((r, SUBLANES, LANES), F32),
        compiler_params=_params("arbitrary"),
        name="moe_experts",
    )(blk_group, next_group, n_used, xg, w1, w3, w2)


def _combine_kernel(code_ref, ps_ref, y_ref, h_ref, fn_ref, o_ref, buf_ref, sem):
    tm = h_ref.shape[0]
    base = pl.program_id(0) * tm

    def start(g, carry):
        r0 = pl.multiple_of(g * ROW_UNROLL, ROW_UNROLL)
        for k in range(ROW_UNROLL):
            _row_copy(y_ref, _slot(code_ref[base + r0 + k], ps_ref), buf_ref, r0 + k, sem).start(priority=k % 2)
        return carry

    def wait(g, carry):
        for _ in range(ROW_UNROLL):
            _row_copy(y_ref, 0, buf_ref, 0, sem).wait()
        return carry

    lax.fori_loop(0, tm // ROW_UNROLL, start, 0)
    lax.fori_loop(0, tm // ROW_UNROLL, wait, 0)
    o_ref[...] = _rms(h_ref[...] + _tiles_to_rows(buf_ref[...]), fn_ref[...])


def _combine(code, p_start, y3, h1, final_norm, tm):
    n, d = h1.shape
    return pl.pallas_call(
        _combine_kernel,
        grid_spec=pltpu.PrefetchScalarGridSpec(
            num_scalar_prefetch=2,
            grid=(n // tm,),
            in_specs=[
                pl.BlockSpec(memory_space=pl.ANY),
                pl.BlockSpec((tm, d), lambda i, *_: (i, 0)),
                pl.BlockSpec((1, d), lambda i, *_: (0, 0)),
            ],
            out_specs=pl.BlockSpec((tm, d), lambda i, *_: (i, 0)),
            scratch_shapes=[pltpu.VMEM((tm, SUBLANES, LANES), F32), pltpu.SemaphoreType.DMA],
        ),
        out_shape=jax.ShapeDtypeStruct((n, d), F32),
        compiler_params=_params("arbitrary"),
        name="moe_combine",
    )(code, p_start, y3, h1, final_norm)


def _prepare_weights(attn_norm, w_in, q_norm, w_uq, kv_norm, w_ukv, b_gate, hg_norm, w_br_mla, w_br_hgrn,
                     w_out, ffn_norm, w_group, b_group, w_route, b_route, final_norm, lb_table):
    d = w_in.shape[0]
    half = MLA_ROPE // 2
    pe0, pe1 = Q_LORA + KV_LORA, Q_LORA + KV_LORA + MLA_ROPE
    zeros = lambda r, c: jnp.zeros((r, c), F32)
    w_lat = w_in[:, :pe0].astype(BF16)
    w_pe = w_in[:, pe0:pe1]
    tail = zeros(d, SLOT - MLA_NOPE - MLA_ROPE)
    w_kpe = jnp.concatenate([zeros(d, MLA_NOPE), w_pe, tail,
                             zeros(d, MLA_NOPE), -w_pe[:, half:], w_pe[:, :half], tail], axis=1).astype(BF16)
    w_rest = w_in[:, pe1:].astype(BF16)

    wq3 = w_uq.reshape(Q_LORA, MLA_HEADS, MLA_NOPE + MLA_ROPE)
    zq = jnp.zeros((Q_LORA, MLA_HEADS, SLOT - MLA_NOPE - MLA_ROPE), F32)
    wq = jnp.concatenate([wq3, zq], axis=-1).reshape(Q_LORA, MLA_HEADS * SLOT).astype(BF16)
    q_nope, q_pe = wq3[..., :MLA_NOPE], wq3[..., MLA_NOPE:]
    wq_sw = jnp.concatenate([jnp.zeros_like(q_nope), -q_pe[..., half:], q_pe[..., :half], zq],
                            axis=-1).reshape(Q_LORA, MLA_HEADS * SLOT).astype(BF16)

    wkv3 = w_ukv.reshape(KV_LORA, MLA_HEADS, MLA_NOPE + MLA_V)
    zk = jnp.zeros((KV_LORA, MLA_HEADS, SLOT - MLA_NOPE), F32)
    wk = jnp.concatenate([wkv3[..., :MLA_NOPE], zk], axis=-1).reshape(KV_LORA, MLA_HEADS * SLOT).astype(BF16)
    zv = jnp.zeros((KV_LORA, MLA_HEADS, SLOT - MLA_V), F32)
    wv_t = jnp.concatenate([wkv3[..., MLA_NOPE:], zv], axis=-1).reshape(KV_LORA, MLA_HEADS * SLOT).T.astype(BF16)

    w_router = jnp.concatenate([w_group, w_route, zeros(d, LANES - N_GROUPS - N_EXPERTS)], axis=1)
    wr_hi = w_router.astype(BF16)
    wr_lo = (w_router - wr_hi.astype(F32)).astype(BF16)
    b_router = jnp.concatenate([b_group, b_route, jnp.zeros((LANES - N_GROUPS - N_EXPERTS,), F32)])[None, :]
    return {
        "attn_norm": attn_norm[None, :], "w_lat": w_lat, "w_kpe": w_kpe, "w_rest": w_rest,
        "q_norm": q_norm[None, :], "kv_norm": kv_norm[None, :],
        "wq": wq, "wq_sw": wq_sw, "wk": wk, "wv_t": wv_t, "b_gate": b_gate, "lb_table": lb_table,
        "hg_norm": hg_norm[None, :], "w_br_mla": w_br_mla.astype(BF16), "w_br_hgrn": w_br_hgrn.astype(BF16),
        "w_out": w_out.astype(BF16), "ffn_norm": ffn_norm[None, :], "w_router_hi": wr_hi, "w_router_lo": wr_lo,
        "b_router": b_router, "final_norm": final_norm[None, :],
    }


def _rope_tables(length):
    inv = ROPE_BASE ** (-jnp.arange(0, MLA_ROPE, 2, dtype=F32) / MLA_ROPE)
    ang = jnp.arange(length, dtype=F32)[:, None] * inv[None, :]
    cos, sin = jnp.cos(ang), jnp.sin(ang)
    ones = jnp.ones((length, MLA_NOPE), F32)
    tail = SLOT - MLA_NOPE - MLA_ROPE
    cos_t = jnp.concatenate([ones, cos, cos, jnp.ones((length, tail), F32)], axis=1)
    sin_t = jnp.concatenate([0.0 * ones, sin, sin, jnp.zeros((length, tail), F32)], axis=1)
    return cos_t, sin_t


def kernel(x, meta_tokens, attn_norm, w_in, q_norm, w_uq, kv_norm, w_ukv, lb_table, hg_norm, w_br_mla, w_br_hgrn, b_gate, w_out, ffn_norm, w_group, b_group, w_route, b_route, w1, w3, w2, final_norm):
    batch, seq, d = x.shape
    assert attn_norm.shape[0] == 1, "one layer"
    assert d == SUBLANES * LANES, "a token row is moved as one (8, 128) tile"
    n = batch * seq
    x2 = x.reshape(n, d)
    wts = _prepare_weights(attn_norm[0], w_in[0], q_norm[0], w_uq[0], kv_norm[0], w_ukv[0], b_gate[0],
                           hg_norm[0], w_br_mla[0], w_br_hgrn[0], w_out[0], ffn_norm[0], w_group[0],
                           b_group[0], w_route[0], b_route[0], final_norm, lb_table)
    cos_t, sin_t = _rope_tables(N_META + seq)

    meta = _in_proj(meta_tokens.astype(x.dtype), N_META, 1, cos_t[:N_META], sin_t[:N_META], wts)
    s0 = _hgrn_meta_state(meta[4], meta[5])

    tm = min(ROW_TILE, seq)
    q, k, vt, hq, lf, hi, hg, gm, gh = _in_proj(x2, tm, seq // tm, cos_t[N_META:], sin_t[N_META:], wts)
    o_mla = _attention(q, k, vt, meta[1], meta[2], batch, seq, min(ATTN_TILE, seq))
    o_hg = _hgrn(hq, lf, hi, hg, wts["hg_norm"], s0, batch, seq)
    h1, u3, route, counts = _merge(o_mla, o_hg, gm, gh, x2, wts, tm)

    code = route[:, ROUTE_CODE].astype(I32)
    cnt = counts[0, :N_GROUPS].astype(I32)
    padded = (cnt + MOE_BLOCK - 1) // MOE_BLOCK * MOE_BLOCK
    p_end = jnp.cumsum(padded)
    p_start = p_end - padded
    n_blocks = -(-n // MOE_BLOCK) + N_GROUPS
    blk_row = jnp.arange(n_blocks, dtype=I32) * MOE_BLOCK
    blk_group = jnp.minimum(jnp.sum((p_end[None, :] <= blk_row[:, None]).astype(I32), axis=1), N_GROUPS - 1)
    n_used = p_end[-1:] // MOE_BLOCK
    gid = jnp.arange(N_GROUPS, dtype=I32)
    later = (gid[None, :] > gid[:, None]) & (cnt[None, :] > 0)
    nxt = jnp.min(jnp.where(later, gid[None, :], N_GROUPS), axis=1)
    nxt = jnp.where(nxt < N_GROUPS, nxt, -1)
    next_group = jnp.sum(jnp.where(blk_group[:, None] == gid[None, :], nxt[None, :], 0), axis=1).astype(I32)

    tmove = min(MOVE_TILE, seq)
    xg = _dispatch(code, p_start, p_start + cnt, p_end, n_used, u3, n_blocks * MOE_BLOCK, tmove)
    y3 = _experts(blk_group, next_group, n_used, xg, w1[0], w3[0], w2[0])
    out = _combine(code, p_start, y3, h1, wts["final_norm"], tmove)
    return out.reshape(batch, seq, d)
```

```python
import jax
import jax.numpy as jnp
from jax import lax
from jax.experimental import pallas as pl
from jax.experimental.pallas import tpu as pltpu

F32 = jnp.float32
BF16 = jnp.bfloat16
I32 = jnp.int32

N_META = 16
EPS = 1e-6
MLA_HEADS = 8
MLA_NOPE = 64
MLA_ROPE = 32
MLA_V = 64
Q_LORA = 384
KV_LORA = 256
ROPE_BASE = 10000.0
HG_HEADS = 4
HG_DK = 128
HG_DV = 128
N_GROUPS = 8
EXPERTS_PER_GROUP = 8
N_EXPERTS = N_GROUPS * EXPERTS_PER_GROUP
D_EXPERT = 256
MOE_BLOCK = 256

LANES = 128
SUBLANES = 8
VMEM_LIMIT_BYTES = 56 * 1024 * 1024

SLOT = LANES
ONE_LANE = MLA_V
ROW_TILE = 512
ATTN_TILE = 512
ATTN_HEAD_SET = 4
MOVE_TILE = 512
MERGE_SLABS = 1
HG_CHUNK = 64
HG_SUB = 8
BAND_GROUP = 8
HEAD_SET = HG_HEADS
LOG2_E = 1.4426950408889634

SEG_CQ = (0, 384)
SEG_CKV = (384, 640)
SEG_HQ = (0, 512)
SEG_HF = (512, 1024)
SEG_HI = (1024, 1536)
SEG_HG = (1536, 2048)
SEG_GM = (2048, 3072)
SEG_GH = (3072, 4096)


def _params(*sem):
    return pltpu.CompilerParams(dimension_semantics=sem, vmem_limit_bytes=VMEM_LIMIT_BYTES)


def _rms(x, g):
    return x * lax.rsqrt(jnp.mean(x * x, axis=-1, keepdims=True) + EPS) * g


def _dot(a, b):
    return jnp.dot(a, b, preferred_element_type=F32)


def _dot_nt(a, b):
    return lax.dot_general(a, b, (((1,), (1,)), ((), ())), preferred_element_type=F32)


def _dot_tn(a, b):
    return lax.dot_general(a, b, (((0,), (0,)), ((), ())), preferred_element_type=F32)


def _split2(x):
    hi = x.astype(BF16)
    lo = (x - hi.astype(F32)).astype(BF16)
    return hi, lo


def _split3(x):
    hi = x.astype(BF16)
    r = x - hi.astype(F32)
    mid = r.astype(BF16)
    lo = (r - mid.astype(F32)).astype(BF16)
    return hi, mid, lo


def _const_spec(shape):
    nd = len(shape)
    return pl.BlockSpec(shape, lambda *_: (0,) * nd, pipeline_mode=pl.Buffered(1))


def _rows_to_tiles(x):
    return x.reshape(x.shape[0], SUBLANES, LANES)


def _tiles_to_rows(x3):
    return x3.reshape(x3.shape[0], SUBLANES * LANES)


def _lower_bound(lbt_ref):
    t0 = lbt_ref[0:1, :]
    t1 = lbt_ref[1:2, :]
    mx = jnp.maximum(t0, t1)
    e0 = jnp.exp(t0 - mx)
    return e0 / (e0 + jnp.exp(t1 - mx))


def _rope(x, cos, sin):
    half = MLA_ROPE // 2
    width = x.shape[1]
    slot_lane = lax.broadcasted_iota(I32, x.shape, 1) & (SLOT - 1)
    partner = jnp.where(slot_lane < MLA_NOPE + half, -pltpu.roll(x, width - half, axis=1),
                        pltpu.roll(x, half, axis=1))
    return x * cos + partner * sin


def _in_proj_kernel(x_ref, g_ref, wl_ref, wp_ref, wr_ref, qn_ref, kvn_ref, wq_ref, wqs_ref, wk_ref, wvt_ref,
                    cos_ref, sin_ref, bg_ref, lbt_ref,
                    q_out, k_out, vt_out, hq_out, lf_out, hi_out, hg_out, gm_out, gh_out):
    u = _rms(x_ref[...], g_ref[...]).astype(BF16)

    def lat(s):
        return _dot(u, wl_ref[:, s[0]:s[1]])

    def seg(s):
        return _dot(u, wr_ref[:, s[0]:s[1]])

    cos = cos_ref[...]
    sin = sin_ref[...]
    cos_t = jnp.tile(cos, (1, MLA_HEADS))
    sin_t = jnp.tile(sin, (1, MLA_HEADS))
    scale = (MLA_NOPE + MLA_ROPE) ** -0.5 * LOG2_E

    cq = lat(SEG_CQ)
    ckv = lat(SEG_CKV)
    k_pe = _dot(u, wp_ref[...])
    hq, hf, hi, hg = seg(SEG_HQ), seg(SEG_HF), seg(SEG_HI), seg(SEG_HG)

    cqn = _rms(cq, qn_ref[...]).astype(BF16)
    ckvn = _rms(ckv, kvn_ref[...]).astype(BF16)
    gm, gh = seg(SEG_GM), seg(SEG_GH)
    q = _dot(cqn, wq_ref[...]) * cos_t + _dot(cqn, wqs_ref[...]) * sin_t
    k = _dot(ckvn, wk_ref[...])
    vt = _dot_nt(wvt_ref[...], ckvn)

    lb = _lower_bound(lbt_ref)
    hq_out[...] = jax.nn.silu(hq).astype(BF16)
    lf_out[...] = jnp.log(lb + (1.0 - lb) * jax.nn.sigmoid(hf)) * LOG2_E
    hi_out[...] = hi.astype(BF16)
    hg_out[...] = jax.nn.silu(hg).astype(BF16)
    gm_out[...] = jax.nn.sigmoid(gm + bg_ref[0:1, :]).astype(BF16)
    gh_out[...] = jax.nn.sigmoid(gh + bg_ref[1:2, :]).astype(BF16)

    q_out[...] = (q * scale).astype(BF16)
    k_out[...] = (k + jnp.tile(_rope(k_pe, cos, sin), (1, MLA_HEADS))).astype(BF16)
    slot_row = lax.broadcasted_iota(I32, vt.shape, 0) & (SLOT - 1)
    vt = jnp.where(slot_row == ONE_LANE, 1.0, vt).astype(BF16)
    tv = vt_out.shape[2]
    for piece in range(vt_out.shape[0]):
        vt_out[piece] = vt[:, piece * tv:(piece + 1) * tv]


def _in_proj(x2, tm, tiles_per_seq, cos, sin, wts):
    n, d = x2.shape
    wide = MLA_HEADS * SLOT
    hw = HG_HEADS * HG_DK
    row = lambda width: pl.BlockSpec((tm, width), lambda i: (i, 0))
    tab = pl.BlockSpec((tm, SLOT), lambda i: (i % tiles_per_seq, 0))
    consts = (wts["attn_norm"], wts["w_lat"], wts["w_kpe"], wts["w_rest"], wts["q_norm"], wts["kv_norm"],
              wts["wq"], wts["wq_sw"], wts["wk"], wts["wv_t"])
    tail = (wts["b_gate"], wts["lb_table"])
    in_specs = ([row(d)] + [_const_spec(c.shape) for c in consts] + [tab, tab]
                + [_const_spec(c.shape) for c in tail])
    widths = (wide, wide, None, hw, hw, hw, hw, d, d)
    dtypes = (BF16, BF16, BF16, BF16, F32, BF16, BF16, BF16, BF16)
    tv = min(tm, ATTN_TILE)
    vt_spec = pl.BlockSpec((tm // tv, wide, tv), lambda i: (i, 0, 0))
    return pl.pallas_call(
        _in_proj_kernel,
        grid=(n // tm,),
        in_specs=in_specs,
        out_specs=[vt_spec if w is None else row(w) for w in widths],
        out_shape=[jax.ShapeDtypeStruct((n // tv, wide, tv) if w is None else (n, w), t)
                   for w, t in zip(widths, dtypes)],
        compiler_params=_params("parallel"),
        name="in_proj",
    )(x2, *consts, cos, sin, *tail)


def _attn_kernel(q_ref, k_ref, vt_ref, km_ref, vtm_ref, o_ref, m_ref, acc_ref):
    i = pl.program_id(1)
    tq = q_ref.shape[0]

    slots = [slice(h * SLOT, (h + 1) * SLOT) for h in range(MLA_HEADS)]

    def head_set(hs, k_of, vt_of, mask, first):
        s = [_dot_nt(k_of(slots[h]), q_ref[:, slots[h]]) for h in hs]
        if mask is not None:
            s = [jnp.where(mask, x, -jnp.inf) for x in s]
        mx = [jnp.max(x, axis=0, keepdims=True) for x in s]
        if first:
            m_new = mx
        else:
            m_old = [m_ref[h] for h in hs]
            m_new = [jnp.maximum(a, b) for a, b in zip(m_old, mx)]
        p = [jnp.exp2(x - m).astype(BF16) for x, m in zip(s, m_new)]
        for n, h in enumerate(hs):
            m_ref[h] = m_new[n]
            pv = _dot(vt_of(slots[h]), p[n])
            acc_ref[h] = pv if first else jnp.exp2(m_old[n] - m_new[n]) * acc_ref[h] + pv

    def tile(k_of, vt_of, mask, first=False):
        for h0 in range(0, MLA_HEADS, ATTN_HEAD_SET):
            head_set(range(h0, h0 + ATTN_HEAD_SET), k_of, vt_of, mask, first)

    tile(lambda sl: km_ref[:, sl], lambda sl: vtm_ref[0, sl, :], None, first=True)

    def x_tile(j, mask):
        rows = pl.ds(pl.multiple_of(j * tq, tq), tq)
        tile(lambda sl: k_ref[rows, sl], lambda sl: vt_ref[j, sl, :], mask)

    def body(j, carry):
        x_tile(j, None)
        return carry

    lax.fori_loop(0, i, body, 0)
    x_tile(i, lax.broadcasted_iota(I32, (tq, tq), 0) <= lax.broadcasted_iota(I32, (tq, tq), 1))

    outs = []
    for h in range(MLA_HEADS):
        acc = acc_ref[h]
        outs.append(acc[:MLA_V, :] / acc[ONE_LANE:ONE_LANE + 1, :])
    o_ref[...] = jnp.concatenate(outs, axis=0).T.astype(BF16)


def _attention(q, k, vt, k_meta, vt_meta, batch, seq, tq):
    wide = MLA_HEADS * SLOT
    nq = seq // tq
    assert vt.shape == (batch * nq, wide, tq)
    return pl.pallas_call(
        _attn_kernel,
        grid=(batch, nq),
        in_specs=[
            pl.BlockSpec((tq, wide), lambda b, i: (b * nq + i, 0)),
            pl.BlockSpec((seq, wide), lambda b, i: (b, 0)),
            pl.BlockSpec((nq, wide, tq), lambda b, i: (b, 0, 0)),
            _const_spec(k_meta.shape),
            _const_spec(vt_meta.shape),
        ],
        out_specs=pl.BlockSpec((tq, MLA_HEADS * MLA_V), lambda b, i: (b * nq + i, 0)),
        out_shape=jax.ShapeDtypeStruct((batch * seq, MLA_HEADS * MLA_V), BF16),
        scratch_shapes=[pltpu.VMEM((MLA_HEADS, 1, tq), F32), pltpu.VMEM((MLA_HEADS, SLOT, tq), F32)],
        compiler_params=_params("parallel", "arbitrary"),
        name="mla_attention",
    )(q, k, vt, k_meta, vt_meta)


def _cumsum_rows(tril, lf):
    parts = _split3(lf)
    return _dot(tril, parts[0]) + _dot(tril, parts[1]) + _dot(tril, parts[2])


def _tril(n):
    return (lax.broadcasted_iota(I32, (n, n), 0) >= lax.broadcasted_iota(I32, (n, n), 1)).astype(BF16)


def _hgrn_meta_kernel(lf_ref, hi_ref, s_out):
    tril = _tril(lf_ref.shape[0])
    for h in range(HG_HEADS):
        sl = slice(h * HG_DK, (h + 1) * HG_DK)
        lf = lf_ref[:, sl]
        b = _cumsum_rows(tril, lf)
        kdec = (1.0 - jnp.exp2(lf)) * jnp.exp2(b[-1:, :] - b)
        s_out[h] = _dot_tn(hi_ref[:, sl], kdec.astype(BF16))


def _hgrn_meta_state(lf, hi):
    return pl.pallas_call(
        _hgrn_meta_kernel,
        out_shape=jax.ShapeDtypeStruct((HG_HEADS, HG_DV, HG_DK), F32),
        compiler_params=pltpu.CompilerParams(vmem_limit_bytes=VMEM_LIMIT_BYTES),
        name="hgrn_meta_state",
    )(lf, hi)


def _hgrn_kernel(hq_ref, lf_ref, hi_ref, hg_ref, hgn_ref, s0_ref, o_ref,
                 st_ref, pb_ref, pk_ref, pv_ref):
    c_rows = HG_CHUNK
    n_chunks = hq_ref.shape[0] // c_rows
    st_ref[...] = s0_ref[...]
    pad = jnp.zeros((HG_HEADS, HG_SUB, HG_DK), F32)
    pb_ref[:, 0:HG_SUB, :] = pad
    pk_ref[:, 0:HG_SUB, :] = pad
    pv_ref[:, 0:HG_SUB, :] = pad
    hgn = hgn_ref[...]
    tril = _tril(c_rows)
    ones = jnp.ones((HG_DK, LANES), BF16)
    n_sub = c_rows // HG_SUB
    far = (lax.broadcasted_iota(I32, (c_rows, c_rows), 0) - lax.broadcasted_iota(I32, (c_rows, c_rows), 1)) >= HG_SUB
    heads = range(HG_HEADS)
    cols = [slice(h * HG_DK, (h + 1) * HG_DK) for h in heads]
    stash = slice(HG_SUB, HG_SUB + c_rows)

    def stages(rows, hs):
        lf = {h: lf_ref[rows, cols[h]] for h in hs}
        kk = {h: 1.0 - jnp.exp2(lf[h]) for h in hs}
        b = {h: _cumsum_rows(tril, lf[h]) for h in hs}
        q = {h: hq_ref[rows, cols[h]].astype(F32) for h in hs}
        v_bf = {h: hi_ref[rows, cols[h]] for h in hs}
        st = {h: st_ref[h] for h in hs}
        for h in hs:
            pb_ref[h, stash, :] = b[h]
            pk_ref[h, stash, :] = kk[h]
            pv_ref[h, stash, :] = v_bf[h].astype(F32)

        y = {}
        for h in hs:
            xs = [(q[h] * kk[h]).astype(BF16)]
            for d in range(1, HG_SUB):
                shifted = slice(HG_SUB - d, HG_SUB - d + c_rows)
                xs.append((q[h] * pk_ref[h, shifted, :] * jnp.exp2(b[h] - pb_ref[h, shifted, :])).astype(BF16))
            y[h] = [_dot(jnp.concatenate(xs[d0:d0 + BAND_GROUP], axis=0), ones)
                    for d0 in range(0, HG_SUB, BAND_GROUP)]

        o = {h: _dot_nt((q[h] * jnp.exp2(b[h])).astype(BF16), st[h].astype(BF16)) for h in hs}

        a_off = {}
        zero_rows = lambda r: [jnp.zeros((r, HG_DK), BF16)] if r else []
        for h in hs:
            qps, kps = [], []
            for j in range(n_sub - 1):
                k0, k1 = HG_SUB * j, HG_SUB * (j + 1)
                rj = b[h][k1 - 1:k1, :]
                qp = (q[h][k1:, :] * jnp.exp2(b[h][k1:, :] - rj)).astype(BF16)
                kp = (kk[h][k0:k1, :] * jnp.exp2(rj - b[h][k0:k1, :])).astype(BF16)
                qps.append(jnp.concatenate(zero_rows(k1) + [qp], axis=0))
                kps.append(jnp.concatenate(zero_rows(k0) + [kp] + zero_rows(c_rows - k1), axis=0))
            a = _dot_nt(jnp.concatenate(qps, axis=1), jnp.concatenate(kps, axis=1))
            a_off[h] = jnp.where(far, a, 0.0).astype(BF16)

        for h in hs:
            o[h] = o[h] + _dot(a_off[h], v_bf[h])
            for d in range(HG_SUB):
                shifted = slice(HG_SUB - d, HG_SUB - d + c_rows)
                part = y[h][d // BAND_GROUP]
                r0 = (d % BAND_GROUP) * c_rows
                o[h] = o[h] + part[r0:r0 + c_rows, :] * pv_ref[h, shifted, :]

        for h in hs:
            on = o[h] * lax.rsqrt(jnp.mean(o[h] * o[h], axis=-1, keepdims=True) + EPS)
            on = on * hgn[:, cols[h]] * hg_ref[rows, cols[h]].astype(F32)
            o_ref[rows, cols[h]] = on.astype(BF16)

        for h in hs:
            b_end = b[h][c_rows - 1:c_rows, :]
            kdec = kk[h] * jnp.exp2(b_end - b[h])
            st_ref[h] = st[h] * jnp.exp2(b_end) + _dot_tn(v_bf[h], kdec.astype(BF16))

    def chunk(c, carry):
        rows = pl.ds(pl.multiple_of(c * c_rows, c_rows), c_rows)
        for h0 in range(0, HG_HEADS, HEAD_SET):
            stages(rows, tuple(range(h0, h0 + HEAD_SET)))
        return carry

    lax.fori_loop(0, n_chunks, chunk, 0)


def _hgrn(hq, lf, hi, hg, hg_norm, s0, batch, seq):
    hw = HG_HEADS * HG_DK
    seq_spec = pl.BlockSpec((seq, hw), lambda b: (b, 0))
    pad_rows = HG_SUB + HG_CHUNK
    return pl.pallas_call(
        _hgrn_kernel,
        grid=(batch,),
        in_specs=[seq_spec, seq_spec, seq_spec, seq_spec, _const_spec(hg_norm.shape), _const_spec(s0.shape)],
        out_specs=seq_spec,
        out_shape=jax.ShapeDtypeStruct((batch * seq, hw), BF16),
        scratch_shapes=[pltpu.VMEM((HG_HEADS, HG_DV, HG_DK), F32)]
        + [pltpu.VMEM((HG_HEADS, pad_rows, HG_DK), F32)] * 3,
        compiler_params=_params("parallel"),
        name="hgrn2",
    )(hq, lf, hi, hg, hg_norm, s0)


ROUTE_CODE, ROUTE_E1, ROUTE_E2, ROUTE_G1, ROUTE_G2 = range(5)
RANK_BITS = 16
ROW_SUBLANES = 2 * SUBLANES


def _slot(code, pstart_ref):
    return pstart_ref[code >> RANK_BITS] + (code & ((1 << RANK_BITS) - 1))


def _merge_kernel(om_ref, oh_ref, gm_ref, gh_ref, x_ref, wbm_ref, wbh_ref, wo_ref, fn_ref,
                  wr_hi_ref, wr_lo_ref, br_ref, h_out, u_out, route_out, cnt_out, carry_ref):
    i = pl.program_id(0)
    tm, d = x_ref.shape

    @pl.when(i == 0)
    def _():
        carry_ref[...] = jnp.zeros_like(carry_ref)

    big = float(2 * LANES)

    def route_rows(rows):
        n_rows = rows.stop - rows.start
        a = _dot(om_ref[rows, :], wbm_ref[...])
        g = _dot(oh_ref[rows, :], wbh_ref[...])
        merged = gm_ref[rows, :].astype(F32) * a + gh_ref[rows, :].astype(F32) * g
        h1 = x_ref[rows, :] + _dot(merged.astype(BF16), wo_ref[...])
        h_out[rows, :] = h1
        u = _rms(h1, fn_ref[...])
        u_out[rows, 0:SUBLANES, :] = _rows_to_tiles(u)

        u_hi, u_lo = _split2(u)
        logits = (_dot(u_hi, wr_hi_ref[...]) + _dot(u_hi, wr_lo_ref[...]) + _dot(u_lo, wr_hi_ref[...])
                  + br_ref[...])
        lane_r = lax.broadcasted_iota(I32, (n_rows, LANES), 1)
        lane_rf = lane_r.astype(F32)

        def first_max(vals):
            mx = jnp.max(vals, axis=-1, keepdims=True)
            idx = jnp.min(jnp.where(vals == mx, lane_rf, big), axis=-1, keepdims=True)
            return mx, idx

        gl = jnp.where(lane_r < N_GROUPS, logits, -jnp.inf)
        g_max, g_sel = first_max(gl)
        p_sel = 1.0 / jnp.sum(jnp.exp(gl - g_max), axis=-1, keepdims=True)
        lo = N_GROUPS + g_sel * EXPERTS_PER_GROUP
        el = jnp.where((lane_rf >= lo) & (lane_rf < lo + EXPERTS_PER_GROUP), logits, -jnp.inf)
        v1, i1 = first_max(el)
        el2 = jnp.where(lane_rf == i1, -jnp.inf, el)
        v2, i2 = first_max(el2)
        t = jnp.exp(v2 - v1)
        return g_sel, i1, i2, p_sel * (1.0 / (1.0 + t)), p_sel * (t / (1.0 + t))

    slab = tm // MERGE_SLABS
    cols = [route_rows(slice(s * slab, (s + 1) * slab)) for s in range(MERGE_SLABS)]
    g_sel, i1, i2, g1, g2 = [jnp.concatenate(c, axis=0) for c in zip(*cols)]
    lane = lax.broadcasted_iota(I32, (tm, LANES), 1)
    lane_f = lane.astype(F32)

    hit = lane_f == g_sel
    onehot = jnp.where(hit, 1.0, 0.0)
    strict = (lax.broadcasted_iota(I32, (tm, tm), 0) > lax.broadcasted_iota(I32, (tm, tm), 1)).astype(BF16)
    before = _dot(strict, onehot.astype(BF16)) + carry_ref[0:1, :]
    rank = jnp.sum(jnp.where(hit, before, 0.0), axis=-1, keepdims=True)
    total = carry_ref[0:1, :] + jnp.sum(onehot, axis=0, keepdims=True)
    carry_ref[...] = jnp.broadcast_to(total, carry_ref.shape)
    cnt_out[...] = jnp.broadcast_to(total, cnt_out.shape)

    route = jnp.zeros((tm, LANES), F32)
    code = g_sel * float(1 << RANK_BITS) + rank
    for pos, val in ((ROUTE_CODE, code), (ROUTE_E1, i1 - N_GROUPS), (ROUTE_E2, i2 - N_GROUPS),
                     (ROUTE_G1, g1), (ROUTE_G2, g2)):
        route = jnp.where(lane == pos, val, route)
    route_out[...] = route
    record = jnp.concatenate([route, jnp.zeros((tm, d - LANES), F32)], axis=1)
    u_out[:, SUBLANES:ROW_SUBLANES, :] = _rows_to_tiles(record)


def _merge(o_mla, o_hg, gm, gh, x2, wts, tm):
    n, d = x2.shape
    row = lambda width: pl.BlockSpec((tm, width), lambda i: (i, 0))
    consts = (wts["w_br_mla"], wts["w_br_hgrn"], wts["w_out"], wts["ffn_norm"],
              wts["w_router_hi"], wts["w_router_lo"], wts["b_router"])
    return pl.pallas_call(
        _merge_kernel,
        grid=(n // tm,),
        in_specs=[row(o_mla.shape[1]), row(o_hg.shape[1]), row(d), row(d), row(d)]
        + [_const_spec(c.shape) for c in consts],
        out_specs=[row(d), pl.BlockSpec((tm, ROW_SUBLANES, LANES), lambda i: (i, 0, 0)), row(LANES),
                   pl.BlockSpec((8, LANES), lambda i: (0, 0))],
        out_shape=[jax.ShapeDtypeStruct((n, d), F32), jax.ShapeDtypeStruct((n, ROW_SUBLANES, LANES), F32),
                   jax.ShapeDtypeStruct((n, LANES), F32), jax.ShapeDtypeStruct((8, LANES), F32)],
        scratch_shapes=[pltpu.VMEM((8, LANES), F32)],
        compiler_params=_params("arbitrary"),
        name="merge_route",
    )(o_mla, o_hg, gm, gh, x2, *consts)


ROW_UNROLL = 8


def _row_copy(src_ref, src_row, dst_ref, dst_row, sem):
    return pltpu.make_async_copy(src_ref.at[pl.ds(src_row, 1)], dst_ref.at[pl.ds(dst_row, 1)], sem)


def _dispatch_kernel(code_ref, ps_ref, pad_lo_ref, pad_hi_ref, nu_ref, u_ref, w1_ref, w3_ref, w2_ref,
                     xg_out, w13_out, w2_out, zero_ref, sem, pad_sem):
    i = pl.program_id(0)
    tm = u_ref.shape[0]
    base = i * tm

    w13_out[:, :, 0:D_EXPERT] = w1_ref[...].astype(BF16)
    w13_out[:, :, D_EXPERT:2 * D_EXPERT] = w3_ref[...].astype(BF16)
    w2_out[...] = w2_ref[...].astype(BF16)

    @pl.when(i == 0)
    def _():
        zero_ref[...] = jnp.zeros_like(zero_ref)
        for g in range(N_GROUPS):
            def fill(r, carry):
                _row_copy(zero_ref, 0, xg_out, r, pad_sem).start()
                return carry

            def drain(r, carry):
                _row_copy(zero_ref, 0, xg_out, 0, pad_sem).wait()
                return carry

            lax.fori_loop(pad_lo_ref[g], pad_hi_ref[g], fill, 0)
            lax.fori_loop(pad_lo_ref[g], pad_hi_ref[g], drain, 0)

        def block_copy(b):
            rows = pl.ds(pl.multiple_of(b * MOE_BLOCK, MOE_BLOCK), MOE_BLOCK)
            return pltpu.make_async_copy(zero_ref, xg_out.at[rows], pad_sem)

        def fill_block(b, carry):
            block_copy(b).start()
            return carry

        def drain_block(b, carry):
            block_copy(b).wait()
            return carry

        n_blocks = xg_out.shape[0] // MOE_BLOCK
        lax.fori_loop(nu_ref[0], n_blocks, fill_block, 0)
        lax.fori_loop(nu_ref[0], n_blocks, drain_block, 0)

    def start(g, carry):
        r0 = pl.multiple_of(g * ROW_UNROLL, ROW_UNROLL)
        for k in range(ROW_UNROLL):
            _row_copy(u_ref, r0 + k, xg_out, _slot(code_ref[base + r0 + k], ps_ref), sem).start(priority=k % 2)
        return carry

    def wait(g, carry):
        for _ in range(ROW_UNROLL):
            _row_copy(u_ref, 0, xg_out, 0, sem).wait()
        return carry

    lax.fori_loop(0, tm // ROW_UNROLL, start, 0)
    lax.fori_loop(0, tm // ROW_UNROLL, wait, 0)


def _dispatch(code, p_start, pad_lo, pad_hi, n_used, u3, w1, w3, w2, n_slots, tm):
    n = u3.shape[0]
    tile = u3.shape[1:]
    steps = n // tm
    n_exp, d, d_e = w1.shape
    assert n_exp % steps == 0, "every grid step narrows the same number of experts"
    per_step = n_exp // steps
    experts = lambda shape: pl.BlockSpec((per_step,) + shape, lambda i, *_: (i, 0, 0))
    return pl.pallas_call(
        _dispatch_kernel,
        grid_spec=pltpu.PrefetchScalarGridSpec(
            num_scalar_prefetch=5,
            grid=(steps,),
            in_specs=[pl.BlockSpec((tm,) + tile, lambda i, *_: (i, 0, 0)),
                      experts((d, d_e)), experts((d, d_e)), experts((d_e, d))],
            out_specs=[pl.BlockSpec(memory_space=pl.ANY), experts((d, 2 * d_e)), experts((d_e, d))],
            scratch_shapes=[pltpu.VMEM((MOE_BLOCK,) + tile, F32), pltpu.SemaphoreType.DMA,
                            pltpu.SemaphoreType.DMA],
        ),
        out_shape=[jax.ShapeDtypeStruct((n_slots,) + tile, F32),
                   jax.ShapeDtypeStruct((n_exp, d, 2 * d_e), BF16),
                   jax.ShapeDtypeStruct((n_exp, d_e, d), BF16)],
        compiler_params=_params("arbitrary"),
        name="moe_dispatch",
    )(code, p_start, pad_lo, pad_hi, n_used, u3, w1, w3, w2)


def _expert_kernel(bg_ref, nu_ref, xg_ref, w13_s, w2_s, y_ref, acc_ref):
    i = pl.program_id(0)
    used = i < nu_ref[0]

    @pl.when(used)
    def _():
        group = bg_ref[i]
        x = _tiles_to_rows(xg_ref[:, 0:SUBLANES, :]).astype(BF16)
        route = _tiles_to_rows(xg_ref[:, SUBLANES:ROW_SUBLANES, :])[:, 0:LANES]
        e1 = route[:, ROUTE_E1:ROUTE_E1 + 1]
        e2 = route[:, ROUTE_E2:ROUTE_E2 + 1]
        g1 = route[:, ROUTE_G1:ROUTE_G1 + 1]
        g2 = route[:, ROUTE_G2:ROUTE_G2 + 1]
        first = (group * EXPERTS_PER_GROUP).astype(F32)
        hidden = []
        for e in range(EXPERTS_PER_GROUP):
            hcat = _dot(x, w13_s[e])
            hidden.append((jax.nn.silu(hcat[:, 0:D_EXPERT]) * hcat[:, D_EXPERT:2 * D_EXPERT]).astype(BF16))
        for e in range(EXPERTS_PER_GROUP):
            is1 = e1 == first + e
            is2 = e2 == first + e
            gate = jnp.where(is1, g1, 0.0) + jnp.where(is2, g2, 0.0)
            part = jnp.where(is1 | is2, gate * _dot(hidden[e], w2_s[e]), 0.0)
            if e == 0:
                acc_ref[...] = part
            else:
                acc_ref[...] += part
        y_ref[...] = _rows_to_tiles(acc_ref[...])

    @pl.when(jnp.logical_not(used))
    def _():
        y_ref[...] = jnp.zeros_like(y_ref)


def _experts(blk_group, n_used, xg, w13, w2):
    r = xg.shape[0]
    d = SUBLANES * LANES
    n_blocks = r // MOE_BLOCK
    last_used = lambda i, nu: jnp.minimum(i, nu[0] - 1)
    group_weights = lambda shape: pl.BlockSpec((EXPERTS_PER_GROUP,) + shape, lambda i, bg, nu: (bg[i], 0, 0))
    return pl.pallas_call(
        _expert_kernel,
        grid_spec=pltpu.PrefetchScalarGridSpec(
            num_scalar_prefetch=2,
            grid=(n_blocks,),
            in_specs=[
                pl.BlockSpec((MOE_BLOCK, ROW_SUBLANES, LANES), lambda i, bg, nu: (last_used(i, nu), 0, 0)),
                group_weights((d, 2 * D_EXPERT)),
                group_weights((D_EXPERT, d)),
            ],
            out_specs=pl.BlockSpec((MOE_BLOCK, SUBLANES, LANES), lambda i, bg, nu: (i, 0, 0)),
            scratch_shapes=[pltpu.VMEM((MOE_BLOCK, d), F32)],
        ),
        out_shape=jax.ShapeDtypeStruct((r, SUBLANES, LANES), F32),
        compiler_params=_params("arbitrary"),
        name="moe_experts",
    )(blk_group, n_used, xg, w13, w2)


def _combine_kernel(code_ref, ps_ref, y_ref, h_ref, fn_ref, o_ref, buf_ref, sem):
    tm = h_ref.shape[0]
    base = pl.program_id(0) * tm

    def start(g, carry):
        r0 = pl.multiple_of(g * ROW_UNROLL, ROW_UNROLL)
        for k in range(ROW_UNROLL):
            _row_copy(y_ref, _slot(code_ref[base + r0 + k], ps_ref), buf_ref, r0 + k, sem).start(priority=k % 2)
        return carry

    def wait(g, carry):
        for _ in range(ROW_UNROLL):
            _row_copy(y_ref, 0, buf_ref, 0, sem).wait()
        return carry

    lax.fori_loop(0, tm // ROW_UNROLL, start, 0)
    lax.fori_loop(0, tm // ROW_UNROLL, wait, 0)
    o_ref[...] = _rms(h_ref[...] + _tiles_to_rows(buf_ref[...]), fn_ref[...])


def _combine(code, p_start, y3, h1, final_norm, tm):
    n, d = h1.shape
    return pl.pallas_call(
        _combine_kernel,
        grid_spec=pltpu.PrefetchScalarGridSpec(
            num_scalar_prefetch=2,
            grid=(n // tm,),
            in_specs=[
                pl.BlockSpec(memory_space=pl.ANY),
                pl.BlockSpec((tm, d), lambda i, *_: (i, 0)),
                pl.BlockSpec((1, d), lambda i, *_: (0, 0)),
            ],
            out_specs=pl.BlockSpec((tm, d), lambda i, *_: (i, 0)),
            scratch_shapes=[pltpu.VMEM((tm, SUBLANES, LANES), F32), pltpu.SemaphoreType.DMA],
        ),
        out_shape=jax.ShapeDtypeStruct((n, d), F32),
        compiler_params=_params("arbitrary"),
        name="moe_combine",
    )(code, p_start, y3, h1, final_norm)


def _prepare_weights(attn_norm, w_in, q_norm, w_uq, kv_norm, w_ukv, b_gate, hg_norm, w_br_mla, w_br_hgrn,
                     w_out, ffn_norm, w_group, b_group, w_route, b_route, final_norm, lb_table):
    d = w_in.shape[0]
    half = MLA_ROPE // 2
    pe0, pe1 = Q_LORA + KV_LORA, Q_LORA + KV_LORA + MLA_ROPE
    zeros = lambda r, c: jnp.zeros((r, c), F32)
    w_lat = w_in[:, :pe0].astype(BF16)
    w_kpe = jnp.concatenate([zeros(d, MLA_NOPE), w_in[:, pe0:pe1], zeros(d, SLOT - MLA_NOPE - MLA_ROPE)],
                            axis=1).astype(BF16)
    w_rest = w_in[:, pe1:].astype(BF16)

    wq3 = w_uq.reshape(Q_LORA, MLA_HEADS, MLA_NOPE + MLA_ROPE)
    zq = jnp.zeros((Q_LORA, MLA_HEADS, SLOT - MLA_NOPE - MLA_ROPE), F32)
    wq = jnp.concatenate([wq3, zq], axis=-1).reshape(Q_LORA, MLA_HEADS * SLOT).astype(BF16)
    q_nope, q_pe = wq3[..., :MLA_NOPE], wq3[..., MLA_NOPE:]
    wq_sw = jnp.concatenate([jnp.zeros_like(q_nope), -q_pe[..., half:], q_pe[..., :half], zq],
                            axis=-1).reshape(Q_LORA, MLA_HEADS * SLOT).astype(BF16)

    wkv3 = w_ukv.reshape(KV_LORA, MLA_HEADS, MLA_NOPE + MLA_V)
    zk = jnp.zeros((KV_LORA, MLA_HEADS, SLOT - MLA_NOPE), F32)
    wk = jnp.concatenate([wkv3[..., :MLA_NOPE], zk], axis=-1).reshape(KV_LORA, MLA_HEADS * SLOT).astype(BF16)
    zv = jnp.zeros((KV_LORA, MLA_HEADS, SLOT - MLA_V), F32)
    wv_t = jnp.concatenate([wkv3[..., MLA_NOPE:], zv], axis=-1).reshape(KV_LORA, MLA_HEADS * SLOT).T.astype(BF16)

    w_router = jnp.concatenate([w_group, w_route, zeros(d, LANES - N_GROUPS - N_EXPERTS)], axis=1)
    wr_hi = w_router.astype(BF16)
    wr_lo = (w_router - wr_hi.astype(F32)).astype(BF16)
    b_router = jnp.concatenate([b_group, b_route, jnp.zeros((LANES - N_GROUPS - N_EXPERTS,), F32)])[None, :]
    return {
        "attn_norm": attn_norm[None, :], "w_lat": w_lat, "w_kpe": w_kpe, "w_rest": w_rest,
        "q_norm": q_norm[None, :], "kv_norm": kv_norm[None, :],
        "wq": wq, "wq_sw": wq_sw, "wk": wk, "wv_t": wv_t, "b_gate": b_gate, "lb_table": lb_table,
        "hg_norm": hg_norm[None, :], "w_br_mla": w_br_mla.astype(BF16), "w_br_hgrn": w_br_hgrn.astype(BF16),
        "w_out": w_out.astype(BF16), "ffn_norm": ffn_norm[None, :], "w_router_hi": wr_hi, "w_router_lo": wr_lo,
        "b_router": b_router, "final_norm": final_norm[None, :],
    }


def _rope_tables(length):
    inv = ROPE_BASE ** (-jnp.arange(0, MLA_ROPE, 2, dtype=F32) / MLA_ROPE)
    ang = jnp.arange(length, dtype=F32)[:, None] * inv[None, :]
    cos, sin = jnp.cos(ang), jnp.sin(ang)
    ones = jnp.ones((length, MLA_NOPE), F32)
    tail = SLOT - MLA_NOPE - MLA_ROPE
    cos_t = jnp.concatenate([ones, cos, cos, jnp.ones((length, tail), F32)], axis=1)
    sin_t = jnp.concatenate([0.0 * ones, sin, sin, jnp.zeros((length, tail), F32)], axis=1)
    return cos_t, sin_t


def kernel(x, meta_tokens, attn_norm, w_in, q_norm, w_uq, kv_norm, w_ukv, lb_table, hg_norm, w_br_mla, w_br_hgrn, b_gate, w_out, ffn_norm, w_group, b_group, w_route, b_route, w1, w3, w2, final_norm):
    batch, seq, d = x.shape
    assert attn_norm.shape[0] == 1, "one layer"
    assert d == SUBLANES * LANES, "a token row is moved as one (8, 128) tile"
    n = batch * seq
    x2 = x.reshape(n, d)
    wts = _prepare_weights(attn_norm[0], w_in[0], q_norm[0], w_uq[0], kv_norm[0], w_ukv[0], b_gate[0],
                           hg_norm[0], w_br_mla[0], w_br_hgrn[0], w_out[0], ffn_norm[0], w_group[0],
                           b_group[0], w_route[0], b_route[0], final_norm, lb_table)
    cos_t, sin_t = _rope_tables(N_META + seq)

    meta = _in_proj(meta_tokens.astype(x.dtype), N_META, 1, cos_t[:N_META], sin_t[:N_META], wts)
    s0 = _hgrn_meta_state(meta[4], meta[5])

    tm = min(ROW_TILE, seq)
    q, k, vt, hq, lf, hi, hg, gm, gh = _in_proj(x2, tm, seq // tm, cos_t[N_META:], sin_t[N_META:], wts)
    o_mla = _attention(q, k, vt, meta[1], meta[2], batch, seq, min(ATTN_TILE, seq))
    o_hg = _hgrn(hq, lf, hi, hg, wts["hg_norm"], s0, batch, seq)
    h1, u3, route, counts = _merge(o_mla, o_hg, gm, gh, x2, wts, tm)

    code = route[:, ROUTE_CODE].astype(I32)
    cnt = counts[0, :N_GROUPS].astype(I32)
    padded = (cnt + MOE_BLOCK - 1) // MOE_BLOCK * MOE_BLOCK
    p_end = jnp.cumsum(padded)
    p_start = p_end - padded
    n_blocks = -(-n // MOE_BLOCK) + N_GROUPS
    blk_row = jnp.arange(n_blocks, dtype=I32) * MOE_BLOCK
    blk_group = jnp.minimum(jnp.sum((p_end[None, :] <= blk_row[:, None]).astype(I32), axis=1), N_GROUPS - 1)
    n_used = p_end[-1:] // MOE_BLOCK

    tmove = min(MOVE_TILE, seq)
    xg, w13_bf, w2_bf = _dispatch(code, p_start, p_start + cnt, p_end, n_used, u3, w1[0], w3[0], w2[0],
                                  n_blocks * MOE_BLOCK, tmove)
    y3 = _experts(blk_group, n_used, xg, w13_bf, w2_bf)
    out = _combine(code, p_start, y3, h1, wts["final_norm"], tmove)
    return out.reshape(batch, seq, d)
```

```python
import jax
import jax.numpy as jnp
from jax import lax
from jax.experimental import pallas as pl
from jax.experimental.pallas import tpu as pltpu

F32 = jnp.float32
BF16 = jnp.bfloat16
I32 = jnp.int32

N_META = 16
EPS = 1e-6
MLA_HEADS = 8
MLA_NOPE = 64
MLA_ROPE = 32
MLA_V = 64
Q_LORA = 384
KV_LORA = 256
ROPE_BASE = 10000.0
HG_HEADS = 4
HG_DK = 128
HG_DV = 128
N_GROUPS = 8
EXPERTS_PER_GROUP = 8
N_EXPERTS = N_GROUPS * EXPERTS_PER_GROUP
D_EXPERT = 256
MOE_BLOCK = 256

LANES = 128
SUBLANES = 8
VMEM_LIMIT_BYTES = 56 * 1024 * 1024

SLOT = LANES
ONE_LANE = MLA_V
ROW_TILE = 512
ATTN_TILE = 512
ATTN_HEAD_SET = 4
MOVE_TILE = 512
MERGE_SLABS = 1
HG_CHUNK = 64
HG_SUB = 8
BAND_GROUP = 8
HEAD_SET = HG_HEADS
LOG2_E = 1.4426950408889634

SEG_CQ = (0, 384)
SEG_CKV = (384, 640)
SEG_HQ = (0, 512)
SEG_HF = (512, 1024)
SEG_HI = (1024, 1536)
SEG_HG = (1536, 2048)
SEG_GM = (2048, 3072)
SEG_GH = (3072, 4096)


def _params(*sem):
    return pltpu.CompilerParams(dimension_semantics=sem, vmem_limit_bytes=VMEM_LIMIT_BYTES)


def _rms(x, g):
    return x * lax.rsqrt(jnp.mean(x * x, axis=-1, keepdims=True) + EPS) * g


def _dot(a, b):
    return jnp.dot(a, b, preferred_element_type=F32)


def _dot_nt(a, b):
    return lax.dot_general(a, b, (((1,), (1,)), ((), ())), preferred_element_type=F32)


def _dot_tn(a, b):
    return lax.dot_general(a, b, (((0,), (0,)), ((), ())), preferred_element_type=F32)


def _split2(x):
    hi = x.astype(BF16)
    lo = (x - hi.astype(F32)).astype(BF16)
    return hi, lo


def _split3(x):
    hi = x.astype(BF16)
    r = x - hi.astype(F32)
    mid = r.astype(BF16)
    lo = (r - mid.astype(F32)).astype(BF16)
    return hi, mid, lo


def _const_spec(shape):
    nd = len(shape)
    return pl.BlockSpec(shape, lambda *_: (0,) * nd, pipeline_mode=pl.Buffered(1))


def _rows_to_tiles(x):
    return x.reshape(x.shape[0], SUBLANES, LANES)


def _tiles_to_rows(x3):
    return x3.reshape(x3.shape[0], SUBLANES * LANES)


def _lower_bound(lbt_ref):
    t0 = lbt_ref[0:1, :]
    t1 = lbt_ref[1:2, :]
    mx = jnp.maximum(t0, t1)
    e0 = jnp.exp(t0 - mx)
    return e0 / (e0 + jnp.exp(t1 - mx))


def _rope(x, cos, sin):
    half = MLA_ROPE // 2
    width = x.shape[1]
    slot_lane = lax.broadcasted_iota(I32, x.shape, 1) & (SLOT - 1)
    partner = jnp.where(slot_lane < MLA_NOPE + half, -pltpu.roll(x, width - half, axis=1),
                        pltpu.roll(x, half, axis=1))
    return x * cos + partner * sin


def _in_proj_kernel(x_ref, g_ref, wl_ref, wp_ref, wr_ref, qn_ref, kvn_ref, wq_ref, wqs_ref, wk_ref, wvt_ref,
                    cos_ref, sin_ref, bg_ref, lbt_ref,
                    q_out, k_out, vt_out, hq_out, lf_out, hi_out, hg_out, gm_out, gh_out):
    u = _rms(x_ref[...], g_ref[...]).astype(BF16)

    def lat(s):
        return _dot(u, wl_ref[:, s[0]:s[1]])

    def seg(s):
        return _dot(u, wr_ref[:, s[0]:s[1]])

    cos = cos_ref[...]
    sin = sin_ref[...]
    cos_t = jnp.tile(cos, (1, MLA_HEADS))
    sin_t = jnp.tile(sin, (1, MLA_HEADS))
    scale = (MLA_NOPE + MLA_ROPE) ** -0.5 * LOG2_E

    cq = lat(SEG_CQ)
    ckv = lat(SEG_CKV)
    k_pe = _dot(u, wp_ref[...])
    hq, hf, hi, hg = seg(SEG_HQ), seg(SEG_HF), seg(SEG_HI), seg(SEG_HG)

    cqn = _rms(cq, qn_ref[...]).astype(BF16)
    ckvn = _rms(ckv, kvn_ref[...]).astype(BF16)
    gm, gh = seg(SEG_GM), seg(SEG_GH)
    q = _dot(cqn, wq_ref[...]) * cos_t + _dot(cqn, wqs_ref[...]) * sin_t
    k = _dot(ckvn, wk_ref[...])
    vt = _dot_nt(wvt_ref[...], ckvn)

    lb = _lower_bound(lbt_ref)
    hq_out[...] = jax.nn.silu(hq).astype(BF16)
    lf_out[...] = jnp.log(lb + (1.0 - lb) * jax.nn.sigmoid(hf)) * LOG2_E
    hi_out[...] = hi.astype(BF16)
    hg_out[...] = jax.nn.silu(hg).astype(BF16)
    gm_out[...] = jax.nn.sigmoid(gm + bg_ref[0:1, :]).astype(BF16)
    gh_out[...] = jax.nn.sigmoid(gh + bg_ref[1:2, :]).astype(BF16)

    q_out[...] = (q * scale).astype(BF16)
    k_out[...] = (k + jnp.tile(_rope(k_pe, cos, sin), (1, MLA_HEADS))).astype(BF16)
    slot_row = lax.broadcasted_iota(I32, vt.shape, 0) & (SLOT - 1)
    vt = jnp.where(slot_row == ONE_LANE, 1.0, vt).astype(BF16)
    tv = vt_out.shape[2]
    for piece in range(vt_out.shape[0]):
        vt_out[piece] = vt[:, piece * tv:(piece + 1) * tv]


def _in_proj(x2, tm, tiles_per_seq, cos, sin, wts):
    n, d = x2.shape
    wide = MLA_HEADS * SLOT
    hw = HG_HEADS * HG_DK
    row = lambda width: pl.BlockSpec((tm, width), lambda i: (i, 0))
    tab = pl.BlockSpec((tm, SLOT), lambda i: (i % tiles_per_seq, 0))
    consts = (wts["attn_norm"], wts["w_lat"], wts["w_kpe"], wts["w_rest"], wts["q_norm"], wts["kv_norm"],
              wts["wq"], wts["wq_sw"], wts["wk"], wts["wv_t"])
    tail = (wts["b_gate"], wts["lb_table"])
    in_specs = ([row(d)] + [_const_spec(c.shape) for c in consts] + [tab, tab]
                + [_const_spec(c.shape) for c in tail])
    widths = (wide, wide, None, hw, hw, hw, hw, d, d)
    dtypes = (BF16, BF16, BF16, BF16, F32, BF16, BF16, BF16, BF16)
    tv = min(tm, ATTN_TILE)
    vt_spec = pl.BlockSpec((tm // tv, wide, tv), lambda i: (i, 0, 0))
    return pl.pallas_call(
        _in_proj_kernel,
        grid=(n // tm,),
        in_specs=in_specs,
        out_specs=[vt_spec if w is None else row(w) for w in widths],
        out_shape=[jax.ShapeDtypeStruct((n // tv, wide, tv) if w is None else (n, w), t)
                   for w, t in zip(widths, dtypes)],
        compiler_params=_params("parallel"),
        name="in_proj",
    )(x2, *consts, cos, sin, *tail)


def _attn_kernel(q_ref, k_ref, vt_ref, km_ref, vtm_ref, w1_ref, w3_ref, w2_ref,
                 o_ref, w13_out, w2_out, m_ref, acc_ref):
    i = pl.program_id(1)
    tq = q_ref.shape[0]

    w13_out[:, :, 0:D_EXPERT] = w1_ref[...].astype(BF16)
    w13_out[:, :, D_EXPERT:2 * D_EXPERT] = w3_ref[...].astype(BF16)
    w2_out[...] = w2_ref[...].astype(BF16)

    slots = [slice(h * SLOT, (h + 1) * SLOT) for h in range(MLA_HEADS)]

    def head_set(hs, k_of, vt_of, mask, first):
        s = [_dot_nt(k_of(slots[h]), q_ref[:, slots[h]]) for h in hs]
        if mask is not None:
            s = [jnp.where(mask, x, -jnp.inf) for x in s]
        mx = [jnp.max(x, axis=0, keepdims=True) for x in s]
        if first:
            m_new = mx
        else:
            m_old = [m_ref[h] for h in hs]
            m_new = [jnp.maximum(a, b) for a, b in zip(m_old, mx)]
        p = [jnp.exp2(x - m).astype(BF16) for x, m in zip(s, m_new)]
        for n, h in enumerate(hs):
            m_ref[h] = m_new[n]
            pv = _dot(vt_of(slots[h]), p[n])
            acc_ref[h] = pv if first else jnp.exp2(m_old[n] - m_new[n]) * acc_ref[h] + pv

    def tile(k_of, vt_of, mask, first=False):
        for h0 in range(0, MLA_HEADS, ATTN_HEAD_SET):
            head_set(range(h0, h0 + ATTN_HEAD_SET), k_of, vt_of, mask, first)

    tile(lambda sl: km_ref[:, sl], lambda sl: vtm_ref[0, sl, :], None, first=True)

    def x_tile(j, mask):
        rows = pl.ds(pl.multiple_of(j * tq, tq), tq)
        tile(lambda sl: k_ref[rows, sl], lambda sl: vt_ref[j, sl, :], mask)

    def body(j, carry):
        x_tile(j, None)
        return carry

    lax.fori_loop(0, i, body, 0)
    x_tile(i, lax.broadcasted_iota(I32, (tq, tq), 0) <= lax.broadcasted_iota(I32, (tq, tq), 1))

    outs = []
    for h in range(MLA_HEADS):
        acc = acc_ref[h]
        outs.append(acc[:MLA_V, :] / acc[ONE_LANE:ONE_LANE + 1, :])
    o_ref[...] = jnp.concatenate(outs, axis=0).T.astype(BF16)


def _attention(q, k, vt, k_meta, vt_meta, w1, w3, w2, batch, seq, tq):
    wide = MLA_HEADS * SLOT
    nq = seq // tq
    assert vt.shape == (batch * nq, wide, tq)
    n_exp, d, d_e = w1.shape
    assert n_exp % (batch * nq) == 0, "every grid step narrows the same number of experts"
    per_step = n_exp // (batch * nq)
    experts = lambda shape: pl.BlockSpec((per_step,) + shape, lambda b, i: (b * nq + i, 0, 0))
    return pl.pallas_call(
        _attn_kernel,
        grid=(batch, nq),
        in_specs=[
            pl.BlockSpec((tq, wide), lambda b, i: (b * nq + i, 0)),
            pl.BlockSpec((seq, wide), lambda b, i: (b, 0)),
            pl.BlockSpec((nq, wide, tq), lambda b, i: (b, 0, 0)),
            _const_spec(k_meta.shape),
            _const_spec(vt_meta.shape),
            experts((d, d_e)), experts((d, d_e)), experts((d_e, d)),
        ],
        out_specs=[pl.BlockSpec((tq, MLA_HEADS * MLA_V), lambda b, i: (b * nq + i, 0)),
                   experts((d, 2 * d_e)), experts((d_e, d))],
        out_shape=[jax.ShapeDtypeStruct((batch * seq, MLA_HEADS * MLA_V), BF16),
                   jax.ShapeDtypeStruct((n_exp, d, 2 * d_e), BF16),
                   jax.ShapeDtypeStruct((n_exp, d_e, d), BF16)],
        scratch_shapes=[pltpu.VMEM((MLA_HEADS, 1, tq), F32), pltpu.VMEM((MLA_HEADS, SLOT, tq), F32)],
        compiler_params=_params("parallel", "arbitrary"),
        name="mla_attention",
    )(q, k, vt, k_meta, vt_meta, w1, w3, w2)


def _cumsum_rows(tril, lf):
    parts = _split3(lf)
    return _dot(tril, parts[0]) + _dot(tril, parts[1]) + _dot(tril, parts[2])


def _tril(n):
    return (lax.broadcasted_iota(I32, (n, n), 0) >= lax.broadcasted_iota(I32, (n, n), 1)).astype(BF16)


def _hgrn_meta_kernel(lf_ref, hi_ref, s_out):
    tril = _tril(lf_ref.shape[0])
    for h in range(HG_HEADS):
        sl = slice(h * HG_DK, (h + 1) * HG_DK)
        lf = lf_ref[:, sl]
        b = _cumsum_rows(tril, lf)
        kdec = (1.0 - jnp.exp2(lf)) * jnp.exp2(b[-1:, :] - b)
        s_out[h] = _dot_tn(hi_ref[:, sl], kdec.astype(BF16))


def _hgrn_meta_state(lf, hi):
    return pl.pallas_call(
        _hgrn_meta_kernel,
        out_shape=jax.ShapeDtypeStruct((HG_HEADS, HG_DV, HG_DK), F32),
        compiler_params=pltpu.CompilerParams(vmem_limit_bytes=VMEM_LIMIT_BYTES),
        name="hgrn_meta_state",
    )(lf, hi)


def _hgrn_kernel(hq_ref, lf_ref, hi_ref, hg_ref, hgn_ref, s0_ref, o_ref,
                 st_ref, pb_ref, pk_ref, pv_ref):
    c_rows = HG_CHUNK
    n_chunks = hq_ref.shape[0] // c_rows
    st_ref[...] = s0_ref[...]
    pad = jnp.zeros((HG_HEADS, HG_SUB, HG_DK), F32)
    pb_ref[:, 0:HG_SUB, :] = pad
    pk_ref[:, 0:HG_SUB, :] = pad
    pv_ref[:, 0:HG_SUB, :] = pad
    hgn = hgn_ref[...]
    tril = _tril(c_rows)
    ones = jnp.ones((HG_DK, LANES), BF16)
    n_sub = c_rows // HG_SUB
    far = (lax.broadcasted_iota(I32, (c_rows, c_rows), 0) - lax.broadcasted_iota(I32, (c_rows, c_rows), 1)) >= HG_SUB
    heads = range(HG_HEADS)
    cols = [slice(h * HG_DK, (h + 1) * HG_DK) for h in heads]
    stash = slice(HG_SUB, HG_SUB + c_rows)

    def stages(rows, hs):
        lf = {h: lf_ref[rows, cols[h]] for h in hs}
        kk = {h: 1.0 - jnp.exp2(lf[h]) for h in hs}
        b = {h: _cumsum_rows(tril, lf[h]) for h in hs}
        q = {h: hq_ref[rows, cols[h]].astype(F32) for h in hs}
        v_bf = {h: hi_ref[rows, cols[h]] for h in hs}
        st = {h: st_ref[h] for h in hs}
        for h in hs:
            pb_ref[h, stash, :] = b[h]
            pk_ref[h, stash, :] = kk[h]
            pv_ref[h, stash, :] = v_bf[h].astype(F32)

        y = {}
        for h in hs:
            xs = [(q[h] * kk[h]).astype(BF16)]
            for d in range(1, HG_SUB):
                shifted = slice(HG_SUB - d, HG_SUB - d + c_rows)
                xs.append((q[h] * pk_ref[h, shifted, :] * jnp.exp2(b[h] - pb_ref[h, shifted, :])).astype(BF16))
            y[h] = [_dot(jnp.concatenate(xs[d0:d0 + BAND_GROUP], axis=0), ones)
                    for d0 in range(0, HG_SUB, BAND_GROUP)]

        o = {h: _dot_nt((q[h] * jnp.exp2(b[h])).astype(BF16), st[h].astype(BF16)) for h in hs}

        a_off = {}
        zero_rows = lambda r: [jnp.zeros((r, HG_DK), BF16)] if r else []
        for h in hs:
            qps, kps = [], []
            for j in range(n_sub - 1):
                k0, k1 = HG_SUB * j, HG_SUB * (j + 1)
                rj = b[h][k1 - 1:k1, :]
                qp = (q[h][k1:, :] * jnp.exp2(b[h][k1:, :] - rj)).astype(BF16)
                kp = (kk[h][k0:k1, :] * jnp.exp2(rj - b[h][k0:k1, :])).astype(BF16)
                qps.append(jnp.concatenate(zero_rows(k1) + [qp], axis=0))
                kps.append(jnp.concatenate(zero_rows(k0) + [kp] + zero_rows(c_rows - k1), axis=0))
            a = _dot_nt(jnp.concatenate(qps, axis=1), jnp.concatenate(kps, axis=1))
            a_off[h] = jnp.where(far, a, 0.0).astype(BF16)

        for h in hs:
            o[h] = o[h] + _dot(a_off[h], v_bf[h])
            for d in range(HG_SUB):
                shifted = slice(HG_SUB - d, HG_SUB - d + c_rows)
                part = y[h][d // BAND_GROUP]
                r0 = (d % BAND_GROUP) * c_rows
                o[h] = o[h] + part[r0:r0 + c_rows, :] * pv_ref[h, shifted, :]

        for h in hs:
            on = o[h] * lax.rsqrt(jnp.mean(o[h] * o[h], axis=-1, keepdims=True) + EPS)
            on = on * hgn[:, cols[h]] * hg_ref[rows, cols[h]].astype(F32)
            o_ref[rows, cols[h]] = on.astype(BF16)

        for h in hs:
            b_end = b[h][c_rows - 1:c_rows, :]
            kdec = kk[h] * jnp.exp2(b_end - b[h])
            st_ref[h] = st[h] * jnp.exp2(b_end) + _dot_tn(v_bf[h], kdec.astype(BF16))

    def chunk(c, carry):
        rows = pl.ds(pl.multiple_of(c * c_rows, c_rows), c_rows)
        for h0 in range(0, HG_HEADS, HEAD_SET):
            stages(rows, tuple(range(h0, h0 + HEAD_SET)))
        return carry

    lax.fori_loop(0, n_chunks, chunk, 0)


def _hgrn(hq, lf, hi, hg, hg_norm, s0, batch, seq):
    hw = HG_HEADS * HG_DK
    seq_spec = pl.BlockSpec((seq, hw), lambda b: (b, 0))
    pad_rows = HG_SUB + HG_CHUNK
    return pl.pallas_call(
        _hgrn_kernel,
        grid=(batch,),
        in_specs=[seq_spec, seq_spec, seq_spec, seq_spec, _const_spec(hg_norm.shape), _const_spec(s0.shape)],
        out_specs=seq_spec,
        out_shape=jax.ShapeDtypeStruct((batch * seq, hw), BF16),
        scratch_shapes=[pltpu.VMEM((HG_HEADS, HG_DV, HG_DK), F32)]
        + [pltpu.VMEM((HG_HEADS, pad_rows, HG_DK), F32)] * 3,
        compiler_params=_params("parallel"),
        name="hgrn2",
    )(hq, lf, hi, hg, hg_norm, s0)


ROUTE_CODE, ROUTE_E1, ROUTE_E2, ROUTE_G1, ROUTE_G2 = range(5)
RANK_BITS = 16
ROW_SUBLANES = 2 * SUBLANES


def _slot(code, pstart_ref):
    return pstart_ref[code >> RANK_BITS] + (code & ((1 << RANK_BITS) - 1))


def _merge_kernel(om_ref, oh_ref, gm_ref, gh_ref, x_ref, wbm_ref, wbh_ref, wo_ref, fn_ref,
                  wr_hi_ref, wr_lo_ref, br_ref, h_out, u_out, route_out, cnt_out, carry_ref):
    i = pl.program_id(0)
    tm, d = x_ref.shape

    @pl.when(i == 0)
    def _():
        carry_ref[...] = jnp.zeros_like(carry_ref)

    big = float(2 * LANES)

    def route_rows(rows):
        n_rows = rows.stop - rows.start
        a = _dot(om_ref[rows, :], wbm_ref[...])
        g = _dot(oh_ref[rows, :], wbh_ref[...])
        merged = gm_ref[rows, :].astype(F32) * a + gh_ref[rows, :].astype(F32) * g
        h1 = x_ref[rows, :] + _dot(merged.astype(BF16), wo_ref[...])
        h_out[rows, :] = h1
        u = _rms(h1, fn_ref[...])
        u_out[rows, 0:SUBLANES, :] = _rows_to_tiles(u)

        u_hi, u_lo = _split2(u)
        logits = (_dot(u_hi, wr_hi_ref[...]) + _dot(u_hi, wr_lo_ref[...]) + _dot(u_lo, wr_hi_ref[...])
                  + br_ref[...])
        lane_r = lax.broadcasted_iota(I32, (n_rows, LANES), 1)
        lane_rf = lane_r.astype(F32)

        def first_max(vals):
            mx = jnp.max(vals, axis=-1, keepdims=True)
            idx = jnp.min(jnp.where(vals == mx, lane_rf, big), axis=-1, keepdims=True)
            return mx, idx

        gl = jnp.where(lane_r < N_GROUPS, logits, -jnp.inf)
        g_max, g_sel = first_max(gl)
        p_sel = 1.0 / jnp.sum(jnp.exp(gl - g_max), axis=-1, keepdims=True)
        lo = N_GROUPS + g_sel * EXPERTS_PER_GROUP
        el = jnp.where((lane_rf >= lo) & (lane_rf < lo + EXPERTS_PER_GROUP), logits, -jnp.inf)
        v1, i1 = first_max(el)
        el2 = jnp.where(lane_rf == i1, -jnp.inf, el)
        v2, i2 = first_max(el2)
        t = jnp.exp(v2 - v1)
        return g_sel, i1, i2, p_sel * (1.0 / (1.0 + t)), p_sel * (t / (1.0 + t))

    slab = tm // MERGE_SLABS
    cols = [route_rows(slice(s * slab, (s + 1) * slab)) for s in range(MERGE_SLABS)]
    g_sel, i1, i2, g1, g2 = [jnp.concatenate(c, axis=0) for c in zip(*cols)]
    lane = lax.broadcasted_iota(I32, (tm, LANES), 1)
    lane_f = lane.astype(F32)

    hit = lane_f == g_sel
    onehot = jnp.where(hit, 1.0, 0.0)
    strict = (lax.broadcasted_iota(I32, (tm, tm), 0) > lax.broadcasted_iota(I32, (tm, tm), 1)).astype(BF16)
    before = _dot(strict, onehot.astype(BF16)) + carry_ref[0:1, :]
    rank = jnp.sum(jnp.where(hit, before, 0.0), axis=-1, keepdims=True)
    total = carry_ref[0:1, :] + jnp.sum(onehot, axis=0, keepdims=True)
    carry_ref[...] = jnp.broadcast_to(total, carry_ref.shape)
    cnt_out[...] = jnp.broadcast_to(total, cnt_out.shape)

    route = jnp.zeros((tm, LANES), F32)
    code = g_sel * float(1 << RANK_BITS) + rank
    for pos, val in ((ROUTE_CODE, code), (ROUTE_E1, i1 - N_GROUPS), (ROUTE_E2, i2 - N_GROUPS),
                     (ROUTE_G1, g1), (ROUTE_G2, g2)):
        route = jnp.where(lane == pos, val, route)
    route_out[...] = route
    record = jnp.concatenate([route, jnp.zeros((tm, d - LANES), F32)], axis=1)
    u_out[:, SUBLANES:ROW_SUBLANES, :] = _rows_to_tiles(record)


def _merge(o_mla, o_hg, gm, gh, x2, wts, tm):
    n, d = x2.shape
    row = lambda width: pl.BlockSpec((tm, width), lambda i: (i, 0))
    consts = (wts["w_br_mla"], wts["w_br_hgrn"], wts["w_out"], wts["ffn_norm"],
              wts["w_router_hi"], wts["w_router_lo"], wts["b_router"])
    return pl.pallas_call(
        _merge_kernel,
        grid=(n // tm,),
        in_specs=[row(o_mla.shape[1]), row(o_hg.shape[1]), row(d), row(d), row(d)]
        + [_const_spec(c.shape) for c in consts],
        out_specs=[row(d), pl.BlockSpec((tm, ROW_SUBLANES, LANES), lambda i: (i, 0, 0)), row(LANES),
                   pl.BlockSpec((8, LANES), lambda i: (0, 0))],
        out_shape=[jax.ShapeDtypeStruct((n, d), F32), jax.ShapeDtypeStruct((n, ROW_SUBLANES, LANES), F32),
                   jax.ShapeDtypeStruct((n, LANES), F32), jax.ShapeDtypeStruct((8, LANES), F32)],
        scratch_shapes=[pltpu.VMEM((8, LANES), F32)],
        compiler_params=_params("arbitrary"),
        name="merge_route",
    )(o_mla, o_hg, gm, gh, x2, *consts)


ROW_UNROLL = 8


def _row_copy(src_ref, src_row, dst_ref, dst_row, sem):
    return pltpu.make_async_copy(src_ref.at[pl.ds(src_row, 1)], dst_ref.at[pl.ds(dst_row, 1)], sem)


def _dispatch_kernel(code_ref, ps_ref, pad_lo_ref, pad_hi_ref, nu_ref, u_ref, xg_out, zero_ref, sem, pad_sem):
    i = pl.program_id(0)
    tm = u_ref.shape[0]
    base = i * tm

    @pl.when(i == 0)
    def _():
        zero_ref[...] = jnp.zeros_like(zero_ref)
        for g in range(N_GROUPS):
            def fill(r, carry):
                _row_copy(zero_ref, 0, xg_out, r, pad_sem).start()
                return carry

            def drain(r, carry):
                _row_copy(zero_ref, 0, xg_out, 0, pad_sem).wait()
                return carry

            lax.fori_loop(pad_lo_ref[g], pad_hi_ref[g], fill, 0)
            lax.fori_loop(pad_lo_ref[g], pad_hi_ref[g], drain, 0)

        def block_copy(b):
            rows = pl.ds(pl.multiple_of(b * MOE_BLOCK, MOE_BLOCK), MOE_BLOCK)
            return pltpu.make_async_copy(zero_ref, xg_out.at[rows], pad_sem)

        def fill_block(b, carry):
            block_copy(b).start()
            return carry

        def drain_block(b, carry):
            block_copy(b).wait()
            return carry

        n_blocks = xg_out.shape[0] // MOE_BLOCK
        lax.fori_loop(nu_ref[0], n_blocks, fill_block, 0)
        lax.fori_loop(nu_ref[0], n_blocks, drain_block, 0)

    def start(g, carry):
        r0 = pl.multiple_of(g * ROW_UNROLL, ROW_UNROLL)
        for k in range(ROW_UNROLL):
            _row_copy(u_ref, r0 + k, xg_out, _slot(code_ref[base + r0 + k], ps_ref), sem).start(priority=k % 2)
        return carry

    def wait(g, carry):
        for _ in range(ROW_UNROLL):
            _row_copy(u_ref, 0, xg_out, 0, sem).wait()
        return carry

    lax.fori_loop(0, tm // ROW_UNROLL, start, 0)
    lax.fori_loop(0, tm // ROW_UNROLL, wait, 0)


def _dispatch(code, p_start, pad_lo, pad_hi, n_used, u3, n_slots, tm):
    n = u3.shape[0]
    tile = u3.shape[1:]
    return pl.pallas_call(
        _dispatch_kernel,
        grid_spec=pltpu.PrefetchScalarGridSpec(
            num_scalar_prefetch=5,
            grid=(n // tm,),
            in_specs=[pl.BlockSpec((tm,) + tile, lambda i, *_: (i, 0, 0))],
            out_specs=pl.BlockSpec(memory_space=pl.ANY),
            scratch_shapes=[pltpu.VMEM((MOE_BLOCK,) + tile, F32), pltpu.SemaphoreType.DMA,
                            pltpu.SemaphoreType.DMA],
        ),
        out_shape=jax.ShapeDtypeStruct((n_slots,) + tile, F32),
        compiler_params=_params("arbitrary"),
        name="moe_dispatch",
    )(code, p_start, pad_lo, pad_hi, n_used, u3)


def _expert_kernel(bg_ref, nu_ref, xg_ref, w13_s, w2_s, y_ref, acc_ref):
    i = pl.program_id(0)
    used = i < nu_ref[0]

    @pl.when(used)
    def _():
        group = bg_ref[i]
        x = _tiles_to_rows(xg_ref[:, 0:SUBLANES, :]).astype(BF16)
        route = _tiles_to_rows(xg_ref[:, SUBLANES:ROW_SUBLANES, :])[:, 0:LANES]
        e1 = route[:, ROUTE_E1:ROUTE_E1 + 1]
        e2 = route[:, ROUTE_E2:ROUTE_E2 + 1]
        g1 = route[:, ROUTE_G1:ROUTE_G1 + 1]
        g2 = route[:, ROUTE_G2:ROUTE_G2 + 1]
        first = (group * EXPERTS_PER_GROUP).astype(F32)
        hidden = []
        for e in range(EXPERTS_PER_GROUP):
            hcat = _dot(x, w13_s[e])
            hidden.append((jax.nn.silu(hcat[:, 0:D_EXPERT]) * hcat[:, D_EXPERT:2 * D_EXPERT]).astype(BF16))
        for e in range(EXPERTS_PER_GROUP):
            is1 = e1 == first + e
            is2 = e2 == first + e
            gate = jnp.where(is1, g1, 0.0) + jnp.where(is2, g2, 0.0)
            part = jnp.where(is1 | is2, gate * _dot(hidden[e], w2_s[e]), 0.0)
            if e == 0:
                acc_ref[...] = part
            else:
                acc_ref[...] += part
        y_ref[...] = _rows_to_tiles(acc_ref[...])

    @pl.when(jnp.logical_not(used))
    def _():
        y_ref[...] = jnp.zeros_like(y_ref)


def _experts(blk_group, n_used, xg, w13, w2):
    r = xg.shape[0]
    d = SUBLANES * LANES
    n_blocks = r // MOE_BLOCK
    last_used = lambda i, nu: jnp.minimum(i, nu[0] - 1)
    group_weights = lambda shape: pl.BlockSpec((EXPERTS_PER_GROUP,) + shape, lambda i, bg, nu: (bg[i], 0, 0))
    return pl.pallas_call(
        _expert_kernel,
        grid_spec=pltpu.PrefetchScalarGridSpec(
            num_scalar_prefetch=2,
            grid=(n_blocks,),
            in_specs=[
                pl.BlockSpec((MOE_BLOCK, ROW_SUBLANES, LANES), lambda i, bg, nu: (last_used(i, nu), 0, 0)),
                group_weights((d, 2 * D_EXPERT)),
                group_weights((D_EXPERT, d)),
            ],
            out_specs=pl.BlockSpec((MOE_BLOCK, SUBLANES, LANES), lambda i, bg, nu: (i, 0, 0)),
            scratch_shapes=[pltpu.VMEM((MOE_BLOCK, d), F32)],
        ),
        out_shape=jax.ShapeDtypeStruct((r, SUBLANES, LANES), F32),
        compiler_params=_params("arbitrary"),
        name="moe_experts",
    )(blk_group, n_used, xg, w13, w2)


def _combine_kernel(code_ref, ps_ref, y_ref, h_ref, fn_ref, o_ref, buf_ref, sem):
    tm = h_ref.shape[0]
    base = pl.program_id(0) * tm

    def start(g, carry):
        r0 = pl.multiple_of(g * ROW_UNROLL, ROW_UNROLL)
        for k in range(ROW_UNROLL):
            _row_copy(y_ref, _slot(code_ref[base + r0 + k], ps_ref), buf_ref, r0 + k, sem).start(priority=k % 2)
        return carry

    def wait(g, carry):
        for _ in range(ROW_UNROLL):
            _row_copy(y_ref, 0, buf_ref, 0, sem).wait()
        return carry

    lax.fori_loop(0, tm // ROW_UNROLL, start, 0)
    lax.fori_loop(0, tm // ROW_UNROLL, wait, 0)
    o_ref[...] = _rms(h_ref[...] + _tiles_to_rows(buf_ref[...]), fn_ref[...])


def _combine(code, p_start, y3, h1, final_norm, tm):
    n, d = h1.shape
    return pl.pallas_call(
        _combine_kernel,
        grid_spec=pltpu.PrefetchScalarGridSpec(
            num_scalar_prefetch=2,
            grid=(n // tm,),
            in_specs=[
                pl.BlockSpec(memory_space=pl.ANY),
                pl.BlockSpec((tm, d), lambda i, *_: (i, 0)),
                pl.BlockSpec((1, d), lambda i, *_: (0, 0)),
            ],
            out_specs=pl.BlockSpec((tm, d), lambda i, *_: (i, 0)),
            scratch_shapes=[pltpu.VMEM((tm, SUBLANES, LANES), F32), pltpu.SemaphoreType.DMA],
        ),
        out_shape=jax.ShapeDtypeStruct((n, d), F32),
        compiler_params=_params("arbitrary"),
        name="moe_combine",
    )(code, p_start, y3, h1, final_norm)


def _prepare_weights(attn_norm, w_in, q_norm, w_uq, kv_norm, w_ukv, b_gate, hg_norm, w_br_mla, w_br_hgrn,
                     w_out, ffn_norm, w_group, b_group, w_route, b_route, final_norm, lb_table):
    d = w_in.shape[0]
    half = MLA_ROPE // 2
    pe0, pe1 = Q_LORA + KV_LORA, Q_LORA + KV_LORA + MLA_ROPE
    zeros = lambda r, c: jnp.zeros((r, c), F32)
    w_lat = w_in[:, :pe0].astype(BF16)
    w_kpe = jnp.concatenate([zeros(d, MLA_NOPE), w_in[:, pe0:pe1], zeros(d, SLOT - MLA_NOPE - MLA_ROPE)],
                            axis=1).astype(BF16)
    w_rest = w_in[:, pe1:].astype(BF16)

    wq3 = w_uq.reshape(Q_LORA, MLA_HEADS, MLA_NOPE + MLA_ROPE)
    zq = jnp.zeros((Q_LORA, MLA_HEADS, SLOT - MLA_NOPE - MLA_ROPE), F32)
    wq = jnp.concatenate([wq3, zq], axis=-1).reshape(Q_LORA, MLA_HEADS * SLOT).astype(BF16)
    q_nope, q_pe = wq3[..., :MLA_NOPE], wq3[..., MLA_NOPE:]
    wq_sw = jnp.concatenate([jnp.zeros_like(q_nope), -q_pe[..., half:], q_pe[..., :half], zq],
                            axis=-1).reshape(Q_LORA, MLA_HEADS * SLOT).astype(BF16)

    wkv3 = w_ukv.reshape(KV_LORA, MLA_HEADS, MLA_NOPE + MLA_V)
    zk = jnp.zeros((KV_LORA, MLA_HEADS, SLOT - MLA_NOPE), F32)
    wk = jnp.concatenate([wkv3[..., :MLA_NOPE], zk], axis=-1).reshape(KV_LORA, MLA_HEADS * SLOT).astype(BF16)
    zv = jnp.zeros((KV_LORA, MLA_HEADS, SLOT - MLA_V), F32)
    wv_t = jnp.concatenate([wkv3[..., MLA_NOPE:], zv], axis=-1).reshape(KV_LORA, MLA_HEADS * SLOT).T.astype(BF16)

    w_router = jnp.concatenate([w_group, w_route, zeros(d, LANES - N_GROUPS - N_EXPERTS)], axis=1)
    wr_hi = w_router.astype(BF16)
    wr_lo = (w_router - wr_hi.astype(F32)).astype(BF16)
    b_router = jnp.concatenate([b_group, b_route, jnp.zeros((LANES - N_GROUPS - N_EXPERTS,), F32)])[None, :]
    return {
        "attn_norm": attn_norm[None, :], "w_lat": w_lat, "w_kpe": w_kpe, "w_rest": w_rest,
        "q_norm": q_norm[None, :], "kv_norm": kv_norm[None, :],
        "wq": wq, "wq_sw": wq_sw, "wk": wk, "wv_t": wv_t, "b_gate": b_gate, "lb_table": lb_table,
        "hg_norm": hg_norm[None, :], "w_br_mla": w_br_mla.astype(BF16), "w_br_hgrn": w_br_hgrn.astype(BF16),
        "w_out": w_out.astype(BF16), "ffn_norm": ffn_norm[None, :], "w_router_hi": wr_hi, "w_router_lo": wr_lo,
        "b_router": b_router, "final_norm": final_norm[None, :],
    }


def _rope_tables(length):
    inv = ROPE_BASE ** (-jnp.arange(0, MLA_ROPE, 2, dtype=F32) / MLA_ROPE)
    ang = jnp.arange(length, dtype=F32)[:, None] * inv[None, :]
    cos, sin = jnp.cos(ang), jnp.sin(ang)
    ones = jnp.ones((length, MLA_NOPE), F32)
    tail = SLOT - MLA_NOPE - MLA_ROPE
    cos_t = jnp.concatenate([ones, cos, cos, jnp.ones((length, tail), F32)], axis=1)
    sin_t = jnp.concatenate([0.0 * ones, sin, sin, jnp.zeros((length, tail), F32)], axis=1)
    return cos_t, sin_t


def kernel(x, meta_tokens, attn_norm, w_in, q_norm, w_uq, kv_norm, w_ukv, lb_table, hg_norm, w_br_mla, w_br_hgrn, b_gate, w_out, ffn_norm, w_group, b_group, w_route, b_route, w1, w3, w2, final_norm):
    batch, seq, d = x.shape
    assert attn_norm.shape[0] == 1, "one layer"
    assert d == SUBLANES * LANES, "a token row is moved as one (8, 128) tile"
    n = batch * seq
    x2 = x.reshape(n, d)
    wts = _prepare_weights(attn_norm[0], w_in[0], q_norm[0], w_uq[0], kv_norm[0], w_ukv[0], b_gate[0],
                           hg_norm[0], w_br_mla[0], w_br_hgrn[0], w_out[0], ffn_norm[0], w_group[0],
                           b_group[0], w_route[0], b_route[0], final_norm, lb_table)
    cos_t, sin_t = _rope_tables(N_META + seq)

    meta = _in_proj(meta_tokens.astype(x.dtype), N_META, 1, cos_t[:N_META], sin_t[:N_META], wts)
    s0 = _hgrn_meta_state(meta[4], meta[5])

    tm = min(ROW_TILE, seq)
    q, k, vt, hq, lf, hi, hg, gm, gh = _in_proj(x2, tm, seq // tm, cos_t[N_META:], sin_t[N_META:], wts)
    o_mla, w13_bf, w2_bf = _attention(q, k, vt, meta[1], meta[2], w1[0], w3[0], w2[0], batch, seq,
                                      min(ATTN_TILE, seq))
    o_hg = _hgrn(hq, lf, hi, hg, wts["hg_norm"], s0, batch, seq)
    h1, u3, route, counts = _merge(o_mla, o_hg, gm, gh, x2, wts, tm)

    code = route[:, ROUTE_CODE].astype(I32)
    cnt = counts[0, :N_GROUPS].astype(I32)
    padded = (cnt + MOE_BLOCK - 1) // MOE_BLOCK * MOE_BLOCK
    p_end = jnp.cumsum(padded)
    p_start = p_end - padded
    n_blocks = -(-n // MOE_BLOCK) + N_GROUPS
    blk_row = jnp.arange(n_blocks, dtype=I32) * MOE_BLOCK
    blk_group = jnp.minimum(jnp.sum((p_end[None, :] <= blk_row[:, None]).astype(I32), axis=1), N_GROUPS - 1)
    n_used = p_end[-1:] // MOE_BLOCK

    tmove = min(MOVE_TILE, seq)
    xg = _dispatch(code, p_start, p_start + cnt, p_end, n_used, u3, n_blocks * MOE_BLOCK, tmove)
    y3 = _experts(blk_group, n_used, xg, w13_bf, w2_bf)
    out = _combine(code, p_start, y3, h1, wts["final_norm"], tmove)
    return out.reshape(batch, seq, d)
```

```python
import jax
import jax.numpy as jnp
from jax import lax
from jax.experimental import pallas as pl
from jax.experimental.pallas import tpu as pltpu

F32 = jnp.float32
BF16 = jnp.bfloat16
I32 = jnp.int32

N_META = 16
EPS = 1e-6
MLA_HEADS = 8
MLA_NOPE = 64
MLA_ROPE = 32
MLA_V = 64
Q_LORA = 384
KV_LORA = 256
ROPE_BASE = 10000.0
HG_HEADS = 4
HG_DK = 128
HG_DV = 128
N_GROUPS = 8
EXPERTS_PER_GROUP = 8
N_EXPERTS = N_GROUPS * EXPERTS_PER_GROUP
D_EXPERT = 256
MOE_BLOCK = 256

LANES = 128
SUBLANES = 8
VMEM_LIMIT_BYTES = 56 * 1024 * 1024

SLOT = LANES
ONE_LANE = MLA_V
ROW_TILE = 512
ATTN_TILE = 512
ATTN_HEAD_SET = 4
MOVE_TILE = 1024
MERGE_SLABS = 1
HG_CHUNK = 64
HG_SUB = 8
BAND_GROUP = 8
HEAD_SET = HG_HEADS
LOG2_E = 1.4426950408889634

SEG_CQ = (0, 384)
SEG_CKV = (384, 640)
SEG_HQ = (0, 512)
SEG_HF = (512, 1024)
SEG_HI = (1024, 1536)
SEG_HG = (1536, 2048)
SEG_GM = (2048, 3072)
SEG_GH = (3072, 4096)


def _params(*sem):
    return pltpu.CompilerParams(dimension_semantics=sem, vmem_limit_bytes=VMEM_LIMIT_BYTES)


def _rms(x, g):
    return x * lax.rsqrt(jnp.mean(x * x, axis=-1, keepdims=True) + EPS) * g


def _dot(a, b):
    return jnp.dot(a, b, preferred_element_type=F32)


def _dot_nt(a, b):
    return lax.dot_general(a, b, (((1,), (1,)), ((), ())), preferred_element_type=F32)


def _dot_tn(a, b):
    return lax.dot_general(a, b, (((0,), (0,)), ((), ())), preferred_element_type=F32)


def _split2(x):
    hi = x.astype(BF16)
    lo = (x - hi.astype(F32)).astype(BF16)
    return hi, lo


def _split3(x):
    hi = x.astype(BF16)
    r = x - hi.astype(F32)
    mid = r.astype(BF16)
    lo = (r - mid.astype(F32)).astype(BF16)
    return hi, mid, lo


def _const_spec(shape):
    nd = len(shape)
    return pl.BlockSpec(shape, lambda *_: (0,) * nd, pipeline_mode=pl.Buffered(1))


def _rows_to_tiles(x):
    return x.reshape(x.shape[0], SUBLANES, LANES)


def _tiles_to_rows(x3):
    return x3.reshape(x3.shape[0], SUBLANES * LANES)


def _lower_bound(lbt_ref):
    t0 = lbt_ref[0:1, :]
    t1 = lbt_ref[1:2, :]
    mx = jnp.maximum(t0, t1)
    e0 = jnp.exp(t0 - mx)
    return e0 / (e0 + jnp.exp(t1 - mx))


def _rope(x, cos, sin):
    half = MLA_ROPE // 2
    width = x.shape[1]
    slot_lane = lax.broadcasted_iota(I32, x.shape, 1) & (SLOT - 1)
    partner = jnp.where(slot_lane < MLA_NOPE + half, -pltpu.roll(x, width - half, axis=1),
                        pltpu.roll(x, half, axis=1))
    return x * cos + partner * sin


def _in_proj_kernel(x_ref, g_ref, wl_ref, wp_ref, wr_ref, qn_ref, kvn_ref, wq_ref, wqs_ref, wk_ref, wvt_ref,
                    cos_ref, sin_ref, bg_ref, lbt_ref,
                    q_out, k_out, vt_out, hq_out, lf_out, hi_out, hg_out, gm_out, gh_out):
    u = _rms(x_ref[...], g_ref[...]).astype(BF16)

    def lat(s):
        return _dot(u, wl_ref[:, s[0]:s[1]])

    def seg(s):
        return _dot(u, wr_ref[:, s[0]:s[1]])

    cos = cos_ref[...]
    sin = sin_ref[...]
    cos_t = jnp.tile(cos, (1, MLA_HEADS))
    sin_t = jnp.tile(sin, (1, MLA_HEADS))
    scale = (MLA_NOPE + MLA_ROPE) ** -0.5 * LOG2_E

    cq = lat(SEG_CQ)
    ckv = lat(SEG_CKV)
    k_pe = _dot(u, wp_ref[...])
    hq, hf, hi, hg = seg(SEG_HQ), seg(SEG_HF), seg(SEG_HI), seg(SEG_HG)

    cqn = _rms(cq, qn_ref[...]).astype(BF16)
    ckvn = _rms(ckv, kvn_ref[...]).astype(BF16)
    gm, gh = seg(SEG_GM), seg(SEG_GH)
    q = _dot(cqn, wq_ref[...]) * cos_t + _dot(cqn, wqs_ref[...]) * sin_t
    k = _dot(ckvn, wk_ref[...])
    vt = _dot_nt(wvt_ref[...], ckvn)

    lb = _lower_bound(lbt_ref)
    hq_out[...] = jax.nn.silu(hq).astype(BF16)
    lf_out[...] = jnp.log(lb + (1.0 - lb) * jax.nn.sigmoid(hf)) * LOG2_E
    hi_out[...] = hi.astype(BF16)
    hg_out[...] = jax.nn.silu(hg).astype(BF16)
    gm_out[...] = jax.nn.sigmoid(gm + bg_ref[0:1, :]).astype(BF16)
    gh_out[...] = jax.nn.sigmoid(gh + bg_ref[1:2, :]).astype(BF16)

    q_out[...] = (q * scale).astype(BF16)
    k_out[...] = (k + jnp.tile(_rope(k_pe, cos, sin), (1, MLA_HEADS))).astype(BF16)
    slot_row = lax.broadcasted_iota(I32, vt.shape, 0) & (SLOT - 1)
    vt = jnp.where(slot_row == ONE_LANE, 1.0, vt).astype(BF16)
    tv = vt_out.shape[2]
    for piece in range(vt_out.shape[0]):
        vt_out[piece] = vt[:, piece * tv:(piece + 1) * tv]


def _in_proj(x2, tm, tiles_per_seq, cos, sin, wts):
    n, d = x2.shape
    wide = MLA_HEADS * SLOT
    hw = HG_HEADS * HG_DK
    row = lambda width: pl.BlockSpec((tm, width), lambda i: (i, 0))
    tab = pl.BlockSpec((tm, SLOT), lambda i: (i % tiles_per_seq, 0))
    consts = (wts["attn_norm"], wts["w_lat"], wts["w_kpe"], wts["w_rest"], wts["q_norm"], wts["kv_norm"],
              wts["wq"], wts["wq_sw"], wts["wk"], wts["wv_t"])
    tail = (wts["b_gate"], wts["lb_table"])
    in_specs = ([row(d)] + [_const_spec(c.shape) for c in consts] + [tab, tab]
                + [_const_spec(c.shape) for c in tail])
    widths = (wide, wide, None, hw, hw, hw, hw, d, d)
    dtypes = (BF16, BF16, BF16, BF16, F32, BF16, BF16, BF16, BF16)
    tv = min(tm, ATTN_TILE)
    vt_spec = pl.BlockSpec((tm // tv, wide, tv), lambda i: (i, 0, 0))
    return pl.pallas_call(
        _in_proj_kernel,
        grid=(n // tm,),
        in_specs=in_specs,
        out_specs=[vt_spec if w is None else row(w) for w in widths],
        out_shape=[jax.ShapeDtypeStruct((n // tv, wide, tv) if w is None else (n, w), t)
                   for w, t in zip(widths, dtypes)],
        compiler_params=_params("parallel"),
        name="in_proj",
    )(x2, *consts, cos, sin, *tail)


def _attn_kernel(q_ref, k_ref, vt_ref, km_ref, vtm_ref, w1_ref, w3_ref, w2_ref,
                 o_ref, w13_out, w2_out, m_ref, acc_ref):
    i = pl.program_id(1)
    tq = q_ref.shape[0]

    w13_out[:, :, 0:D_EXPERT] = w1_ref[...].astype(BF16)
    w13_out[:, :, D_EXPERT:2 * D_EXPERT] = w3_ref[...].astype(BF16)
    w2_out[...] = w2_ref[...].astype(BF16)

    slots = [slice(h * SLOT, (h + 1) * SLOT) for h in range(MLA_HEADS)]

    def head_set(hs, k_of, vt_of, mask, first):
        s = [_dot_nt(k_of(slots[h]), q_ref[:, slots[h]]) for h in hs]
        if mask is not None:
            s = [jnp.where(mask, x, -jnp.inf) for x in s]
        mx = [jnp.max(x, axis=0, keepdims=True) for x in s]
        if first:
            m_new = mx
        else:
            m_old = [m_ref[h] for h in hs]
            m_new = [jnp.maximum(a, b) for a, b in zip(m_old, mx)]
        p = [jnp.exp2(x - m).astype(BF16) for x, m in zip(s, m_new)]
        for n, h in enumerate(hs):
            m_ref[h] = m_new[n]
            pv = _dot(vt_of(slots[h]), p[n])
            acc_ref[h] = pv if first else jnp.exp2(m_old[n] - m_new[n]) * acc_ref[h] + pv

    def tile(k_of, vt_of, mask, first=False):
        for h0 in range(0, MLA_HEADS, ATTN_HEAD_SET):
            head_set(range(h0, h0 + ATTN_HEAD_SET), k_of, vt_of, mask, first)

    tile(lambda sl: km_ref[:, sl], lambda sl: vtm_ref[0, sl, :], None, first=True)

    def x_tile(j, mask):
        rows = pl.ds(pl.multiple_of(j * tq, tq), tq)
        tile(lambda sl: k_ref[rows, sl], lambda sl: vt_ref[j, sl, :], mask)

    def body(j, carry):
        x_tile(j, None)
        return carry

    lax.fori_loop(0, i, body, 0)
    x_tile(i, lax.broadcasted_iota(I32, (tq, tq), 0) <= lax.broadcasted_iota(I32, (tq, tq), 1))

    outs = []
    for h in range(MLA_HEADS):
        acc = acc_ref[h]
        outs.append(acc[:MLA_V, :] / acc[ONE_LANE:ONE_LANE + 1, :])
    o_ref[...] = jnp.concatenate(outs, axis=0).T.astype(BF16)


def _attention(q, k, vt, k_meta, vt_meta, w1, w3, w2, batch, seq, tq):
    wide = MLA_HEADS * SLOT
    nq = seq // tq
    assert vt.shape == (batch * nq, wide, tq)
    n_exp, d, d_e = w1.shape
    assert n_exp % (batch * nq) == 0, "every grid step narrows the same number of experts"
    per_step = n_exp // (batch * nq)
    experts = lambda shape: pl.BlockSpec((per_step,) + shape, lambda b, i: (b * nq + i, 0, 0))
    return pl.pallas_call(
        _attn_kernel,
        grid=(batch, nq),
        in_specs=[
            pl.BlockSpec((tq, wide), lambda b, i: (b * nq + i, 0)),
            pl.BlockSpec((seq, wide), lambda b, i: (b, 0)),
            pl.BlockSpec((nq, wide, tq), lambda b, i: (b, 0, 0)),
            _const_spec(k_meta.shape),
            _const_spec(vt_meta.shape),
            experts((d, d_e)), experts((d, d_e)), experts((d_e, d)),
        ],
        out_specs=[pl.BlockSpec((tq, MLA_HEADS * MLA_V), lambda b, i: (b * nq + i, 0)),
                   experts((d, 2 * d_e)), experts((d_e, d))],
        out_shape=[jax.ShapeDtypeStruct((batch * seq, MLA_HEADS * MLA_V), BF16),
                   jax.ShapeDtypeStruct((n_exp, d, 2 * d_e), BF16),
                   jax.ShapeDtypeStruct((n_exp, d_e, d), BF16)],
        scratch_shapes=[pltpu.VMEM((MLA_HEADS, 1, tq), F32), pltpu.VMEM((MLA_HEADS, SLOT, tq), F32)],
        compiler_params=_params("parallel", "arbitrary"),
        name="mla_attention",
    )(q, k, vt, k_meta, vt_meta, w1, w3, w2)


def _cumsum_rows(tril, lf):
    parts = _split3(lf)
    return _dot(tril, parts[0]) + _dot(tril, parts[1]) + _dot(tril, parts[2])


def _tril(n):
    return (lax.broadcasted_iota(I32, (n, n), 0) >= lax.broadcasted_iota(I32, (n, n), 1)).astype(BF16)


def _hgrn_meta_kernel(lf_ref, hi_ref, s_out):
    tril = _tril(lf_ref.shape[0])
    for h in range(HG_HEADS):
        sl = slice(h * HG_DK, (h + 1) * HG_DK)
        lf = lf_ref[:, sl]
        b = _cumsum_rows(tril, lf)
        kdec = (1.0 - jnp.exp2(lf)) * jnp.exp2(b[-1:, :] - b)
        s_out[h] = _dot_tn(hi_ref[:, sl], kdec.astype(BF16))


def _hgrn_meta_state(lf, hi):
    return pl.pallas_call(
        _hgrn_meta_kernel,
        out_shape=jax.ShapeDtypeStruct((HG_HEADS, HG_DV, HG_DK), F32),
        compiler_params=pltpu.CompilerParams(vmem_limit_bytes=VMEM_LIMIT_BYTES),
        name="hgrn_meta_state",
    )(lf, hi)


def _hgrn_kernel(hq_ref, lf_ref, hi_ref, hg_ref, hgn_ref, s0_ref, o_ref,
                 st_ref, pb_ref, pk_ref, pv_ref):
    c_rows = HG_CHUNK
    n_chunks = hq_ref.shape[0] // c_rows
    st_ref[...] = s0_ref[...]
    pad = jnp.zeros((HG_HEADS, HG_SUB, HG_DK), F32)
    pb_ref[:, 0:HG_SUB, :] = pad
    pk_ref[:, 0:HG_SUB, :] = pad
    pv_ref[:, 0:HG_SUB, :] = pad
    hgn = hgn_ref[...]
    tril = _tril(c_rows)
    ones = jnp.ones((HG_DK, LANES), BF16)
    n_sub = c_rows // HG_SUB
    far = (lax.broadcasted_iota(I32, (c_rows, c_rows), 0) - lax.broadcasted_iota(I32, (c_rows, c_rows), 1)) >= HG_SUB
    heads = range(HG_HEADS)
    cols = [slice(h * HG_DK, (h + 1) * HG_DK) for h in heads]
    stash = slice(HG_SUB, HG_SUB + c_rows)

    def stages(rows, hs):
        lf = {h: lf_ref[rows, cols[h]] for h in hs}
        kk = {h: 1.0 - jnp.exp2(lf[h]) for h in hs}
        b = {h: _cumsum_rows(tril, lf[h]) for h in hs}
        q = {h: hq_ref[rows, cols[h]].astype(F32) for h in hs}
        v_bf = {h: hi_ref[rows, cols[h]] for h in hs}
        st = {h: st_ref[h] for h in hs}
        for h in hs:
            pb_ref[h, stash, :] = b[h]
            pk_ref[h, stash, :] = kk[h]
            pv_ref[h, stash, :] = v_bf[h].astype(F32)

        y = {}
        for h in hs:
            xs = [(q[h] * kk[h]).astype(BF16)]
            for d in range(1, HG_SUB):
                shifted = slice(HG_SUB - d, HG_SUB - d + c_rows)
                xs.append((q[h] * pk_ref[h, shifted, :] * jnp.exp2(b[h] - pb_ref[h, shifted, :])).astype(BF16))
            y[h] = [_dot(jnp.concatenate(xs[d0:d0 + BAND_GROUP], axis=0), ones)
                    for d0 in range(0, HG_SUB, BAND_GROUP)]

        o = {h: _dot_nt((q[h] * jnp.exp2(b[h])).astype(BF16), st[h].astype(BF16)) for h in hs}

        a_off = {}
        zero_rows = lambda r: [jnp.zeros((r, HG_DK), BF16)] if r else []
        for h in hs:
            qps, kps = [], []
            for j in range(n_sub - 1):
                k0, k1 = HG_SUB * j, HG_SUB * (j + 1)
                rj = b[h][k1 - 1:k1, :]
                qp = (q[h][k1:, :] * jnp.exp2(b[h][k1:, :] - rj)).astype(BF16)
                kp = (kk[h][k0:k1, :] * jnp.exp2(rj - b[h][k0:k1, :])).astype(BF16)
                qps.append(jnp.concatenate(zero_rows(k1) + [qp], axis=0))
                kps.append(jnp.concatenate(zero_rows(k0) + [kp] + zero_rows(c_rows - k1), axis=0))
            a = _dot_nt(jnp.concatenate(qps, axis=1), jnp.concatenate(kps, axis=1))
            a_off[h] = jnp.where(far, a, 0.0).astype(BF16)

        for h in hs:
            o[h] = o[h] + _dot(a_off[h], v_bf[h])
            for d in range(HG_SUB):
                shifted = slice(HG_SUB - d, HG_SUB - d + c_rows)
                part = y[h][d // BAND_GROUP]
                r0 = (d % BAND_GROUP) * c_rows
                o[h] = o[h] + part[r0:r0 + c_rows, :] * pv_ref[h, shifted, :]

        for h in hs:
            on = o[h] * lax.rsqrt(jnp.mean(o[h] * o[h], axis=-1, keepdims=True) + EPS)
            on = on * hgn[:, cols[h]] * hg_ref[rows, cols[h]].astype(F32)
            o_ref[rows, cols[h]] = on.astype(BF16)

        for h in hs:
            b_end = b[h][c_rows - 1:c_rows, :]
            kdec = kk[h] * jnp.exp2(b_end - b[h])
            st_ref[h] = st[h] * jnp.exp2(b_end) + _dot_tn(v_bf[h], kdec.astype(BF16))

    def chunk(c, carry):
        rows = pl.ds(pl.multiple_of(c * c_rows, c_rows), c_rows)
        for h0 in range(0, HG_HEADS, HEAD_SET):
            stages(rows, tuple(range(h0, h0 + HEAD_SET)))
        return carry

    lax.fori_loop(0, n_chunks, chunk, 0, unroll=4)


def _hgrn(hq, lf, hi, hg, hg_norm, s0, batch, seq):
    hw = HG_HEADS * HG_DK
    seq_spec = pl.BlockSpec((seq, hw), lambda b: (b, 0))
    pad_rows = HG_SUB + HG_CHUNK
    return pl.pallas_call(
        _hgrn_kernel,
        grid=(batch,),
        in_specs=[seq_spec, seq_spec, seq_spec, seq_spec, _const_spec(hg_norm.shape), _const_spec(s0.shape)],
        out_specs=seq_spec,
        out_shape=jax.ShapeDtypeStruct((batch * seq, hw), BF16),
        scratch_shapes=[pltpu.VMEM((HG_HEADS, HG_DV, HG_DK), F32)]
        + [pltpu.VMEM((HG_HEADS, pad_rows, HG_DK), F32)] * 3,
        compiler_params=_params("parallel"),
        name="hgrn2",
    )(hq, lf, hi, hg, hg_norm, s0)


ROUTE_CODE, ROUTE_E1, ROUTE_E2, ROUTE_G1, ROUTE_G2 = range(5)
RANK_BITS = 16
ROW_SUBLANES = 2 * SUBLANES


def _slot(code, pstart_ref):
    return pstart_ref[code >> RANK_BITS] + (code & ((1 << RANK_BITS) - 1))


def _merge_kernel(om_ref, oh_ref, gm_ref, gh_ref, x_ref, wbm_ref, wbh_ref, wo_ref, fn_ref,
                  wr_hi_ref, wr_lo_ref, br_ref, h_out, u_out, route_out, cnt_out, carry_ref):
    i = pl.program_id(0)
    tm, d = x_ref.shape

    @pl.when(i == 0)
    def _():
        carry_ref[...] = jnp.zeros_like(carry_ref)

    big = float(2 * LANES)

    def route_rows(rows):
        n_rows = rows.stop - rows.start
        a = _dot(om_ref[rows, :], wbm_ref[...])
        g = _dot(oh_ref[rows, :], wbh_ref[...])
        merged = gm_ref[rows, :].astype(F32) * a + gh_ref[rows, :].astype(F32) * g
        h1 = x_ref[rows, :] + _dot(merged.astype(BF16), wo_ref[...])
        h_out[rows, :] = h1
        u = _rms(h1, fn_ref[...])
        u_out[rows, 0:SUBLANES, :] = _rows_to_tiles(u)

        u_hi, u_lo = _split2(u)
        logits = (_dot(u_hi, wr_hi_ref[...]) + _dot(u_hi, wr_lo_ref[...]) + _dot(u_lo, wr_hi_ref[...])
                  + br_ref[...])
        lane_r = lax.broadcasted_iota(I32, (n_rows, LANES), 1)
        lane_rf = lane_r.astype(F32)

        def first_max(vals):
            mx = jnp.max(vals, axis=-1, keepdims=True)
            idx = jnp.min(jnp.where(vals == mx, lane_rf, big), axis=-1, keepdims=True)
            return mx, idx

        gl = jnp.where(lane_r < N_GROUPS, logits, -jnp.inf)
        g_max, g_sel = first_max(gl)
        p_sel = 1.0 / jnp.sum(jnp.exp(gl - g_max), axis=-1, keepdims=True)
        lo = N_GROUPS + g_sel * EXPERTS_PER_GROUP
        el = jnp.where((lane_rf >= lo) & (lane_rf < lo + EXPERTS_PER_GROUP), logits, -jnp.inf)
        v1, i1 = first_max(el)
        el2 = jnp.where(lane_rf == i1, -jnp.inf, el)
        v2, i2 = first_max(el2)
        t = jnp.exp(v2 - v1)
        return g_sel, i1, i2, p_sel * (1.0 / (1.0 + t)), p_sel * (t / (1.0 + t))

    slab = tm // MERGE_SLABS
    cols = [route_rows(slice(s * slab, (s + 1) * slab)) for s in range(MERGE_SLABS)]
    g_sel, i1, i2, g1, g2 = [jnp.concatenate(c, axis=0) for c in zip(*cols)]
    lane = lax.broadcasted_iota(I32, (tm, LANES), 1)
    lane_f = lane.astype(F32)

    hit = lane_f == g_sel
    onehot = jnp.where(hit, 1.0, 0.0)
    strict = (lax.broadcasted_iota(I32, (tm, tm), 0) > lax.broadcasted_iota(I32, (tm, tm), 1)).astype(BF16)
    before = _dot(strict, onehot.astype(BF16)) + carry_ref[0:1, :]
    rank = jnp.sum(jnp.where(hit, before, 0.0), axis=-1, keepdims=True)
    total = carry_ref[0:1, :] + jnp.sum(onehot, axis=0, keepdims=True)
    carry_ref[...] = jnp.broadcast_to(total, carry_ref.shape)
    cnt_out[...] = jnp.broadcast_to(total, cnt_out.shape)

    route = jnp.zeros((tm, LANES), F32)
    code = g_sel * float(1 << RANK_BITS) + rank
    for pos, val in ((ROUTE_CODE, code), (ROUTE_E1, i1 - N_GROUPS), (ROUTE_E2, i2 - N_GROUPS),
                     (ROUTE_G1, g1), (ROUTE_G2, g2)):
        route = jnp.where(lane == pos, val, route)
    route_out[...] = route
    record = jnp.concatenate([route, jnp.zeros((tm, d - LANES), F32)], axis=1)
    u_out[:, SUBLANES:ROW_SUBLANES, :] = _rows_to_tiles(record)


def _merge(o_mla, o_hg, gm, gh, x2, wts, tm):
    n, d = x2.shape
    row = lambda width: pl.BlockSpec((tm, width), lambda i: (i, 0))
    consts = (wts["w_br_mla"], wts["w_br_hgrn"], wts["w_out"], wts["ffn_norm"],
              wts["w_router_hi"], wts["w_router_lo"], wts["b_router"])
    return pl.pallas_call(
        _merge_kernel,
        grid=(n // tm,),
        in_specs=[row(o_mla.shape[1]), row(o_hg.shape[1]), row(d), row(d), row(d)]
        + [_const_spec(c.shape) for c in consts],
        out_specs=[row(d), pl.BlockSpec((tm, ROW_SUBLANES, LANES), lambda i: (i, 0, 0)), row(LANES),
                   pl.BlockSpec((8, LANES), lambda i: (0, 0))],
        out_shape=[jax.ShapeDtypeStruct((n, d), F32), jax.ShapeDtypeStruct((n, ROW_SUBLANES, LANES), F32),
                   jax.ShapeDtypeStruct((n, LANES), F32), jax.ShapeDtypeStruct((8, LANES), F32)],
        scratch_shapes=[pltpu.VMEM((8, LANES), F32)],
        compiler_params=_params("arbitrary"),
        name="merge_route",
    )(o_mla, o_hg, gm, gh, x2, *consts)


ROW_UNROLL = 8


def _row_copy(src_ref, src_row, dst_ref, dst_row, sem):
    return pltpu.make_async_copy(src_ref.at[pl.ds(src_row, 1)], dst_ref.at[pl.ds(dst_row, 1)], sem)


def _dispatch_kernel(code_ref, ps_ref, pad_lo_ref, pad_hi_ref, nu_ref, u_ref, xg_out, zero_ref, sem, pad_sem):
    i = pl.program_id(0)
    tm = u_ref.shape[0]
    base = i * tm

    @pl.when(i == 0)
    def _():
        zero_ref[...] = jnp.zeros_like(zero_ref)
        for g in range(N_GROUPS):
            def fill(r, carry):
                _row_copy(zero_ref, 0, xg_out, r, pad_sem).start()
                return carry

            def drain(r, carry):
                _row_copy(zero_ref, 0, xg_out, 0, pad_sem).wait()
                return carry

            lax.fori_loop(pad_lo_ref[g], pad_hi_ref[g], fill, 0)
            lax.fori_loop(pad_lo_ref[g], pad_hi_ref[g], drain, 0)

        def block_copy(b):
            rows = pl.ds(pl.multiple_of(b * MOE_BLOCK, MOE_BLOCK), MOE_BLOCK)
            return pltpu.make_async_copy(zero_ref, xg_out.at[rows], pad_sem)

        def fill_block(b, carry):
            block_copy(b).start()
            return carry

        def drain_block(b, carry):
            block_copy(b).wait()
            return carry

        n_blocks = xg_out.shape[0] // MOE_BLOCK
        lax.fori_loop(nu_ref[0], n_blocks, fill_block, 0)
        lax.fori_loop(nu_ref[0], n_blocks, drain_block, 0)

    def start(g, carry):
        r0 = pl.multiple_of(g * ROW_UNROLL, ROW_UNROLL)
        for k in range(ROW_UNROLL):
            _row_copy(u_ref, r0 + k, xg_out, _slot(code_ref[base + r0 + k], ps_ref), sem).start(priority=k % 2)
        return carry

    def wait(g, carry):
        for _ in range(ROW_UNROLL):
            _row_copy(u_ref, 0, xg_out, 0, sem).wait()
        return carry

    lax.fori_loop(0, tm // ROW_UNROLL, start, 0)
    lax.fori_loop(0, tm // ROW_UNROLL, wait, 0)


def _dispatch(code, p_start, pad_lo, pad_hi, n_used, u3, n_slots, tm):
    n = u3.shape[0]
    tile = u3.shape[1:]
    return pl.pallas_call(
        _dispatch_kernel,
        grid_spec=pltpu.PrefetchScalarGridSpec(
            num_scalar_prefetch=5,
            grid=(n // tm,),
            in_specs=[pl.BlockSpec((tm,) + tile, lambda i, *_: (i, 0, 0))],
            out_specs=pl.BlockSpec(memory_space=pl.ANY),
            scratch_shapes=[pltpu.VMEM((MOE_BLOCK,) + tile, F32), pltpu.SemaphoreType.DMA,
                            pltpu.SemaphoreType.DMA],
        ),
        out_shape=jax.ShapeDtypeStruct((n_slots,) + tile, F32),
        compiler_params=_params("arbitrary"),
        name="moe_dispatch",
    )(code, p_start, pad_lo, pad_hi, n_used, u3)


def _expert_kernel(bg_ref, nu_ref, xg_ref, w13_s, w2_s, y_ref, acc_ref):
    i = pl.program_id(0)
    used = i < nu_ref[0]

    @pl.when(used)
    def _():
        group = bg_ref[i]
        x = _tiles_to_rows(xg_ref[:, 0:SUBLANES, :]).astype(BF16)
        route = _tiles_to_rows(xg_ref[:, SUBLANES:ROW_SUBLANES, :])[:, 0:LANES]
        e1 = route[:, ROUTE_E1:ROUTE_E1 + 1]
        e2 = route[:, ROUTE_E2:ROUTE_E2 + 1]
        g1 = route[:, ROUTE_G1:ROUTE_G1 + 1]
        g2 = route[:, ROUTE_G2:ROUTE_G2 + 1]
        first = (group * EXPERTS_PER_GROUP).astype(F32)
        hidden = []
        for e in range(EXPERTS_PER_GROUP):
            hcat = _dot(x, w13_s[e])
            hidden.append((jax.nn.silu(hcat[:, 0:D_EXPERT]) * hcat[:, D_EXPERT:2 * D_EXPERT]).astype(BF16))
        for e in range(EXPERTS_PER_GROUP):
            is1 = e1 == first + e
            is2 = e2 == first + e
            gate = jnp.where(is1, g1, 0.0) + jnp.where(is2, g2, 0.0)
            part = jnp.where(is1 | is2, gate * _dot(hidden[e], w2_s[e]), 0.0)
            if e == 0:
                acc_ref[...] = part
            else:
                acc_ref[...] += part
        y_ref[...] = _rows_to_tiles(acc_ref[...])

    @pl.when(jnp.logical_not(used))
    def _():
        y_ref[...] = jnp.zeros_like(y_ref)


def _experts(blk_group, n_used, xg, w13, w2):
    r = xg.shape[0]
    d = SUBLANES * LANES
    n_blocks = r // MOE_BLOCK
    last_used = lambda i, nu: jnp.minimum(i, nu[0] - 1)
    group_weights = lambda shape: pl.BlockSpec((EXPERTS_PER_GROUP,) + shape, lambda i, bg, nu: (bg[i], 0, 0))
    return pl.pallas_call(
        _expert_kernel,
        grid_spec=pltpu.PrefetchScalarGridSpec(
            num_scalar_prefetch=2,
            grid=(n_blocks,),
            in_specs=[
                pl.BlockSpec((MOE_BLOCK, ROW_SUBLANES, LANES), lambda i, bg, nu: (last_used(i, nu), 0, 0)),
                group_weights((d, 2 * D_EXPERT)),
                group_weights((D_EXPERT, d)),
            ],
            out_specs=pl.BlockSpec((MOE_BLOCK, SUBLANES, LANES), lambda i, bg, nu: (i, 0, 0)),
            scratch_shapes=[pltpu.VMEM((MOE_BLOCK, d), F32)],
        ),
        out_shape=jax.ShapeDtypeStruct((r, SUBLANES, LANES), F32),
        compiler_params=_params("arbitrary"),
        name="moe_experts",
    )(blk_group, n_used, xg, w13, w2)


def _combine_kernel(code_ref, ps_ref, y_ref, h_ref, fn_ref, o_ref, buf_ref, sem):
    tm = h_ref.shape[0]
    base = pl.program_id(0) * tm

    def start(g, carry):
        r0 = pl.multiple_of(g * ROW_UNROLL, ROW_UNROLL)
        for k in range(ROW_UNROLL):
            _row_copy(y_ref, _slot(code_ref[base + r0 + k], ps_ref), buf_ref, r0 + k, sem).start(priority=k % 2)
        return carry

    def wait(g, carry):
        for _ in range(ROW_UNROLL):
            _row_copy(y_ref, 0, buf_ref, 0, sem).wait()
        return carry

    lax.fori_loop(0, tm // ROW_UNROLL, start, 0)
    lax.fori_loop(0, tm // ROW_UNROLL, wait, 0)
    o_ref[...] = _rms(h_ref[...] + _tiles_to_rows(buf_ref[...]), fn_ref[...])


def _combine(code, p_start, y3, h1, final_norm, tm):
    n, d = h1.shape
    return pl.pallas_call(
        _combine_kernel,
        grid_spec=pltpu.PrefetchScalarGridSpec(
            num_scalar_prefetch=2,
            grid=(n // tm,),
            in_specs=[
                pl.BlockSpec(memory_space=pl.ANY),
                pl.BlockSpec((tm, d), lambda i, *_: (i, 0)),
                pl.BlockSpec((1, d), lambda i, *_: (0, 0)),
            ],
            out_specs=pl.BlockSpec((tm, d), lambda i, *_: (i, 0)),
            scratch_shapes=[pltpu.VMEM((tm, SUBLANES, LANES), F32), pltpu.SemaphoreType.DMA],
        ),
        out_shape=jax.ShapeDtypeStruct((n, d), F32),
        compiler_params=_params("arbitrary"),
        name="moe_combine",
    )(code, p_start, y3, h1, final_norm)


def _prepare_weights(attn_norm, w_in, q_norm, w_uq, kv_norm, w_ukv, b_gate, hg_norm, w_br_mla, w_br_hgrn,
                     w_out, ffn_norm, w_group, b_group, w_route, b_route, final_norm, lb_table):
    d = w_in.shape[0]
    half = MLA_ROPE // 2
    pe0, pe1 = Q_LORA + KV_LORA, Q_LORA + KV_LORA + MLA_ROPE
    zeros = lambda r, c: jnp.zeros((r, c), F32)
    w_lat = w_in[:, :pe0].astype(BF16)
    w_kpe = jnp.concatenate([zeros(d, MLA_NOPE), w_in[:, pe0:pe1], zeros(d, SLOT - MLA_NOPE - MLA_ROPE)],
                            axis=1).astype(BF16)
    w_rest = w_in[:, pe1:].astype(BF16)

    wq3 = w_uq.reshape(Q_LORA, MLA_HEADS, MLA_NOPE + MLA_ROPE)
    zq = jnp.zeros((Q_LORA, MLA_HEADS, SLOT - MLA_NOPE - MLA_ROPE), F32)
    wq = jnp.concatenate([wq3, zq], axis=-1).reshape(Q_LORA, MLA_HEADS * SLOT).astype(BF16)
    q_nope, q_pe = wq3[..., :MLA_NOPE], wq3[..., MLA_NOPE:]
    wq_sw = jnp.concatenate([jnp.zeros_like(q_nope), -q_pe[..., half:], q_pe[..., :half], zq],
                            axis=-1).reshape(Q_LORA, MLA_HEADS * SLOT).astype(BF16)

    wkv3 = w_ukv.reshape(KV_LORA, MLA_HEADS, MLA_NOPE + MLA_V)
    zk = jnp.zeros((KV_LORA, MLA_HEADS, SLOT - MLA_NOPE), F32)
    wk = jnp.concatenate([wkv3[..., :MLA_NOPE], zk], axis=-1).reshape(KV_LORA, MLA_HEADS * SLOT).astype(BF16)
    zv = jnp.zeros((KV_LORA, MLA_HEADS, SLOT - MLA_V), F32)
    wv_t = jnp.concatenate([wkv3[..., MLA_NOPE:], zv], axis=-1).reshape(KV_LORA, MLA_HEADS * SLOT).T.astype(BF16)

    w_router = jnp.concatenate([w_group, w_route, zeros(d, LANES - N_GROUPS - N_EXPERTS)], axis=1)
    wr_hi = w_router.astype(BF16)
    wr_lo = (w_router - wr_hi.astype(F32)).astype(BF16)
    b_router = jnp.concatenate([b_group, b_route, jnp.zeros((LANES - N_GROUPS - N_EXPERTS,), F32)])[None, :]
    return {
        "attn_norm": attn_norm[None, :], "w_lat": w_lat, "w_kpe": w_kpe, "w_rest": w_rest,
        "q_norm": q_norm[None, :], "kv_norm": kv_norm[None, :],
        "wq": wq, "wq_sw": wq_sw, "wk": wk, "wv_t": wv_t, "b_gate": b_gate, "lb_table": lb_table,
        "hg_norm": hg_norm[None, :], "w_br_mla": w_br_mla.astype(BF16), "w_br_hgrn": w_br_hgrn.astype(BF16),
        "w_out": w_out.astype(BF16), "ffn_norm": ffn_norm[None, :], "w_router_hi": wr_hi, "w_router_lo": wr_lo,
        "b_router": b_router, "final_norm": final_norm[None, :],
    }


def _rope_tables(length):
    inv = ROPE_BASE ** (-jnp.arange(0, MLA_ROPE, 2, dtype=F32) / MLA_ROPE)
    ang = jnp.arange(length, dtype=F32)[:, None] * inv[None, :]
    cos, sin = jnp.cos(ang), jnp.sin(ang)
    ones = jnp.ones((length, MLA_NOPE), F32)
    tail = SLOT - MLA_NOPE - MLA_ROPE
    cos_t = jnp.concatenate([ones, cos, cos, jnp.ones((length, tail), F32)], axis=1)
    sin_t = jnp.concatenate([0.0 * ones, sin, sin, jnp.zeros((length, tail), F32)], axis=1)
    return cos_t, sin_t


def kernel(x, meta_tokens, attn_norm, w_in, q_norm, w_uq, kv_norm, w_ukv, lb_table, hg_norm, w_br_mla, w_br_hgrn, b_gate, w_out, ffn_norm, w_group, b_group, w_route, b_route, w1, w3, w2, final_norm):
    batch, seq, d = x.shape
    assert attn_norm.shape[0] == 1, "one layer"
    assert d == SUBLANES * LANES, "a token row is moved as one (8, 128) tile"
    n = batch * seq
    x2 = x.reshape(n, d)
    wts = _prepare_weights(attn_norm[0], w_in[0], q_norm[0], w_uq[0], kv_norm[0], w_ukv[0], b_gate[0],
                           hg_norm[0], w_br_mla[0], w_br_hgrn[0], w_out[0], ffn_norm[0], w_group[0],
                           b_group[0], w_route[0], b_route[0], final_norm, lb_table)
    cos_t, sin_t = _rope_tables(N_META + seq)

    meta = _in_proj(meta_tokens.astype(x.dtype), N_META, 1, cos_t[:N_META], sin_t[:N_META], wts)
    s0 = _hgrn_meta_state(meta[4], meta[5])

    tm = min(ROW_TILE, seq)
    q, k, vt, hq, lf, hi, hg, gm, gh = _in_proj(x2, tm, seq // tm, cos_t[N_META:], sin_t[N_META:], wts)
    o_mla, w13_bf, w2_bf = _attention(q, k, vt, meta[1], meta[2], w1[0], w3[0], w2[0], batch, seq,
                                      min(ATTN_TILE, seq))
    o_hg = _hgrn(hq, lf, hi, hg, wts["hg_norm"], s0, batch, seq)
    h1, u3, route, counts = _merge(o_mla, o_hg, gm, gh, x2, wts, tm)

    code = route[:, ROUTE_CODE].astype(I32)
    cnt = counts[0, :N_GROUPS].astype(I32)
    padded = (cnt + MOE_BLOCK - 1) // MOE_BLOCK * MOE_BLOCK
    p_end = jnp.cumsum(padded)
    p_start = p_end - padded
    n_blocks = -(-n // MOE_BLOCK) + N_GROUPS
    blk_row = jnp.arange(n_blocks, dtype=I32) * MOE_BLOCK
    blk_group = jnp.minimum(jnp.sum((p_end[None, :] <= blk_row[:, None]).astype(I32), axis=1), N_GROUPS - 1)
    n_used = p_end[-1:] // MOE_BLOCK

    tmove = min(MOVE_TILE, seq)
    xg = _dispatch(code, p_start, p_start + cnt, p_end, n_used, u3, n_blocks * MOE_BLOCK, tmove)
    y3 = _experts(blk_group, n_used, xg, w13_bf, w2_bf)
    out = _combine(code, p_start, y3, h1, wts["final_norm"], tmove)
    return out.reshape(batch, seq, d)
```
